```python
import math
import jax, jax.numpy as jnp
from jax import lax
import numpy as np

D_MODEL = 1024
BATCH = 8
SEQ = 4096
DEPTH = 1

MEM_LEN = 256
A_HEADS = 8
A_HEAD_DIM = 64
A_PAIRS = ((128, 1), (512, 4), (2048, 16))
B_HEADS = 4
B_QK_DIM = 64
B_V_DIM = 2 * B_QK_DIM
A_WIDTH = A_HEADS * A_HEAD_DIM
B_QK_WIDTH = B_HEADS * 2 * B_QK_DIM
B_V_WIDTH = B_HEADS * B_V_DIM
MIX_WIDTH = A_WIDTH + B_V_WIDTH
IN_WIDTH = 3 * A_WIDTH + 2 * B_QK_WIDTH + B_V_WIDTH
Q_BLOCK = 128
ROPE_THETA = 500000.0
ROT_DIM = A_HEAD_DIM // 4
M_HEADS = 4
M_HEAD_DIM = D_MODEL // M_HEADS
N_EXPERTS = 256
TOP_K = 8
N_GROUPS = 8
TOPK_GROUPS = 4
EXPERT_DIM = 256
SHARED_DIM = 256
ROUTED_SCALE = 2.5
MOE_BLOCK = 128
LN_EPS = 1e-5
SUBLN_EPS = 1e-5
DN_ALPHA = (2 * DEPTH) ** 0.25
DN_BETA = (8 * DEPTH) ** -0.25

kernel_name = 'hybrid_dilated_diff_moe_block'


def layer_norm(x, g, b):
    xf = x.astype(jnp.float32)
    mu = xf.mean(-1, keepdims=True)
    var = jnp.square(xf - mu).mean(-1, keepdims=True)
    return ((xf - mu) * lax.rsqrt(var + LN_EPS) * g.astype(jnp.float32) + b.astype(jnp.float32)).astype(x.dtype)


def rope_tables(positions):
    inv = ROPE_THETA ** (-jnp.arange(0, ROT_DIM, 2, dtype=jnp.float32) / ROT_DIM)
    ang = positions.astype(jnp.float32)[..., None] * inv
    return jnp.cos(ang), jnp.sin(ang)


def apply_partial_rope(t, cos, sin):
    half = cos.shape[-1]
    bshape = cos.shape[:2] + (1,) * (t.ndim - 3) + (half,)
    c = cos.reshape(bshape).astype(t.dtype)
    s = sin.reshape(bshape).astype(t.dtype)
    t1, t2, tp = t[..., :half], t[..., half:2 * half], t[..., 2 * half:]
    return jnp.concatenate([t1 * c - t2 * s, t2 * c + t1 * s, tp], axis=-1)


def dilated_window_branch(q, k, v, window, dilation):
    Bn, Sn, H, E = q.shape
    blk = window // dilation
    L = Sn // dilation
    nblk = -(-L // blk)
    Lp = nblk * blk

    def to_sub(t):
        t = t.reshape(Bn, L, dilation, H, E).transpose(0, 2, 1, 3, 4)
        t = jnp.pad(t, ((0, 0), (0, 0), (0, Lp - L), (0, 0), (0, 0)))
        return t.reshape(Bn, dilation, nblk, blk, H, E)

    def band(t):
        prev = jnp.pad(t[:, :, :-1], ((0, 0), (0, 0), (1, 0), (0, 0), (0, 0), (0, 0)))
        return jnp.concatenate([prev, t], axis=3)

    qs = to_sub(q)
    kb, vb = band(to_sub(k)), band(to_sub(v))
    s = jnp.einsum('brnqhe,brnkhe->brnhqk', qs, kb).astype(jnp.float32)
    qi = jnp.arange(blk)[:, None]
    ki = jnp.arange(2 * blk)[None, :]
    dist = blk + qi - ki
    n_idx = jnp.arange(nblk)[:, None, None]
    valid = (dist >= 0) & (dist <= blk) & ((n_idx - 1) * blk + ki >= 0)
    s = jnp.where(valid[:, None], s, -jnp.inf)
    m = s.max(-1, keepdims=True)
    p = jnp.exp(s - m)
    den = p.sum(-1, keepdims=True)
    o = jnp.einsum('brnhqk,brnkhe->brnqhe', p, vb.astype(jnp.float32)) / jnp.swapaxes(den, 3, 4)
    lse = jnp.swapaxes((m + jnp.log(den))[..., 0], 3, 4)
    o = o.reshape(Bn, dilation, Lp, H, E)[:, :, :L].transpose(0, 2, 1, 3, 4).reshape(Bn, Sn, H, E)
    lse = lse.reshape(Bn, dilation, Lp, H)[:, :, :L].transpose(0, 2, 1, 3).reshape(Bn, Sn, H)
    return o, lse


def dilated_mixture_attention(q, k, v):
    outs, lses = [], []
    for window, dilation in A_PAIRS:
        o, lse = dilated_window_branch(q, k, v, window, dilation)
        outs.append(o)
        lses.append(lse)
    wts = jax.nn.softmax(jnp.stack(lses), axis=0)
    return jnp.einsum('pbsh,pbshe->bshe', wts, jnp.stack(outs)).astype(q.dtype)


def diff_attention(q, k, v, lam, lambda_init, subln_g):
    Bn, Sn, H, _, E = q.shape
    nq = Sn // Q_BLOCK
    qb = q.reshape(Bn, nq, Q_BLOCK, H, 2, E).transpose(1, 0, 2, 3, 4, 5)
    kpos = jnp.arange(Sn)
    vf = v.astype(jnp.float32)

    def block(args):
        qblk, start = args
        s = jnp.einsum('bqhce,bkhce->bhcqk', qblk, k).astype(jnp.float32)
        qpos = start + jnp.arange(Q_BLOCK)
        s = jnp.where(kpos[None, :] <= qpos[:, None], s, -jnp.inf)
        p = jax.nn.softmax(s, axis=-1)
        a = p[:, :, 0] - lam * p[:, :, 1]
        return jnp.einsum('bhqk,bkhe->bqhe', a, vf)

    o = lax.map(block, (qb, jnp.arange(nq) * Q_BLOCK))
    o = o.transpose(1, 0, 2, 3, 4).reshape(Bn, Sn, H, 2 * E)
    o = o * lax.rsqrt(jnp.mean(jnp.square(o), -1, keepdims=True) + SUBLN_EPS) * subln_g.astype(jnp.float32)
    return (o * (1.0 - lambda_init)).astype(q.dtype)


def hybrid_mixer(h, cos, sin, w_in, w_out, lq1, lk1, lq2, lk2, subln_g, lambda_init):
    Bn, Sn, _ = h.shape
    proj = h @ w_in
    qa, ka, va, qb, kb, vb = jnp.split(
        proj, [A_WIDTH, 2 * A_WIDTH, 3 * A_WIDTH, 3 * A_WIDTH + B_QK_WIDTH, 3 * A_WIDTH + 2 * B_QK_WIDTH], axis=-1)
    qa = apply_partial_rope(qa.reshape(Bn, Sn, A_HEADS, A_HEAD_DIM), cos, sin) * (A_HEAD_DIM ** -0.5)
    ka = apply_partial_rope(ka.reshape(Bn, Sn, A_HEADS, A_HEAD_DIM), cos, sin)
    va = va.reshape(Bn, Sn, A_HEADS, A_HEAD_DIM)
    oa = dilated_mixture_attention(qa, ka, va).reshape(Bn, Sn, A_WIDTH)
    qb = apply_partial_rope(qb.reshape(Bn, Sn, B_HEADS, 2, B_QK_DIM), cos, sin) * (B_QK_DIM ** -0.5)
    kb = apply_partial_rope(kb.reshape(Bn, Sn, B_HEADS, 2, B_QK_DIM), cos, sin)
    vb = vb.reshape(Bn, Sn, B_HEADS, B_V_DIM)
    lam = (jnp.exp(jnp.sum(lq1.astype(jnp.float32) * lk1.astype(jnp.float32)))
           - jnp.exp(jnp.sum(lq2.astype(jnp.float32) * lk2.astype(jnp.float32))) + lambda_init)
    ob = diff_attention(qb, kb, vb, lam, lambda_init, subln_g).reshape(Bn, Sn, B_V_WIDTH)
    return jnp.concatenate([oa, ob], axis=-1) @ w_out


def memory_cross_attention(h, mem, wq, wkv, wo):
    Bn, Sn, D = h.shape
    q = (h @ wq).reshape(Bn, Sn, M_HEADS, M_HEAD_DIM)
    kv = (mem @ wkv).reshape(Bn, mem.shape[1], 2, M_HEADS, M_HEAD_DIM)
    s = jnp.einsum('bqhe,bkhe->bhqk', q, kv[:, :, 0]).astype(jnp.float32) * (M_HEAD_DIM ** -0.5)
    p = jax.nn.softmax(s, axis=-1)
    o = jnp.einsum('bhqk,bkhe->bqhe', p.astype(h.dtype), kv[:, :, 1]).reshape(Bn, Sn, D)
    return o @ wo


def swiglu(x, wg, wu, wd):
    return (jax.nn.silu(x @ wg) * (x @ wu)) @ wd


def moe_ffn(h, w_router, e_bias, w_gate, w_up, w_down, ws_gate, ws_up, ws_down):
    Bn, Sn, D = h.shape
    xt = h.reshape(-1, D)
    T = xt.shape[0]
    scores = jax.nn.sigmoid((xt @ w_router).astype(jnp.float32))
    biased = scores + e_bias.astype(jnp.float32)
    group_score = lax.top_k(biased.reshape(T, N_GROUPS, -1), 2)[0].sum(-1)
    top_groups = lax.top_k(group_score, TOPK_GROUPS)[1]
    gmask = jnp.any(top_groups[..., None] == jnp.arange(N_GROUPS), axis=1)
    masked = jnp.where(jnp.repeat(gmask, N_EXPERTS // N_GROUPS, axis=1), biased, -jnp.inf)
    idx = lax.top_k(masked, TOP_K)[1]
    w = jnp.take_along_axis(scores, idx, axis=1)
    w = w / w.sum(-1, keepdims=True) * ROUTED_SCALE
    A = T * TOP_K
    flat_e = idx.reshape(A)
    flat_tok = jnp.repeat(jnp.arange(T, dtype=jnp.int32), TOP_K)
    flat_w = w.reshape(A)
    order = jnp.argsort(flat_e)
    se, stok, sw = flat_e[order], flat_tok[order], flat_w[order]
    counts = jnp.bincount(flat_e, length=N_EXPERTS)
    starts = jnp.cumsum(counts) - counts
    padded = (counts + MOE_BLOCK - 1) // MOE_BLOCK * MOE_BLOCK
    pend = jnp.cumsum(padded)
    pstart = pend - padded
    dest = pstart[se] + jnp.arange(A) - starts[se]
    n_rows = (-(-A // MOE_BLOCK) + N_EXPERTS) * MOE_BLOCK
    nb = n_rows // MOE_BLOCK
    row_tok = jnp.zeros((n_rows,), jnp.int32).at[dest].set(stok)
    row_w = jnp.zeros((n_rows,), jnp.float32).at[dest].set(sw)
    block_e = jnp.minimum(jnp.searchsorted(pend, jnp.arange(nb) * MOE_BLOCK, side='right'), N_EXPERTS - 1)

    def expert_block(args):
        tok, wt, e = args
        xb = xt[tok]
        return swiglu(xb, w_gate[e], w_up[e], w_down[e]).astype(jnp.float32) * wt[:, None]

    y = lax.map(expert_block, (row_tok.reshape(nb, MOE_BLOCK), row_w.reshape(nb, MOE_BLOCK), block_e))
    routed = jnp.zeros((T, D), jnp.float32).at[row_tok].add(y.reshape(n_rows, D))
    shared = swiglu(xt, ws_gate, ws_up, ws_down).astype(jnp.float32)
    return (routed + shared).astype(h.dtype).reshape(Bn, Sn, D)


def setup_inputs(seed: int = 0) -> dict:
    key = jax.random.key(seed)
    ks = jax.random.split(key, 32)
    D, E, F = D_MODEL, N_EXPERTS, EXPERT_DIM
    nrm = lambda k, shape, scale: jax.random.normal(k, shape, jnp.float32) * scale
    x = nrm(ks[0], (BATCH, SEQ, D), 1.0)
    mem = nrm(ks[1], (BATCH, MEM_LEN, D), 1.0)
    offset = jax.random.randint(ks[2], (BATCH, 1), 0, 1024, dtype=jnp.int32)
    positions = (offset + jnp.arange(SEQ, dtype=jnp.int32)[None, :]).astype(jnp.int32)
    col_scale = jnp.concatenate([
        jnp.ones((2 * A_WIDTH,), jnp.float32), jnp.full((A_WIDTH,), DN_BETA, jnp.float32),
        jnp.ones((2 * B_QK_WIDTH,), jnp.float32), jnp.full((B_V_WIDTH,), DN_BETA, jnp.float32)])
    w_in = nrm(ks[3], (DEPTH, D, IN_WIDTH), D ** -0.5) * col_scale
    w_out = nrm(ks[4], (DEPTH, MIX_WIDTH, D), MIX_WIDTH ** -0.5 * DN_BETA)
    lambda_q1 = nrm(ks[5], (DEPTH, B_QK_DIM), 0.1)
    lambda_k1 = nrm(ks[6], (DEPTH, B_QK_DIM), 0.1)
    lambda_q2 = nrm(ks[7], (DEPTH, B_QK_DIM), 0.1)
    lambda_k2 = nrm(ks[8], (DEPTH, B_QK_DIM), 0.1)
    subln_g = 1.0 + nrm(ks[9], (DEPTH, B_V_DIM), 0.02)
    ln1_g = 1.0 + nrm(ks[10], (DEPTH, D), 0.02)
    ln1_b = nrm(ks[11], (DEPTH, D), 0.02)
    wq_mem = nrm(ks[12], (DEPTH, D, D), D ** -0.5)
    wkv_mem = nrm(ks[13], (DEPTH, D, 2 * D), D ** -0.5)
    wo_mem = nrm(ks[14], (DEPTH, D, D), D ** -0.5 * DN_BETA)
    ln2_g = 1.0 + nrm(ks[15], (DEPTH, D), 0.02)
    ln2_b = nrm(ks[16], (DEPTH, D), 0.02)
    w_router = nrm(ks[17], (DEPTH, D, E), D ** -0.5)
    e_bias = nrm(ks[18], (DEPTH, E), 0.01)
    w_gate = nrm(ks[19], (DEPTH, E, D, F), D ** -0.5)
    w_up = nrm(ks[20], (DEPTH, E, D, F), D ** -0.5)
    w_down = nrm(ks[21], (DEPTH, E, F, D), F ** -0.5 * DN_BETA)
    ws_gate = nrm(ks[22], (DEPTH, D, SHARED_DIM), D ** -0.5)
    ws_up = nrm(ks[23], (DEPTH, D, SHARED_DIM), D ** -0.5)
    ws_down = nrm(ks[24], (DEPTH, SHARED_DIM, D), SHARED_DIM ** -0.5 * DN_BETA)
    ln3_g = 1.0 + nrm(ks[25], (DEPTH, D), 0.02)
    ln3_b = nrm(ks[26], (DEPTH, D), 0.02)
    return {'x': x, 'mem': mem, 'positions': positions, 'w_in': w_in, 'w_out': w_out,
            'lambda_q1': lambda_q1, 'lambda_k1': lambda_k1, 'lambda_q2': lambda_q2, 'lambda_k2': lambda_k2,
            'subln_g': subln_g, 'ln1_g': ln1_g, 'ln1_b': ln1_b,
            'wq_mem': wq_mem, 'wkv_mem': wkv_mem, 'wo_mem': wo_mem, 'ln2_g': ln2_g, 'ln2_b': ln2_b,
            'w_router': w_router, 'e_bias': e_bias, 'w_gate': w_gate, 'w_up': w_up, 'w_down': w_down,
            'ws_gate': ws_gate, 'ws_up': ws_up, 'ws_down': ws_down, 'ln3_g': ln3_g, 'ln3_b': ln3_b}


def reference(x, mem, positions, w_in, w_out, lambda_q1, lambda_k1, lambda_q2, lambda_k2,
              subln_g, ln1_g, ln1_b, wq_mem, wkv_mem, wo_mem, ln2_g, ln2_b,
              w_router, e_bias, w_gate, w_up, w_down, ws_gate, ws_up, ws_down, ln3_g, ln3_b):
    cos, sin = rope_tables(positions)
    h = x
    for l in range(DEPTH):
        lambda_init = 0.8 - 0.6 * math.exp(-0.3 * l)
        mix = hybrid_mixer(h, cos, sin, w_in[l], w_out[l], lambda_q1[l], lambda_k1[l],
                           lambda_q2[l], lambda_k2[l], subln_g[l], lambda_init)
        h = layer_norm(DN_ALPHA * h + mix, ln1_g[l], ln1_b[l])
        h = layer_norm(DN_ALPHA * h + memory_cross_attention(h, mem, wq_mem[l], wkv_mem[l], wo_mem[l]),
                       ln2_g[l], ln2_b[l])
        h = layer_norm(DN_ALPHA * h + moe_ffn(h, w_router[l], e_bias[l], w_gate[l], w_up[l], w_down[l],
                                              ws_gate[l], ws_up[l], ws_down[l]),
                       ln3_g[l], ln3_b[l])
    return h
```

```python
import functools
import math

import jax
import jax.numpy as jnp
from jax import lax
from jax.experimental import pallas as pl
from jax.experimental.pallas import tpu as pltpu

A_HEADS = 8
A_HEAD_DIM = 64
A_PAIRS = ((128, 1), (512, 4), (2048, 16))
B_HEADS = 4
B_QK_DIM = 64
B_V_DIM = 2 * B_QK_DIM
A_WIDTH = A_HEADS * A_HEAD_DIM
B_QK_WIDTH = B_HEADS * 2 * B_QK_DIM
B_V_WIDTH = B_HEADS * B_V_DIM
ROPE_THETA = 500000.0
ROT_DIM = A_HEAD_DIM // 4
M_HEADS = 4
N_EXPERTS = 256
TOP_K = 8
N_GROUPS = 8
TOPK_GROUPS = 4
ROUTED_SCALE = 2.5
LN_EPS = 1e-5
SUBLN_EPS = 1e-5
DEPTH = 1
DN_ALPHA = (2 * DEPTH) ** 0.25

LANES = 128
BAND = 128
VMEM_LIMIT = 56 * 1024 * 1024

BF16 = jnp.bfloat16
F32 = jnp.float32
NEG_INF = float("-inf")


def _cparams(sem):
    return pltpu.CompilerParams(dimension_semantics=sem, vmem_limit_bytes=VMEM_LIMIT)


def _dot(a, b):
    return jnp.dot(a, b, preferred_element_type=F32)


def _dot_nt(a, b):
    return lax.dot_general(a, b, (((1,), (1,)), ((), ())), preferred_element_type=F32)


def _layer_norm(x, g, b):
    mu = jnp.mean(x, axis=-1, keepdims=True)
    xc = x - mu
    var = jnp.mean(xc * xc, axis=-1, keepdims=True)
    return xc * lax.rsqrt(var + LN_EPS) * g + b


def _pack_halves(a, b):
    ua = lax.bitcast_convert_type(a.astype(BF16).astype(F32), jnp.uint32)
    ub = lax.bitcast_convert_type(b.astype(BF16).astype(F32), jnp.uint32)
    return (ua >> 16) | (ub & jnp.uint32(0xFFFF0000))


def _unpack_halves(w):
    a = lax.bitcast_convert_type(w << 16, F32)
    b = lax.bitcast_convert_type(w & jnp.uint32(0xFFFF0000), F32)
    return a, b


def _in_proj_kernel(x_ref, w_ref, c_ref, s1_ref, s2_ref, qa_ref, ka_ref, va_ref, qb_ref, kb_ref, vb_ref):
    x = x_ref[...].astype(BF16)
    cc, s1, s2 = c_ref[...], s1_ref[...], s2_ref[...]

    def rope(t):
        outs = []
        for c in range(t.shape[1] // LANES):
            tc = t[:, c * LANES:(c + 1) * LANES]
            nxt = pltpu.roll(tc, LANES - ROT_DIM // 2, 1)
            prv = pltpu.roll(tc, ROT_DIM // 2, 1)
            outs.append(tc * cc + nxt * s1 + prv * s2)
        return jnp.concatenate(outs, axis=1)

    def proj(lo, width):
        return _dot(x, w_ref[:, lo:lo + width])

    o = 0
    qa_ref[...] = (rope(proj(o, A_WIDTH)) * (A_HEAD_DIM ** -0.5)).astype(BF16); o += A_WIDTH
    ka_ref[...] = rope(proj(o, A_WIDTH)).astype(BF16); o += A_WIDTH
    va_ref[...] = proj(o, A_WIDTH).astype(BF16); o += A_WIDTH
    qb_ref[...] = (rope(proj(o, B_QK_WIDTH)) * (B_QK_DIM ** -0.5)).astype(BF16); o += B_QK_WIDTH
    kb_ref[...] = rope(proj(o, B_QK_WIDTH)).astype(BF16); o += B_QK_WIDTH
    vb_ref[...] = proj(o, B_V_WIDTH).astype(BF16)


def _in_proj(x2, w_in, cc, s1, s2, tm):
    T, D = x2.shape
    W = w_in.shape[1]
    row = lambda i: (i, 0)
    outs = [jax.ShapeDtypeStruct((T, A_WIDTH), BF16)] * 3 + [
        jax.ShapeDtypeStruct((T, B_QK_WIDTH), BF16)] * 2 + [jax.ShapeDtypeStruct((T, B_V_WIDTH), BF16)]
    return pl.pallas_call(
        _in_proj_kernel,
        grid=(T // tm,),
        in_specs=[pl.BlockSpec((tm, D), row), pl.BlockSpec((D, W), lambda i: (0, 0)),
                  pl.BlockSpec((tm, LANES), row), pl.BlockSpec((tm, LANES), row), pl.BlockSpec((tm, LANES), row)],
        out_specs=[pl.BlockSpec((tm, o.shape[1]), row) for o in outs],
        out_shape=outs,
        compiler_params=_cparams(("parallel",)),
        name="in_proj",
    )(x2, w_in, cc, s1, s2)


def _dilated_kernel(q_ref, kp_ref, kc_ref, vp_ref, vc_ref, o_ref, lse_ref):
    n = pl.program_id(1)
    qi = lax.broadcasted_iota(jnp.int32, (BAND, BAND), 0)
    kj = lax.broadcasted_iota(jnp.int32, (BAND, BAND), 1)
    prev_ok = kj >= qi + jnp.where(n > 0, 0, BAND)
    cur_ok = kj <= qi
    lane = lax.broadcasted_iota(jnp.int32, (BAND, LANES), 1)
    stats = jnp.zeros((BAND, LANES), F32)
    for c in range(A_WIDTH // LANES):
        sl = slice(c * LANES, (c + 1) * LANES)
        q, kp, kc, vp, vc = q_ref[0, :, sl], kp_ref[0, :, sl], kc_ref[0, :, sl], vp_ref[0, :, sl], vc_ref[0, :, sl]
        halves = []
        for hh in range(LANES // A_HEAD_DIM):
            in_head = (lane >= hh * A_HEAD_DIM) & (lane < (hh + 1) * A_HEAD_DIM)
            qm = jnp.where(in_head, q, jnp.zeros_like(q))
            sp = jnp.where(prev_ok, _dot_nt(qm, kp), NEG_INF)
            sc = jnp.where(cur_ok, _dot_nt(qm, kc), NEG_INF)
            m = jnp.maximum(jnp.max(sp, axis=1, keepdims=True), jnp.max(sc, axis=1, keepdims=True))
            pp, pc = jnp.exp(sp - m), jnp.exp(sc - m)
            den = jnp.sum(pp, axis=1, keepdims=True) + jnp.sum(pc, axis=1, keepdims=True)
            o = (_dot(pp.astype(BF16), vp) + _dot(pc.astype(BF16), vc)) / den
            halves.append((in_head, o))
            h = c * (LANES // A_HEAD_DIM) + hh
            stats = jnp.where(lane == h, m + jnp.log(den), stats)
        out = halves[0][1]
        for in_head, o in halves[1:]:
            out = jnp.where(in_head, o, out)
        o_ref[0, :, sl] = out.astype(BF16)
    lse_ref[0] = stats


def _dilated(q, k, v):
    G, L, W = q.shape
    cur = lambda g, n: (g, n, 0)
    prev = lambda g, n: (g, jnp.maximum(n - 1, 0), 0)
    blk = (1, BAND, W)
    return pl.pallas_call(
        _dilated_kernel,
        grid=(G, L // BAND),
        in_specs=[pl.BlockSpec(blk, cur), pl.BlockSpec(blk, prev), pl.BlockSpec(blk, cur),
                  pl.BlockSpec(blk, prev), pl.BlockSpec(blk, cur)],
        out_specs=[pl.BlockSpec(blk, cur), pl.BlockSpec((1, BAND, LANES), cur)],
        out_shape=[jax.ShapeDtypeStruct((G, L, W), BF16), jax.ShapeDtypeStruct((G, L, LANES), F32)],
        compiler_params=_cparams(("parallel", "parallel")),
        name="dilated",
    )(q, k, k, v, v)


def _diff_kernel(lq1_ref, lk1_ref, lq2_ref, lk2_ref, g_ref, q_ref, k_ref, v_ref, o_ref, *, tq, lambda_init):
    iq = pl.program_id(2)
    q = q_ref[0]
    lane = lax.broadcasted_iota(jnp.int32, q.shape, 1)
    zero = jnp.zeros_like(q)
    qs = (jnp.where(lane < B_QK_DIM, q, zero), jnp.where(lane >= B_QK_DIM, q, zero))

    def block(j, carry, masked):
        kb = k_ref[0, pl.ds(pl.multiple_of(j * tq, tq), tq), :]
        vb = v_ref[0, pl.ds(pl.multiple_of(j * tq, tq), tq), :]
        out = []
        for c in range(2):
            m, l, acc = carry[c]
            s = _dot_nt(qs[c], kb)
            if masked:
                ri = lax.broadcasted_iota(jnp.int32, s.shape, 0)
                ci = lax.broadcasted_iota(jnp.int32, s.shape, 1)
                s = jnp.where(ci <= ri, s, NEG_INF)
            m_new = jnp.maximum(m, jnp.max(s, axis=1, keepdims=True))
            a = jnp.exp(m - m_new)
            p = jnp.exp(s - m_new)
            l = a * l + jnp.sum(p, axis=1, keepdims=True)
            acc = a * acc + _dot(p.astype(BF16), vb)
            out.append((m_new, l, acc))
        return tuple(out)

    init = tuple((jnp.full((tq, 1), NEG_INF, F32), jnp.zeros((tq, 1), F32), jnp.zeros((tq, B_V_DIM), F32))
                 for _ in range(2))
    carry = lax.fori_loop(0, iq, lambda j, cr: block(j, cr, False), init)
    (_, l0, a0), (_, l1, a1) = block(iq, carry, True)

    lam = (jnp.exp(jnp.sum(lq1_ref[...] * lk1_ref[...], axis=1, keepdims=True))
           - jnp.exp(jnp.sum(lq2_ref[...] * lk2_ref[...], axis=1, keepdims=True)) + lambda_init)
    o = a0 / l0 - lam * (a1 / l1)
    o = o * lax.rsqrt(jnp.mean(o * o, axis=-1, keepdims=True) + SUBLN_EPS) * g_ref[...]
    o_ref[0] = (o * (1.0 - lambda_init)).astype(BF16)


def _diff_attention(qb, kb, vb, lq1, lk1, lq2, lk2, subln_g, lambda_init, tq):
    B, S, _ = qb.shape
    vec = lambda n: pl.BlockSpec((1, n), lambda b, h, i: (0, 0))
    return pl.pallas_call(
        functools.partial(_diff_kernel, tq=tq, lambda_init=lambda_init),
        grid=(B, B_HEADS, S // tq),
        in_specs=[vec(B_QK_DIM)] * 4 + [vec(B_V_DIM),
                  pl.BlockSpec((1, tq, B_V_DIM), lambda b, h, i: (b, i, h)),
                  pl.BlockSpec((1, S, B_V_DIM), lambda b, h, i: (b, 0, h)),
                  pl.BlockSpec((1, S, B_V_DIM), lambda b, h, i: (b, 0, h))],
        out_specs=pl.BlockSpec((1, tq, B_V_DIM), lambda b, h, i: (b, i, h)),
        out_shape=jax.ShapeDtypeStruct((B, S, B_V_WIDTH), BF16),
        compiler_params=_cparams(("parallel", "parallel", "arbitrary")),
        name="diff_attn",
    )(lq1, lk1, lq2, lk2, subln_g, qb, kb, vb)


def _mix_out_kernel(o1_ref, o2_ref, o3_ref, l1_ref, l2_ref, l3_ref, ob_ref, x_ref, w_ref, e_ref, g_ref, b_ref, h_ref):
    ls = (l1_ref[...], l2_ref[...], l3_ref[...])
    mx = jnp.maximum(jnp.maximum(ls[0], ls[1]), ls[2])
    es = [jnp.exp(l - mx) for l in ls]
    den = es[0] + es[1] + es[2]
    e = e_ref[...]
    oa = None
    for ex, o_ref in zip(es, (o1_ref, o2_ref, o3_ref)):
        wt = ex / den
        hi = wt.astype(BF16)
        lo = (wt - hi.astype(F32)).astype(BF16)
        wexp = _dot(hi, e) + _dot(lo, e)
        term = wexp * o_ref[...].astype(F32)
        oa = term if oa is None else oa + term
    y = _dot(oa.astype(BF16), w_ref[:A_WIDTH, :]) + _dot(ob_ref[...], w_ref[A_WIDTH:, :])
    h_ref[...] = _layer_norm(DN_ALPHA * x_ref[...] + y, g_ref[...], b_ref[...])


def _mix_out(os_, ls_, ob, x2, w_out, expand, g, b, tm):
    T, D = x2.shape
    row = lambda i: (i, 0)
    const = lambda i: (0, 0)
    return pl.pallas_call(
        _mix_out_kernel,
        grid=(T // tm,),
        in_specs=[pl.BlockSpec((tm, A_WIDTH), row)] * 3 + [pl.BlockSpec((tm, LANES), row)] * 3 + [
            pl.BlockSpec((tm, B_V_WIDTH), row), pl.BlockSpec((tm, D), row),
            pl.BlockSpec(w_out.shape, const), pl.BlockSpec(expand.shape, const),
            pl.BlockSpec((1, D), const), pl.BlockSpec((1, D), const)],
        out_specs=pl.BlockSpec((tm, D), row),
        out_shape=jax.ShapeDtypeStruct((T, D), F32),
        compiler_params=_cparams(("parallel",)),
        name="mix_out",
    )(*os_, *ls_, ob, x2, w_out, expand, g, b)


def _matmul_kernel(a_ref, w_ref, o_ref):
    o_ref[...] = _dot(a_ref[...].astype(BF16), w_ref[...]).astype(o_ref.dtype)


def _mem_kv(mem2, wkv, tm):
    R, D = mem2.shape
    N = wkv.shape[1]
    return pl.pallas_call(
        _matmul_kernel,
        grid=(R // tm,),
        in_specs=[pl.BlockSpec((tm, D), lambda i: (i, 0)), pl.BlockSpec((D, N), lambda i: (0, 0))],
        out_specs=pl.BlockSpec((tm, N), lambda i: (i, 0)),
        out_shape=jax.ShapeDtypeStruct((R, N), BF16),
        compiler_params=_cparams(("parallel",)),
        name="mem_kv",
    )(mem2, wkv)


def _mem_attn_kernel(h_ref, kv_ref, wq_ref, wo_ref, g_ref, b_ref, out_ref, packed_ref):
    h = h_ref[...]
    D = h.shape[1]
    hd = D // M_HEADS
    q = (_dot(h.astype(BF16), wq_ref[...]) * (hd ** -0.5)).astype(BF16)
    heads = []
    for i in range(M_HEADS):
        k = kv_ref[:, i * hd:(i + 1) * hd]
        v = kv_ref[:, D + i * hd:D + (i + 1) * hd]
        s = _dot_nt(q[:, i * hd:(i + 1) * hd], k)
        p = jnp.exp(s - jnp.max(s, axis=1, keepdims=True))
        p = p / jnp.sum(p, axis=1, keepdims=True)
        heads.append(_dot(p.astype(BF16), v).astype(BF16))
    o = jnp.concatenate(heads, axis=1)
    h2 = _layer_norm(DN_ALPHA * h + _dot(o, wo_ref[...]), g_ref[...], b_ref[...])
    out_ref[...] = h2
    packed_ref[...] = _pack_halves(h2[:, :D // 2], h2[:, D // 2:])


def _mem_attn(h1, kv, wq, wo, g, b, tm, tiles_per_batch):
    T, D = h1.shape
    M = kv.shape[0] // (T // (tm * tiles_per_batch))
    row = lambda i: (i, 0)
    const = lambda i: (0, 0)
    return pl.pallas_call(
        _mem_attn_kernel,
        grid=(T // tm,),
        in_specs=[pl.BlockSpec((tm, D), row), pl.BlockSpec((M, 2 * D), lambda i: (i // tiles_per_batch, 0)),
                  pl.BlockSpec((D, D), const), pl.BlockSpec((D, D), const),
                  pl.BlockSpec((1, D), const), pl.BlockSpec((1, D), const)],
        out_specs=[pl.BlockSpec((tm, D), row), pl.BlockSpec((tm, D // 2), row)],
        out_shape=[jax.ShapeDtypeStruct((T, D), F32), jax.ShapeDtypeStruct((T, D // 2), jnp.uint32)],
        compiler_params=_cparams(("parallel",)),
        name="mem_attn",
    )(h1, kv, wq, wo, g, b)


def _argmax_rows(v, row_iota, n_rows):
    m = jnp.max(v, axis=0, keepdims=True)
    idx = jnp.min(jnp.where(v == m, row_iota, n_rows), axis=0, keepdims=True)
    return m, idx


def _router_kernel(h_ref, whi_ref, wlo_ref, bias_ref, idx_ref, w_ref, rank_ref, cnt_ref, carry_ref):
    @pl.when(pl.program_id(0) == 0)
    def _():
        carry_ref[...] = jnp.zeros_like(carry_ref)

    h = h_ref[...]
    tt = h.shape[0]
    xhi = h.astype(BF16)
    xlo = (h - xhi.astype(F32)).astype(BF16)
    whi, wlo = whi_ref[...], wlo_ref[...]
    logits = _dot_nt(whi, xhi) + (_dot_nt(wlo, xhi) + _dot_nt(whi, xlo))
    scores = jax.nn.sigmoid(logits)
    biased = scores + bias_ref[...]

    gsz = N_EXPERTS // N_GROUPS
    giota = lax.broadcasted_iota(jnp.int32, (gsz, tt), 0)
    gscore = []
    for g in range(N_GROUPS):
        blk = biased[g * gsz:(g + 1) * gsz]
        m1, i1 = _argmax_rows(blk, giota, gsz)
        m2 = jnp.max(jnp.where(giota == i1, NEG_INF, blk), axis=0, keepdims=True)
        gscore.append(m1 + m2)
    gscore = jnp.concatenate(gscore, axis=0)
    riota = lax.broadcasted_iota(jnp.int32, (N_GROUPS, tt), 0)
    gsel = jnp.zeros((N_GROUPS, tt), F32)
    for _ in range(TOPK_GROUPS):
        _, gi = _argmax_rows(gscore, riota, N_GROUPS)
        hit = riota == gi
        gsel = jnp.where(hit, 1.0, gsel)
        gscore = jnp.where(hit, NEG_INF, gscore)
    masked = jnp.concatenate(
        [jnp.where(gsel[g:g + 1] > 0.5, biased[g * gsz:(g + 1) * gsz], NEG_INF) for g in range(N_GROUPS)], axis=0)

    eiota = lax.broadcasted_iota(jnp.int32, (N_EXPERTS, tt), 0)
    chosen = jnp.zeros((N_EXPERTS, tt), F32)
    idxs, ws = [], []
    for _ in range(TOP_K):
        _, ei = _argmax_rows(masked, eiota, N_EXPERTS)
        hit = eiota == ei
        idxs.append(ei)
        ws.append(jnp.sum(jnp.where(hit, scores, 0.0), axis=0, keepdims=True))
        chosen = jnp.where(hit, 1.0, chosen)
        masked = jnp.where(hit, NEG_INF, masked)
    idx = jnp.concatenate(idxs, axis=0)
    w = jnp.concatenate(ws, axis=0)
    w = w / jnp.sum(w, axis=0, keepdims=True) * ROUTED_SCALE

    before = (lax.broadcasted_iota(jnp.int32, (tt, tt), 0) < lax.broadcasted_iota(jnp.int32, (tt, tt), 1))
    within = _dot(chosen.astype(BF16), before.astype(BF16))
    erank = within + carry_ref[...]
    ranks = [jnp.sum(jnp.where(eiota == idxs[k], erank, 0.0), axis=0, keepdims=True) for k in range(TOP_K)]
    carry = carry_ref[...] + jnp.sum(chosen, axis=1, keepdims=True)
    carry_ref[...] = carry

    idx_ref[...] = idx
    w_ref[...] = w
    rank_ref[...] = jnp.concatenate(ranks, axis=0).astype(jnp.int32)
    cnt_ref[...] = carry.astype(jnp.int32)


def _router(h2, whi, wlo, bias, tt):
    T, D = h2.shape
    col = lambda i: (0, i)
    const = lambda i: (0, 0)
    return pl.pallas_call(
        _router_kernel,
        grid=(T // tt,),
        in_specs=[pl.BlockSpec((tt, D), lambda i: (i, 0)), pl.BlockSpec((N_EXPERTS, D), const),
                  pl.BlockSpec((N_EXPERTS, D), const), pl.BlockSpec((N_EXPERTS, 1), const)],
        out_specs=[pl.BlockSpec((TOP_K, tt), col), pl.BlockSpec((TOP_K, tt), col), pl.BlockSpec((TOP_K, tt), col),
                   pl.BlockSpec((N_EXPERTS, 1), const)],
        out_shape=[jax.ShapeDtypeStruct((TOP_K, T), jnp.int32), jax.ShapeDtypeStruct((TOP_K, T), F32),
                   jax.ShapeDtypeStruct((TOP_K, T), jnp.int32), jax.ShapeDtypeStruct((N_EXPERTS, 1), jnp.int32)],
        scratch_shapes=[pltpu.VMEM((N_EXPERTS, 1), F32)],
        compiler_params=_cparams(("arbitrary",)),
        name="router",
    )(h2, whi, wlo, bias)


def _dest_kernel(idx_ref, rank_ref, start_ref, dest_ref):
    idx = idx_ref[...]
    tt = idx.shape[1]
    eiota = lax.broadcasted_iota(jnp.int32, (N_EXPERTS, tt), 0)
    start = start_ref[...].astype(F32)
    rows = [jnp.sum(jnp.where(eiota == idx[k:k + 1], start, 0.0), axis=0, keepdims=True) for k in range(TOP_K)]
    dest_ref[...] = jnp.concatenate(rows, axis=0).astype(jnp.int32) + rank_ref[...]


def _dest_rows(idx, rank, start, tt):
    T = idx.shape[1]
    col = lambda i: (0, i)
    return pl.pallas_call(
        _dest_kernel,
        grid=(T // tt,),
        in_specs=[pl.BlockSpec((TOP_K, tt), col), pl.BlockSpec((TOP_K, tt), col),
                  pl.BlockSpec((N_EXPERTS, 1), lambda i: (0, 0))],
        out_specs=pl.BlockSpec((TOP_K, tt), col),
        out_shape=jax.ShapeDtypeStruct((TOP_K, T), jnp.int32),
        compiler_params=_cparams(("parallel",)),
        name="dest_rows",
    )(idx, rank, start)


def _dispatch_kernel(dest_ref, x_ref, xs_ref, sem):
    tt = x_ref.shape[0]

    def row_copy(t, k):
        return pltpu.make_async_copy(x_ref.at[pl.ds(t, 1)], xs_ref.at[pl.ds(dest_ref[k, t], 1)], sem)

    def issue(t, c):
        for k in range(TOP_K):
            row_copy(t, k).start()
        return c

    lax.fori_loop(0, tt, issue, 0)

    def drain(t, c):
        for k in range(TOP_K):
            row_copy(t, k).wait()
        return c

    lax.fori_loop(0, tt, drain, 0)


def _dispatch(dest, packed, n_rows, tt):
    T, W = packed.shape
    return pl.pallas_call(
        _dispatch_kernel,
        grid=(T // tt,),
        in_specs=[pl.BlockSpec((TOP_K, tt), lambda i: (0, i), memory_space=pltpu.SMEM),
                  pl.BlockSpec((tt, W), lambda i: (i, 0))],
        out_specs=pl.BlockSpec(memory_space=pl.ANY),
        out_shape=jax.ShapeDtypeStruct((n_rows, W), jnp.uint32),
        scratch_shapes=[pltpu.SemaphoreType.DMA],
        compiler_params=_cparams(("arbitrary",)),
        name="dispatch",
    )(dest, packed)


def _experts_kernel(te_ref, tv_ref, nu_ref, x_ref, wg_ref, wu_ref, wd_ref, y_ref):
    i = pl.program_id(0)

    @pl.when(i < nu_ref[0])
    def _():
        tm = x_ref.shape[0]
        rows = lax.broadcasted_iota(jnp.int32, (tm, 1), 0)
        xw = jnp.where(rows < tv_ref[i], x_ref[...], jnp.uint32(0))
        xa, xb = _unpack_halves(xw)
        xa, xb = xa.astype(BF16), xb.astype(BF16)
        half = xw.shape[1]
        wg, wu = wg_ref[0].astype(BF16), wu_ref[0].astype(BF16)
        g = _dot(xa, wg[:half]) + _dot(xb, wg[half:])
        u = _dot(xa, wu[:half]) + _dot(xb, wu[half:])
        act = (g * jax.nn.sigmoid(g) * u).astype(BF16)
        y = _dot(act, wd_ref[0].astype(BF16))
        y_ref[...] = _pack_halves(y[:, :half], y[:, half:])


def _experts(tile_e, tile_valid, n_used, xs, w_gate, w_up, w_down, tm):
    n_rows, W = xs.shape
    E, D, F = w_gate.shape
    n_tiles = n_rows // tm

    def tile(i, te, tv, nu):
        return (jnp.minimum(i, nu[0] - 1), 0)

    def expert(i, te, tv, nu):
        return (te[jnp.minimum(i, nu[0] - 1)], 0, 0)

    return pl.pallas_call(
        _experts_kernel,
        grid_spec=pltpu.PrefetchScalarGridSpec(
            num_scalar_prefetch=3,
            grid=(n_tiles,),
            in_specs=[pl.BlockSpec((tm, W), tile), pl.BlockSpec((1, D, F), expert),
                      pl.BlockSpec((1, D, F), expert), pl.BlockSpec((1, F, D), expert)],
            out_specs=pl.BlockSpec((tm, W), tile),
        ),
        out_shape=jax.ShapeDtypeStruct((n_rows, W), jnp.uint32),
        compiler_params=_cparams(("arbitrary",)),
        name="experts",
    )(tile_e, tile_valid, n_used, xs, w_gate, w_up, w_down)


def _combine_kernel(dest_ref, w_ref, h_ref, ys_ref, sg_ref, su_ref, sd_ref, g_ref, b_ref, out_ref, buf, sem):
    tt = h_ref.shape[0]

    def row_copy(t, k):
        return pltpu.make_async_copy(ys_ref.at[pl.ds(dest_ref[k, t], 1)], buf.at[k, pl.ds(t, 1)], sem)

    def issue(t, c):
        for k in range(TOP_K):
            row_copy(t, k).start()
        return c

    lax.fori_loop(0, tt, issue, 0)

    h = h_ref[...]
    hb = h.astype(BF16)
    gate = _dot(hb, sg_ref[...])
    act = (gate * jax.nn.sigmoid(gate) * _dot(hb, su_ref[...])).astype(BF16)
    shared = _dot(act, sd_ref[...])

    def drain(t, c):
        for k in range(TOP_K):
            row_copy(t, k).wait()
        return c

    lax.fori_loop(0, tt, drain, 0)

    w = w_ref[...]
    lo = hi = None
    for k in range(TOP_K):
        a, b = _unpack_halves(buf[k])
        wk = w[:, k:k + 1]
        lo = a * wk if lo is None else lo + a * wk
        hi = b * wk if hi is None else hi + b * wk
    routed = jnp.concatenate([lo, hi], axis=1)
    out_ref[...] = _layer_norm(DN_ALPHA * h + (routed + shared), g_ref[...], b_ref[...])


def _combine(dest, w_tok, h2, ys, sg, su, sd, g, b, tt):
    T, D = h2.shape
    W = ys.shape[1]
    row = lambda i: (i, 0)
    const = lambda i: (0, 0)
    return pl.pallas_call(
        _combine_kernel,
        grid=(T // tt,),
        in_specs=[pl.BlockSpec((TOP_K, tt), lambda i: (0, i), memory_space=pltpu.SMEM),
                  pl.BlockSpec((tt, TOP_K), row), pl.BlockSpec((tt, D), row),
                  pl.BlockSpec(memory_space=pl.ANY),
                  pl.BlockSpec(sg.shape, const), pl.BlockSpec(su.shape, const), pl.BlockSpec(sd.shape, const),
                  pl.BlockSpec((1, D), const), pl.BlockSpec((1, D), const)],
        out_specs=pl.BlockSpec((tt, D), row),
        out_shape=jax.ShapeDtypeStruct((T, D), F32),
        scratch_shapes=[pltpu.VMEM((TOP_K, tt, W), jnp.uint32), pltpu.SemaphoreType.DMA],
        compiler_params=_cparams(("arbitrary",)),
        name="combine",
    )(dest, w_tok, h2, ys, sg, su, sd, g, b)


def _rope_lane_tables(positions):
    half = ROT_DIM // 2
    inv = ROPE_THETA ** (-jnp.arange(0, ROT_DIM, 2, dtype=F32) / ROT_DIM)
    ang = positions.astype(F32).reshape(-1, 1) * inv
    cos, sin = jnp.cos(ang), jnp.sin(ang)
    T = cos.shape[0]
    rest = A_HEAD_DIM - ROT_DIM
    one_head = lambda first, second, fill: jnp.concatenate(
        [first, second, jnp.full((T, rest), fill, F32)], axis=1)
    zeros = jnp.zeros_like(sin)
    cc = one_head(cos, cos, 1.0)
    s1 = one_head(-sin, zeros, 0.0)
    s2 = one_head(zeros, sin, 0.0)
    rep = LANES // A_HEAD_DIM
    return jnp.tile(cc, (1, rep)), jnp.tile(s1, (1, rep)), jnp.tile(s2, (1, rep))


def _to_sub(t, d):
    B, S, W = t.shape
    return t.reshape(B, S // d, d, W).transpose(0, 2, 1, 3).reshape(B * d, S // d, W)


def _from_sub(t, d, B):
    G, L, W = t.shape
    return t.reshape(B, d, L, W).transpose(0, 2, 1, 3).reshape(B * L * d, W)


def _layer(x, mem, cc, s1, s2, l, w_in, w_out, lq1, lk1, lq2, lk2, subln_g, ln1_g, ln1_b,
           wq_mem, wkv_mem, wo_mem, ln2_g, ln2_b, w_router, e_bias, w_gate, w_up, w_down,
           ws_gate, ws_up, ws_down, ln3_g, ln3_b):
    B, S, D = x.shape
    T = B * S
    lambda_init = 0.8 - 0.6 * math.exp(-0.3 * l)
    x2 = x.reshape(T, D)
    tm = min(512, S)

    qa, ka, va, qb, kb, vb = _in_proj(x2, w_in.astype(BF16), cc, s1, s2, tm)

    outs, lses = [], []
    for window, d in A_PAIRS:
        assert window // d == BAND
        q3, k3, v3 = (_to_sub(t.reshape(B, S, A_WIDTH), d) for t in (qa, ka, va))
        o, lse = _dilated(q3, k3, v3)
        outs.append(_from_sub(o, d, B))
        lses.append(_from_sub(lse, d, B))

    row = lambda v: v.reshape(1, -1).astype(F32)
    ob = _diff_attention(qb.reshape(B, S, -1), kb.reshape(B, S, -1), vb.reshape(B, S, -1),
                         row(lq1), row(lk1), row(lq2), row(lk2), row(subln_g), lambda_init, min(256, S))

    expand = (jnp.arange(LANES)[:, None] == (jnp.arange(A_WIDTH)[None, :] // A_HEAD_DIM)).astype(BF16)
    h1 = _mix_out(outs, lses, ob.reshape(T, -1), x2, w_out.astype(BF16), expand, row(ln1_g), row(ln1_b), tm)

    M = mem.shape[1]
    kv = _mem_kv(mem.reshape(B * M, D), wkv_mem.astype(BF16), min(512, B * M))
    h2, packed = _mem_attn(h1, kv, wq_mem.astype(BF16), wo_mem.astype(BF16), row(ln2_g), row(ln2_b), tm, S // tm)

    wr_t = w_router.T
    wr_hi = wr_t.astype(BF16)
    wr_lo = (wr_t - wr_hi.astype(F32)).astype(BF16)
    idx, w_route, rank, counts = _router(h2, wr_hi, wr_lo, e_bias.reshape(-1, 1).astype(F32), tm)

    tile_rows = 256
    counts = counts.reshape(-1)
    padded = (counts + tile_rows - 1) // tile_rows * tile_rows
    pend = jnp.cumsum(padded)
    pstart = pend - padded
    n_tiles = (T * TOP_K) // tile_rows + N_EXPERTS
    n_rows = n_tiles * tile_rows
    tile_lo = jnp.arange(n_tiles, dtype=jnp.int32) * tile_rows
    tile_e = jnp.minimum(jnp.searchsorted(pend, tile_lo, side="right"), N_EXPERTS - 1).astype(jnp.int32)
    tile_valid = jnp.clip(counts[tile_e] - (tile_lo - pstart[tile_e]), 0, tile_rows).astype(jnp.int32)
    n_used = (pend[-1] // tile_rows).astype(jnp.int32).reshape(1)

    dest = _dest_rows(idx, rank, pstart.reshape(-1, 1).astype(jnp.int32), min(2048, T))
    xs = _dispatch(dest, packed, n_rows, min(256, S))
    ys = _experts(tile_e, tile_valid, n_used, xs, w_gate, w_up, w_down, tile_rows)
    out = _combine(dest, w_route.T, h2, ys, ws_gate.astype(BF16), ws_up.astype(BF16), ws_down.astype(BF16),
                   row(ln3_g), row(ln3_b), min(256, S))
    return out.reshape(B, S, D)


def kernel(x, mem, positions, w_in, w_out, lambda_q1, lambda_k1, lambda_q2, lambda_k2, subln_g, ln1_g, ln1_b,
           wq_mem, wkv_mem, wo_mem, ln2_g, ln2_b, w_router, e_bias, w_gate, w_up, w_down, ws_gate, ws_up, ws_down,
           ln3_g, ln3_b):
    cc, s1, s2 = _rope_lane_tables(positions)
    h = x
    for l in range(w_in.shape[0]):
        h = _layer(h, mem, cc, s1, s2, l, w_in[l], w_out[l], lambda_q1[l], lambda_k1[l], lambda_q2[l],
                   lambda_k2[l], subln_g[l], ln1_g[l], ln1_b[l], wq_mem[l], wkv_mem[l], wo_mem[l],
                   ln2_g[l], ln2_b[l], w_router[l], e_bias[l], w_gate[l], w_up[l], w_down[l],
                   ws_gate[l], ws_up[l], ws_down[l], ln3_g[l], ln3_b[l])
    return h
```

```python
import functools
import math

import jax
import jax.numpy as jnp
from jax import lax
from jax.experimental import pallas as pl
from jax.experimental.pallas import tpu as pltpu

A_HEADS = 8
A_HEAD_DIM = 64
A_PAIRS = ((128, 1), (512, 4), (2048, 16))
B_HEADS = 4
B_QK_DIM = 64
B_V_DIM = 2 * B_QK_DIM
A_WIDTH = A_HEADS * A_HEAD_DIM
B_QK_WIDTH = B_HEADS * 2 * B_QK_DIM
B_V_WIDTH = B_HEADS * B_V_DIM
ROPE_THETA = 500000.0
ROT_DIM = A_HEAD_DIM // 4
M_HEADS = 4
N_EXPERTS = 256
TOP_K = 8
N_GROUPS = 8
TOPK_GROUPS = 4
ROUTED_SCALE = 2.5
LN_EPS = 1e-5
SUBLN_EPS = 1e-5
DEPTH = 1
DN_ALPHA = (2 * DEPTH) ** 0.25

LANES = 128
BAND = 128
VMEM_LIMIT = 56 * 1024 * 1024

BF16 = jnp.bfloat16
F32 = jnp.float32
NEG_INF = float("-inf")


def _cparams(sem):
    return pltpu.CompilerParams(dimension_semantics=sem, vmem_limit_bytes=VMEM_LIMIT)


def _dot(a, b):
    return jnp.dot(a, b, preferred_element_type=F32)


def _dot_nt(a, b):
    return lax.dot_general(a, b, (((1,), (1,)), ((), ())), preferred_element_type=F32)


def _layer_norm(x, g, b):
    mu = jnp.mean(x, axis=-1, keepdims=True)
    xc = x - mu
    var = jnp.mean(xc * xc, axis=-1, keepdims=True)
    return xc * lax.rsqrt(var + LN_EPS) * g + b


def _in_proj_kernel(x_ref, w_ref, wvt_ref, c_ref, s1_ref, s2_ref, qa_ref, ka_ref, qb_ref, kb_ref, vat_ref, vbt_ref):
    x = x_ref[...].astype(BF16)
    cc, s1, s2 = c_ref[...], s1_ref[...], s2_ref[...]

    def rope(t):
        outs = []
        for c in range(t.shape[1] // LANES):
            tc = t[:, c * LANES:(c + 1) * LANES]
            nxt = pltpu.roll(tc, LANES - ROT_DIM // 2, 1)
            prv = pltpu.roll(tc, ROT_DIM // 2, 1)
            outs.append(tc * cc + nxt * s1 + prv * s2)
        return jnp.concatenate(outs, axis=1)

    def proj(lo, width):
        return _dot(x, w_ref[:, lo:lo + width])

    o = 0
    qa_ref[...] = (rope(proj(o, A_WIDTH)) * (A_HEAD_DIM ** -0.5)).astype(BF16); o += A_WIDTH
    ka_ref[...] = rope(proj(o, A_WIDTH)).astype(BF16); o += A_WIDTH
    qb_ref[...] = (rope(proj(o, B_QK_WIDTH)) * (B_QK_DIM ** -0.5)).astype(BF16); o += B_QK_WIDTH
    kb_ref[...] = rope(proj(o, B_QK_WIDTH)).astype(BF16)
    vat_ref[0] = _dot_nt(wvt_ref[:A_WIDTH, :], x).astype(BF16)
    vbt_ref[0] = _dot_nt(wvt_ref[A_WIDTH:, :], x).astype(BF16)


def _in_proj(x2, w_qk, w_vt, cc, s1, s2, tm, seq):
    T, D = x2.shape
    nt = seq // tm
    row = lambda i: (i, 0)
    const = lambda i: (0, 0)
    tcol = lambda i: (i // nt, 0, i % nt)
    outs = [jax.ShapeDtypeStruct((T, A_WIDTH), BF16)] * 2 + [jax.ShapeDtypeStruct((T, B_QK_WIDTH), BF16)] * 2
    outs_t = [jax.ShapeDtypeStruct((T // seq, A_WIDTH, seq), BF16), jax.ShapeDtypeStruct((T // seq, B_V_WIDTH, seq), BF16)]
    return pl.pallas_call(
        _in_proj_kernel,
        grid=(T // tm,),
        in_specs=[pl.BlockSpec((tm, D), row), pl.BlockSpec(w_qk.shape, const), pl.BlockSpec(w_vt.shape, const),
                  pl.BlockSpec((tm, LANES), row), pl.BlockSpec((tm, LANES), row), pl.BlockSpec((tm, LANES), row)],
        out_specs=[pl.BlockSpec((tm, o.shape[1]), row) for o in outs] + [
            pl.BlockSpec((1, o.shape[1], tm), tcol) for o in outs_t],
        out_shape=outs + outs_t,
        compiler_params=_cparams(("parallel",)),
        name="in_proj",
    )(x2, w_qk, w_vt, cc, s1, s2)


def _dilated_kernel(q_ref, kp_ref, kc_ref, vtp_ref, vtc_ref, o_ref, lse_ref):
    n = pl.program_id(1)
    tq = q_ref.shape[1]
    nk = BAND + tq
    two = LANES // A_HEAD_DIM
    kr = lax.broadcasted_iota(jnp.int32, (nk, two * tq), 0)
    qi = lax.broadcasted_iota(jnp.int32, (nk, two * tq), 1) % tq
    ok = (kr >= jnp.where(n > 0, qi, jnp.maximum(qi, BAND))) & (kr <= qi + BAND)
    lane = lax.broadcasted_iota(jnp.int32, (tq, LANES), 1)
    row = lax.broadcasted_iota(jnp.int32, (LANES, tq), 0)
    chunks = [slice(c * LANES, (c + 1) * LANES) for c in range(A_WIDTH // LANES)]
    scores = []
    for sl in chunks:
        q = q_ref[0, :, sl]
        zero = jnp.zeros_like(q)
        q2 = jnp.concatenate([jnp.where((lane >= h * A_HEAD_DIM) & (lane < (h + 1) * A_HEAD_DIM), q, zero)
                              for h in range(two)], axis=0)
        kband = jnp.concatenate([kp_ref[0, :, sl], kc_ref[0, :, sl]], axis=0)
        scores.append(_dot_nt(kband, q2))
    probs, lses = [], []
    for st in scores:
        s = jnp.where(ok, st, NEG_INF)
        m = jnp.max(s, axis=0, keepdims=True)
        p = jnp.exp(s - m)
        den = jnp.sum(p, axis=0, keepdims=True)
        probs.append((p.astype(BF16), den))
        lse = m + jnp.log(den)
        lses += [lse[:, h * tq:(h + 1) * tq] for h in range(two)]
    for sl, (p, den) in zip(chunks, probs):
        vt = jnp.concatenate([vtp_ref[0, sl, :], vtc_ref[0, sl, :]], axis=1)
        ot = _dot(vt, p) / den
        out_t = ot[:, :tq]
        for h in range(1, two):
            out_t = jnp.where(row >= h * A_HEAD_DIM, ot[:, h * tq:(h + 1) * tq], out_t)
        o_ref[0, :, sl] = out_t.T.astype(BF16)
    lse_ref[0] = jnp.concatenate(lses, axis=0)


def _dilated(q, k, vt, tq):
    G, L, W = q.shape
    per = tq // BAND
    cur = lambda g, n: (g, n, 0)
    prev = lambda g, n: (g, jnp.maximum(n * per - 1, 0), 0)
    cur_t = lambda g, n: (g, 0, n)
    prev_t = lambda g, n: (g, 0, jnp.maximum(n * per - 1, 0))
    return pl.pallas_call(
        _dilated_kernel,
        grid=(G, L // tq),
        in_specs=[pl.BlockSpec((1, tq, W), cur), pl.BlockSpec((1, BAND, W), prev), pl.BlockSpec((1, tq, W), cur),
                  pl.BlockSpec((1, W, BAND), prev_t), pl.BlockSpec((1, W, tq), cur_t)],
        out_specs=[pl.BlockSpec((1, tq, W), cur), pl.BlockSpec((1, A_HEADS, tq), cur_t)],
        out_shape=[jax.ShapeDtypeStruct((G, L, W), BF16), jax.ShapeDtypeStruct((G, A_HEADS, L), F32)],
        compiler_params=_cparams(("parallel", "parallel")),
        name="dilated",
    )(q, k, k, vt, vt)


def _diff_kernel(lq1_ref, lk1_ref, lq2_ref, lk2_ref, g_ref, q_ref, k_ref, vt_ref, o_ref, acc_ref, *, tq, tk, lambda_init):
    iq = pl.program_id(2)
    q = q_ref[0]
    lane = lax.broadcasted_iota(jnp.int32, q.shape, 1)
    zero = jnp.zeros_like(q)
    q2 = jnp.concatenate([jnp.where(lane < B_QK_DIM, q, zero), jnp.where(lane >= B_QK_DIM, q, zero)], axis=0)
    acc_ref[...] = jnp.zeros_like(acc_ref)

    def block(j, carry, diag):
        start = pl.multiple_of(j * tk, tk)
        kb = k_ref[0, pl.ds(start, tk), :]
        vt = vt_ref[0, :, pl.ds(start, tk)]
        st = _dot_nt(kb, q2)
        if diag is not None:
            ki = lax.broadcasted_iota(jnp.int32, (tk, tq), 0) + diag
            qi = lax.broadcasted_iota(jnp.int32, (tk, tq), 1)
            ok = ki <= qi
        out, work = [], []
        for c in range(2):
            m, l = carry[c]
            s = st[:, c * tq:(c + 1) * tq]
            if diag is not None:
                s = jnp.where(ok, s, NEG_INF)
            m_new = jnp.maximum(m, jnp.max(s, axis=0, keepdims=True))
            a = jnp.exp(m - m_new)
            p = jnp.exp(s - m_new)
            out.append((m_new, a * l + jnp.sum(p, axis=0, keepdims=True)))
            work.append((a, p.astype(BF16)))
        for c, (a, p) in enumerate(work):
            acc_ref[c] = a * acc_ref[c] + _dot(vt, p)
        return tuple(out)

    init = tuple((jnp.full((1, tq), NEG_INF, F32), jnp.zeros((1, tq), F32)) for _ in range(2))
    per_tile = tq // tk
    carry = lax.fori_loop(0, iq * per_tile, lambda j, cr: block(j, cr, None), init)
    for r in range(per_tile):
        carry = block(iq * per_tile + r, carry, r * tk)
    (_, l0), (_, l1) = carry

    lam = (jnp.exp(jnp.sum(lq1_ref[...] * lk1_ref[...], axis=1, keepdims=True))
           - jnp.exp(jnp.sum(lq2_ref[...] * lk2_ref[...], axis=1, keepdims=True)) + lambda_init)
    o = acc_ref[0] / l0 - lam * (acc_ref[1] / l1)
    o = o * lax.rsqrt(jnp.mean(o * o, axis=0, keepdims=True) + SUBLN_EPS) * g_ref[...]
    o_ref[0] = (o * (1.0 - lambda_init)).T.astype(BF16)


def _diff_attention(qb, kb, vbt, lq1, lk1, lq2, lk2, subln_g, lambda_init, tq, tk):
    B, S, _ = qb.shape
    vec = lambda n: pl.BlockSpec((1, n), lambda b, h, i: (0, 0))
    return pl.pallas_call(
        functools.partial(_diff_kernel, tq=tq, tk=tk, lambda_init=lambda_init),
        grid=(B, B_HEADS, S // tq),
        in_specs=[vec(B_QK_DIM)] * 4 + [pl.BlockSpec((B_V_DIM, 1), lambda b, h, i: (0, 0)),
                  pl.BlockSpec((1, tq, B_V_DIM), lambda b, h, i: (b, i, h)),
                  pl.BlockSpec((1, S, B_V_DIM), lambda b, h, i: (b, 0, h)),
                  pl.BlockSpec((1, B_V_DIM, S), lambda b, h, i: (b, h, 0))],
        out_specs=pl.BlockSpec((1, tq, B_V_DIM), lambda b, h, i: (b, i, h)),
        out_shape=jax.ShapeDtypeStruct((B, S, B_V_WIDTH), BF16),
        scratch_shapes=[pltpu.VMEM((2, B_V_DIM, tq), F32)],
        compiler_params=_cparams(("parallel", "parallel", "arbitrary")),
        name="diff_attn",
    )(lq1, lk1, lq2, lk2, subln_g, qb, kb, vbt)


def _mix_out_kernel(o1_ref, o2_ref, o3_ref, l1_ref, l2_ref, l3_ref, ob_ref, x_ref, w_ref, e_ref, g_ref, b_ref, h_ref):
    ls = (l1_ref[...], l2_ref[...], l3_ref[...])
    mx = jnp.maximum(jnp.maximum(ls[0], ls[1]), ls[2])
    es = [jnp.exp(l - mx) for l in ls]
    den = es[0] + es[1] + es[2]
    e = e_ref[...]
    oa = None
    for ex, o_ref in zip(es, (o1_ref, o2_ref, o3_ref)):
        wt = ex / den
        hi = wt.astype(BF16)
        lo = (wt - hi.astype(F32)).astype(BF16)
        wexp = _dot(hi, e) + _dot(lo, e)
        term = wexp * o_ref[...].astype(F32)
        oa = term if oa is None else oa + term
    y = _dot(oa.astype(BF16), w_ref[:A_WIDTH, :]) + _dot(ob_ref[...], w_ref[A_WIDTH:, :])
    h_ref[...] = _layer_norm(DN_ALPHA * x_ref[...] + y, g_ref[...], b_ref[...])


def _mix_out(os_, ls_, ob, x2, w_out, expand, g, b, tm):
    T, D = x2.shape
    row = lambda i: (i, 0)
    const = lambda i: (0, 0)
    return pl.pallas_call(
        _mix_out_kernel,
        grid=(T // tm,),
        in_specs=[pl.BlockSpec((tm, A_WIDTH), row)] * 3 + [pl.BlockSpec((tm, LANES), row)] * 3 + [
            pl.BlockSpec((tm, B_V_WIDTH), row), pl.BlockSpec((tm, D), row),
            pl.BlockSpec(w_out.shape, const), pl.BlockSpec(expand.shape, const),
            pl.BlockSpec((1, D), const), pl.BlockSpec((1, D), const)],
        out_specs=pl.BlockSpec((tm, D), row),
        out_shape=jax.ShapeDtypeStruct((T, D), F32),
        compiler_params=_cparams(("parallel",)),
        name="mix_out",
    )(*os_, *ls_, ob, x2, w_out, expand, g, b)


def _matmul_kernel(a_ref, w_ref, o_ref):
    o_ref[...] = _dot(a_ref[...].astype(BF16), w_ref[...]).astype(o_ref.dtype)


def _mem_kv(mem2, wkv, tm):
    R, D = mem2.shape
    N = wkv.shape[1]
    return pl.pallas_call(
        _matmul_kernel,
        grid=(R // tm,),
        in_specs=[pl.BlockSpec((tm, D), lambda i: (i, 0)), pl.BlockSpec((D, N), lambda i: (0, 0))],
        out_specs=pl.BlockSpec((tm, N), lambda i: (i, 0)),
        out_shape=jax.ShapeDtypeStruct((R, N), BF16),
        compiler_params=_cparams(("parallel",)),
        name="mem_kv",
    )(mem2, wkv)


def _mem_attn_kernel(h_ref, kv_ref, wq_ref, wo_ref, g_ref, b_ref, out_ref):
    h = h_ref[...]
    D = h.shape[1]
    hd = D // M_HEADS
    q = (_dot(h.astype(BF16), wq_ref[...]) * (hd ** -0.5)).astype(BF16)
    heads = []
    for i in range(M_HEADS):
        k = kv_ref[:, i * hd:(i + 1) * hd]
        v = kv_ref[:, D + i * hd:D + (i + 1) * hd]
        s = _dot_nt(q[:, i * hd:(i + 1) * hd], k)
        p = jnp.exp(s - jnp.max(s, axis=1, keepdims=True))
        p = p / jnp.sum(p, axis=1, keepdims=True)
        heads.append(_dot(p.astype(BF16), v).astype(BF16))
    o = jnp.concatenate(heads, axis=1)
    out_ref[...] = _layer_norm(DN_ALPHA * h + _dot(o, wo_ref[...]), g_ref[...], b_ref[...])


def _mem_attn(h1, kv, wq, wo, g, b, tm, tiles_per_batch):
    T, D = h1.shape
    M = kv.shape[0] // (T // (tm * tiles_per_batch))
    row = lambda i: (i, 0)
    const = lambda i: (0, 0)
    return pl.pallas_call(
        _mem_attn_kernel,
        grid=(T // tm,),
        in_specs=[pl.BlockSpec((tm, D), row), pl.BlockSpec((M, 2 * D), lambda i: (i // tiles_per_batch, 0)),
                  pl.BlockSpec((D, D), const), pl.BlockSpec((D, D), const),
                  pl.BlockSpec((1, D), const), pl.BlockSpec((1, D), const)],
        out_specs=pl.BlockSpec((tm, D), row),
        out_shape=jax.ShapeDtypeStruct((T, D), F32),
        compiler_params=_cparams(("parallel",)),
        name="mem_attn",
    )(h1, kv, wq, wo, g, b)


def _argmax_rows(v, row_iota, n_rows):
    m = jnp.max(v, axis=0, keepdims=True)
    idx = jnp.min(jnp.where(v == m, row_iota, n_rows), axis=0, keepdims=True)
    return m, idx


def _router_kernel(h_ref, whi_ref, wlo_ref, bias_ref, idx_ref, w_ref, rank_ref, cnt_ref, carry_ref):
    @pl.when(pl.program_id(0) == 0)
    def _():
        carry_ref[...] = jnp.zeros_like(carry_ref)

    h = h_ref[...]
    tt = h.shape[0]
    xhi = h.astype(BF16)
    xlo = (h - xhi.astype(F32)).astype(BF16)
    whi, wlo = whi_ref[...], wlo_ref[...]
    logits = _dot_nt(whi, xhi) + (_dot_nt(wlo, xhi) + _dot_nt(whi, xlo))
    scores = jax.nn.sigmoid(logits)
    biased = scores + bias_ref[...]

    gsz = N_EXPERTS // N_GROUPS
    giota = lax.broadcasted_iota(jnp.int32, (gsz, tt), 0)
    gscore = []
    for g in range(N_GROUPS):
        blk = biased[g * gsz:(g + 1) * gsz]
        m1, i1 = _argmax_rows(blk, giota, gsz)
        m2 = jnp.max(jnp.where(giota == i1, NEG_INF, blk), axis=0, keepdims=True)
        gscore.append(m1 + m2)
    gscore = jnp.concatenate(gscore, axis=0)
    riota = lax.broadcasted_iota(jnp.int32, (N_GROUPS, tt), 0)
    gsel = jnp.zeros((N_GROUPS, tt), F32)
    for _ in range(TOPK_GROUPS):
        _, gi = _argmax_rows(gscore, riota, N_GROUPS)
        hit = riota == gi
        gsel = jnp.where(hit, 1.0, gsel)
        gscore = jnp.where(hit, NEG_INF, gscore)
    masked = jnp.concatenate(
        [jnp.where(gsel[g:g + 1] > 0.5, biased[g * gsz:(g + 1) * gsz], NEG_INF) for g in range(N_GROUPS)], axis=0)

    eiota = lax.broadcasted_iota(jnp.int32, (N_EXPERTS, tt), 0)
    chosen = jnp.zeros((N_EXPERTS, tt), F32)
    idxs, ws = [], []
    for _ in range(TOP_K):
        _, ei = _argmax_rows(masked, eiota, N_EXPERTS)
        hit = eiota == ei
        idxs.append(ei)
        ws.append(jnp.sum(jnp.where(hit, scores, 0.0), axis=0, keepdims=True))
        chosen = jnp.where(hit, 1.0, chosen)
        masked = jnp.where(hit, NEG_INF, masked)
    idx = jnp.concatenate(idxs, axis=0)
    w = jnp.concatenate(ws, axis=0)
    w = w / jnp.sum(w, axis=0, keepdims=True) * ROUTED_SCALE

    before = (lax.broadcasted_iota(jnp.int32, (tt, tt), 0) < lax.broadcasted_iota(jnp.int32, (tt, tt), 1))
    within = _dot(chosen.astype(BF16), before.astype(BF16))
    erank = within + carry_ref[...]
    ranks = [jnp.sum(jnp.where(eiota == idxs[k], erank, 0.0), axis=0, keepdims=True) for k in range(TOP_K)]
    carry = carry_ref[...] + jnp.sum(chosen, axis=1, keepdims=True)
    carry_ref[...] = carry

    idx_ref[...] = idx
    w_ref[...] = w
    rank_ref[...] = jnp.concatenate(ranks, axis=0).astype(jnp.int32)
    cnt_ref[...] = carry.astype(jnp.int32)


def _router(h2, whi, wlo, bias, tt):
    T, D = h2.shape
    col = lambda i: (0, i)
    const = lambda i: (0, 0)
    return pl.pallas_call(
        _router_kernel,
        grid=(T // tt,),
        in_specs=[pl.BlockSpec((tt, D), lambda i: (i, 0)), pl.BlockSpec((N_EXPERTS, D), const),
                  pl.BlockSpec((N_EXPERTS, D), const), pl.BlockSpec((N_EXPERTS, 1), const)],
        out_specs=[pl.BlockSpec((TOP_K, tt), col), pl.BlockSpec((TOP_K, tt), col), pl.BlockSpec((TOP_K, tt), col),
                   pl.BlockSpec((N_EXPERTS, 1), const)],
        out_shape=[jax.ShapeDtypeStruct((TOP_K, T), jnp.int32), jax.ShapeDtypeStruct((TOP_K, T), F32),
                   jax.ShapeDtypeStruct((TOP_K, T), jnp.int32), jax.ShapeDtypeStruct((N_EXPERTS, 1), jnp.int32)],
        scratch_shapes=[pltpu.VMEM((N_EXPERTS, 1), F32)],
        compiler_params=_cparams(("arbitrary",)),
        name="router",
    )(h2, whi, wlo, bias)


def _dest_kernel(idx_ref, rank_ref, start_ref, dest_ref):
    idx = idx_ref[...]
    tt = idx.shape[1]
    eiota = lax.broadcasted_iota(jnp.int32, (N_EXPERTS, tt), 0)
    start = start_ref[...].astype(F32)
    rows = [jnp.sum(jnp.where(eiota == idx[k:k + 1], start, 0.0), axis=0, keepdims=True) for k in range(TOP_K)]
    dest_ref[...] = jnp.concatenate(rows, axis=0).astype(jnp.int32) + rank_ref[...]


def _dest_rows(idx, rank, start, tt):
    T = idx.shape[1]
    col = lambda i: (0, i)
    return pl.pallas_call(
        _dest_kernel,
        grid=(T // tt,),
        in_specs=[pl.BlockSpec((TOP_K, tt), col), pl.BlockSpec((TOP_K, tt), col),
                  pl.BlockSpec((N_EXPERTS, 1), lambda i: (0, 0))],
        out_specs=pl.BlockSpec((TOP_K, tt), col),
        out_shape=jax.ShapeDtypeStruct((TOP_K, T), jnp.int32),
        compiler_params=_cparams(("parallel",)),
        name="dest_rows",
    )(idx, rank, start)


def _dispatch_kernel(dest_ref, x_ref, xs_ref, sem):
    tt = x_ref.shape[0]

    def issue(t, c):
        for k in range(TOP_K):
            pltpu.make_async_copy(x_ref.at[pl.ds(t, 1)], xs_ref.at[pl.ds(dest_ref[k, t], 1)], sem).start(priority=k % 2)
        return c

    lax.fori_loop(0, tt, issue, 0)
    n = TOP_K * tt
    pltpu.make_async_copy(xs_ref.at[pl.ds(0, n)], xs_ref.at[pl.ds(0, n)], sem).wait()


def _dispatch(dest, x, n_rows, tt):
    T, D = x.shape
    return pl.pallas_call(
        _dispatch_kernel,
        grid=(T // tt,),
        in_specs=[pl.BlockSpec((TOP_K, tt), lambda i: (0, i), memory_space=pltpu.SMEM),
                  pl.BlockSpec((tt, D), lambda i: (i, 0))],
        out_specs=pl.BlockSpec(memory_space=pl.ANY),
        out_shape=jax.ShapeDtypeStruct((n_rows, D), F32),
        scratch_shapes=[pltpu.SemaphoreType.DMA],
        compiler_params=_cparams(("arbitrary",)),
        name="dispatch",
    )(dest, x)


def _experts_kernel(te_ref, tv_ref, nu_ref, x_ref, wg_ref, wu_ref, wd_ref, y_ref, wg_bf, wu_bf, wd_bf):
    i = pl.program_id(0)

    @pl.when(i < nu_ref[0])
    def _():
        @pl.when((i == 0) | (te_ref[i] != te_ref[jnp.maximum(i - 1, 0)]))
        def _():
            wg_bf[...] = wg_ref[0].astype(BF16)
            wu_bf[...] = wu_ref[0].astype(BF16)
            wd_bf[...] = wd_ref[0].astype(BF16)

        tm = x_ref.shape[0]
        rows = lax.broadcasted_iota(jnp.int32, (tm, 1), 0)
        x = jnp.where(rows < tv_ref[i], x_ref[...], 0.0).astype(BF16)
        g = _dot(x, wg_bf[...])
        act = (g * jax.nn.sigmoid(g) * _dot(x, wu_bf[...])).astype(BF16)
        y_ref[...] = _dot(act, wd_bf[...])


def _experts(tile_e, tile_valid, n_used, xs, w_gate, w_up, w_down, tm):
    n_rows, W = xs.shape
    E, D, F = w_gate.shape
    n_tiles = n_rows // tm

    def tile(i, te, tv, nu):
        return (jnp.minimum(i, nu[0] - 1), 0)

    def expert(i, te, tv, nu):
        return (te[jnp.minimum(i, nu[0] - 1)], 0, 0)

    return pl.pallas_call(
        _experts_kernel,
        grid_spec=pltpu.PrefetchScalarGridSpec(
            num_scalar_prefetch=3,
            grid=(n_tiles,),
            in_specs=[pl.BlockSpec((tm, W), tile), pl.BlockSpec((1, D, F), expert),
                      pl.BlockSpec((1, D, F), expert), pl.BlockSpec((1, F, D), expert)],
            out_specs=pl.BlockSpec((tm, W), tile),
            scratch_shapes=[pltpu.VMEM((D, F), BF16), pltpu.VMEM((D, F), BF16), pltpu.VMEM((F, D), BF16)],
        ),
        out_shape=jax.ShapeDtypeStruct((n_rows, W), F32),
        compiler_params=_cparams(("arbitrary",)),
        name="experts",
    )(tile_e, tile_valid, n_used, xs, w_gate, w_up, w_down)


def _combine_kernel(dest_ref, w_ref, h_ref, ys_ref, sg_ref, su_ref, sd_ref, g_ref, b_ref, out_ref, buf, sem):
    tt = h_ref.shape[0]

    def issue(t, c):
        for k in range(TOP_K):
            pltpu.make_async_copy(ys_ref.at[pl.ds(dest_ref[k, t], 1)], buf.at[k, pl.ds(t, 1)], sem).start(priority=k % 2)
        return c

    lax.fori_loop(0, tt, issue, 0)

    h = h_ref[...]
    hb = h.astype(BF16)
    gate = _dot(hb, sg_ref[...])
    act = (gate * jax.nn.sigmoid(gate) * _dot(hb, su_ref[...])).astype(BF16)
    shared = _dot(act, sd_ref[...])

    for k in range(TOP_K):
        pltpu.make_async_copy(ys_ref.at[pl.ds(0, tt)], buf.at[k], sem).wait()

    w = w_ref[...]
    routed = buf[0] * w[:, 0:1]
    for k in range(1, TOP_K):
        routed = routed + buf[k] * w[:, k:k + 1]
    out_ref[...] = _layer_norm(DN_ALPHA * h + (routed + shared), g_ref[...], b_ref[...])


def _combine(dest, w_tok, h2, ys, sg, su, sd, g, b, tt):
    T, D = h2.shape
    W = ys.shape[1]
    row = lambda i: (i, 0)
    const = lambda i: (0, 0)
    return pl.pallas_call(
        _combine_kernel,
        grid=(T // tt,),
        in_specs=[pl.BlockSpec((TOP_K, tt), lambda i: (0, i), memory_space=pltpu.SMEM),
                  pl.BlockSpec((tt, TOP_K), row), pl.BlockSpec((tt, D), row),
                  pl.BlockSpec(memory_space=pl.ANY),
                  pl.BlockSpec(sg.shape, const), pl.BlockSpec(su.shape, const), pl.BlockSpec(sd.shape, const),
                  pl.BlockSpec((1, D), const), pl.BlockSpec((1, D), const)],
        out_specs=pl.BlockSpec((tt, D), row),
        out_shape=jax.ShapeDtypeStruct((T, D), F32),
        scratch_shapes=[pltpu.VMEM((TOP_K, tt, W), F32), pltpu.SemaphoreType.DMA],
        compiler_params=_cparams(("arbitrary",)),
        name="combine",
    )(dest, w_tok, h2, ys, sg, su, sd, g, b)


def _rope_lane_tables(positions):
    half = ROT_DIM // 2
    inv = ROPE_THETA ** (-jnp.arange(0, ROT_DIM, 2, dtype=F32) / ROT_DIM)
    ang = positions.astype(F32).reshape(-1, 1) * inv
    cos, sin = jnp.cos(ang), jnp.sin(ang)
    T = cos.shape[0]
    rest = A_HEAD_DIM - ROT_DIM
    one_head = lambda first, second, fill: jnp.concatenate(
        [first, second, jnp.full((T, rest), fill, F32)], axis=1)
    zeros = jnp.zeros_like(sin)
    cc = one_head(cos, cos, 1.0)
    s1 = one_head(-sin, zeros, 0.0)
    s2 = one_head(zeros, sin, 0.0)
    rep = LANES // A_HEAD_DIM
    return jnp.tile(cc, (1, rep)), jnp.tile(s1, (1, rep)), jnp.tile(s2, (1, rep))


def _to_sub(t, d):
    B, S, W = t.shape
    return t.reshape(B, S // d, d, W).transpose(0, 2, 1, 3).reshape(B * d, S // d, W)


def _from_sub(t, d, B):
    G, L, W = t.shape
    return t.reshape(B, d, L, W).transpose(0, 2, 1, 3).reshape(B * L * d, W)


def _layer(x, mem, cc, s1, s2, l, w_in, w_out, lq1, lk1, lq2, lk2, subln_g, ln1_g, ln1_b,
           wq_mem, wkv_mem, wo_mem, ln2_g, ln2_b, w_router, e_bias, w_gate, w_up, w_down,
           ws_gate, ws_up, ws_down, ln3_g, ln3_b):
    B, S, D = x.shape
    T = B * S
    lambda_init = 0.8 - 0.6 * math.exp(-0.3 * l)
    x2 = x.reshape(T, D)
    tm = min(512, S)

    w_bf = w_in.astype(BF16)
    o_va, o_qb, o_vb = 2 * A_WIDTH, 3 * A_WIDTH, 3 * A_WIDTH + 2 * B_QK_WIDTH
    w_qk = jnp.concatenate([w_bf[:, :o_va], w_bf[:, o_qb:o_vb]], axis=1)
    w_vt = jnp.concatenate([w_bf[:, o_va:o_qb], w_bf[:, o_vb:]], axis=1).T
    qa, ka, qb, kb, vat, vbt = _in_proj(x2, w_qk, w_vt, cc, s1, s2, tm, S)

    outs, lses = [], []
    for window, d in A_PAIRS:
        assert window // d == BAND
        L = S // d
        q3, k3 = (_to_sub(t.reshape(B, S, A_WIDTH), d) for t in (qa, ka))
        vt3 = vat.reshape(B, A_WIDTH, L, d).transpose(0, 3, 1, 2).reshape(B * d, A_WIDTH, L)
        o, lse = _dilated(q3, k3, vt3, min(2 * BAND, L))
        outs.append(_from_sub(o, d, B))
        lse = lse.reshape(B, d, A_HEADS, L).transpose(0, 3, 1, 2).reshape(T, A_HEADS)
        lses.append(jnp.pad(lse, ((0, 0), (0, LANES - A_HEADS))))

    row = lambda v: v.reshape(1, -1).astype(F32)
    ob = _diff_attention(qb.reshape(B, S, -1), kb.reshape(B, S, -1), vbt, row(lq1), row(lk1), row(lq2), row(lk2),
                         subln_g.reshape(-1, 1).astype(F32), lambda_init, min(1024, S), min(512, S))

    expand = (jnp.arange(LANES)[:, None] == (jnp.arange(A_WIDTH)[None, :] // A_HEAD_DIM)).astype(BF16)
    h1 = _mix_out(outs, lses, ob.reshape(T, -1), x2, w_out.astype(BF16), expand, row(ln1_g), row(ln1_b), tm)

    M = mem.shape[1]
    kv = _mem_kv(mem.reshape(B * M, D), wkv_mem.astype(BF16), min(512, B * M))
    h2 = _mem_attn(h1, kv, wq_mem.astype(BF16), wo_mem.astype(BF16), row(ln2_g), row(ln2_b), tm, S // tm)

    wr_t = w_router.T
    wr_hi = wr_t.astype(BF16)
    wr_lo = (wr_t - wr_hi.astype(F32)).astype(BF16)
    idx, w_route, rank, counts = _router(h2, wr_hi, wr_lo, e_bias.reshape(-1, 1).astype(F32), tm)

    tile_rows = 512
    counts = counts.reshape(-1)
    padded = (counts + tile_rows - 1) // tile_rows * tile_rows
    pend = jnp.cumsum(padded)
    pstart = pend - padded
    n_tiles = (T * TOP_K) // tile_rows + N_EXPERTS
    n_rows = n_tiles * tile_rows
    tile_lo = jnp.arange(n_tiles, dtype=jnp.int32) * tile_rows
    tile_e = jnp.minimum(jnp.sum(pend[None, :] <= tile_lo[:, None], axis=1), N_EXPERTS - 1).astype(jnp.int32)
    tile_valid = jnp.clip(counts[tile_e] - (tile_lo - pstart[tile_e]), 0, tile_rows).astype(jnp.int32)
    n_used = (pend[-1] // tile_rows).astype(jnp.int32).reshape(1)

    dest = _dest_rows(idx, rank, pstart.reshape(-1, 1).astype(jnp.int32), min(2048, T))
    xs = _dispatch(dest, h2, n_rows, min(256, S))
    ys = _experts(tile_e, tile_valid, n_used, xs, w_gate, w_up, w_down, tile_rows)
    out = _combine(dest, w_route.T, h2, ys, ws_gate.astype(BF16), ws_up.astype(BF16), ws_down.astype(BF16),
                   row(ln3_g), row(ln3_b), min(256, S))
    return out.reshape(B, S, D)


def kernel(x, mem, positions, w_in, w_out, lambda_q1, lambda_k1, lambda_q2, lambda_k2, subln_g, ln1_g, ln1_b,
           wq_mem, wkv_mem, wo_mem, ln2_g, ln2_b, w_router, e_bias, w_gate, w_up, w_down, ws_gate, ws_up, ws_down,
           ln3_g, ln3_b):
    cc, s1, s2 = _rope_lane_tables(positions)
    h = x
    for l in range(w_in.shape[0]):
        h = _layer(h, mem, cc, s1, s2, l, w_in[l], w_out[l], lambda_q1[l], lambda_k1[l], lambda_q2[l],
                   lambda_k2[l], subln_g[l], ln1_g[l], ln1_b[l], wq_mem[l], wkv_mem[l], wo_mem[l],
                   ln2_g[l], ln2_b[l], w_router[l], e_bias[l], w_gate[l], w_up[l], w_down[l],
                   ws_gate[l], ws_up[l], ws_down[l], ln3_g[l], ln3_b[l])
    return h
```

```python
import functools
import math

import jax
import jax.numpy as jnp
from jax import lax
from jax.experimental import pallas as pl
from jax.experimental.pallas import tpu as pltpu

A_HEADS = 8
A_HEAD_DIM = 64
A_PAIRS = ((128, 1), (512, 4), (2048, 16))
B_HEADS = 4
B_QK_DIM = 64
B_V_DIM = 2 * B_QK_DIM
A_WIDTH = A_HEADS * A_HEAD_DIM
B_QK_WIDTH = B_HEADS * 2 * B_QK_DIM
B_V_WIDTH = B_HEADS * B_V_DIM
ROPE_THETA = 500000.0
ROT_DIM = A_HEAD_DIM // 4
M_HEADS = 4
N_EXPERTS = 256
TOP_K = 8
N_GROUPS = 8
TOPK_GROUPS = 4
ROUTED_SCALE = 2.5
LN_EPS = 1e-5
SUBLN_EPS = 1e-5
DEPTH = 1
DN_ALPHA = (2 * DEPTH) ** 0.25

LANES = 128
BAND = 128
VMEM_LIMIT = 56 * 1024 * 1024

BF16 = jnp.bfloat16
F32 = jnp.float32
NEG_INF = float("-inf")


def _cparams(sem):
    return pltpu.CompilerParams(dimension_semantics=sem, vmem_limit_bytes=VMEM_LIMIT)


def _dot(a, b):
    return jnp.dot(a, b, preferred_element_type=F32)


def _dot_nt(a, b):
    return lax.dot_general(a, b, (((1,), (1,)), ((), ())), preferred_element_type=F32)


def _layer_norm(x, g, b):
    mu = jnp.mean(x, axis=-1, keepdims=True)
    xc = x - mu
    var = jnp.mean(xc * xc, axis=-1, keepdims=True)
    return xc * lax.rsqrt(var + LN_EPS) * g + b


def _slab_rows(d):
    return d // LANES


def _store_slabs(ref, x):
    n, d = x.shape
    s = _slab_rows(d)
    for c in range(s):
        ref[pl.ds(c, n, stride=s), :] = x[:, c * LANES:(c + 1) * LANES]


def _load_slabs(ref, n, d):
    s = _slab_rows(d)
    return jnp.concatenate([ref[pl.ds(c, n, stride=s), :] for c in range(s)], axis=1)


def _in_proj_kernel(x_ref, w_ref, wvt_ref, c_ref, s1_ref, s2_ref, qa_ref, ka_ref, qb_ref, kb_ref, vat_ref, vbt_ref):
    x = x_ref[...].astype(BF16)
    cc, s1, s2 = c_ref[...], s1_ref[...], s2_ref[...]

    def rope(t):
        outs = []
        for c in range(t.shape[1] // LANES):
            tc = t[:, c * LANES:(c + 1) * LANES]
            nxt = pltpu.roll(tc, LANES - ROT_DIM // 2, 1)
            prv = pltpu.roll(tc, ROT_DIM // 2, 1)
            outs.append(tc * cc + nxt * s1 + prv * s2)
        return jnp.concatenate(outs, axis=1)

    def proj(lo, width):
        return _dot(x, w_ref[:, lo:lo + width])

    o = 0
    qa_ref[...] = (rope(proj(o, A_WIDTH)) * (A_HEAD_DIM ** -0.5)).astype(BF16); o += A_WIDTH
    ka_ref[...] = rope(proj(o, A_WIDTH)).astype(BF16); o += A_WIDTH
    qb_ref[...] = (rope(proj(o, B_QK_WIDTH)) * (B_QK_DIM ** -0.5)).astype(BF16); o += B_QK_WIDTH
    kb_ref[...] = rope(proj(o, B_QK_WIDTH)).astype(BF16)
    vat_ref[0] = _dot_nt(wvt_ref[:A_WIDTH, :], x).astype(BF16)
    vbt_ref[0] = _dot_nt(wvt_ref[A_WIDTH:, :], x).astype(BF16)


def _in_proj(x2, w_qk, w_vt, cc, s1, s2, tm, seq):
    T, D = x2.shape
    nt = seq // tm
    row = lambda i: (i, 0)
    const = lambda i: (0, 0)
    tcol = lambda i: (i // nt, 0, i % nt)
    outs = [jax.ShapeDtypeStruct((T, A_WIDTH), BF16)] * 2 + [jax.ShapeDtypeStruct((T, B_QK_WIDTH), BF16)] * 2
    outs_t = [jax.ShapeDtypeStruct((T // seq, A_WIDTH, seq), BF16), jax.ShapeDtypeStruct((T // seq, B_V_WIDTH, seq), BF16)]
    return pl.pallas_call(
        _in_proj_kernel,
        grid=(T // tm,),
        in_specs=[pl.BlockSpec((tm, D), row), pl.BlockSpec(w_qk.shape, const), pl.BlockSpec(w_vt.shape, const),
                  pl.BlockSpec((tm, LANES), row), pl.BlockSpec((tm, LANES), row), pl.BlockSpec((tm, LANES), row)],
        out_specs=[pl.BlockSpec((tm, o.shape[1]), row) for o in outs] + [
            pl.BlockSpec((1, o.shape[1], tm), tcol) for o in outs_t],
        out_shape=outs + outs_t,
        compiler_params=_cparams(("parallel",)),
        name="in_proj",
    )(x2, w_qk, w_vt, cc, s1, s2)


def _dilated_kernel(q_ref, kp_ref, kc_ref, vtp_ref, vtc_ref, o_ref, lse_ref):
    n = pl.program_id(1)
    tq = q_ref.shape[1]
    nk = BAND + tq
    two = LANES // A_HEAD_DIM
    kr = lax.broadcasted_iota(jnp.int32, (nk, two * tq), 0)
    qi = lax.broadcasted_iota(jnp.int32, (nk, two * tq), 1) % tq
    ok = (kr >= jnp.where(n > 0, qi, jnp.maximum(qi, BAND))) & (kr <= qi + BAND)
    lane = lax.broadcasted_iota(jnp.int32, (tq, LANES), 1)
    row = lax.broadcasted_iota(jnp.int32, (LANES, tq), 0)
    chunks = [slice(c * LANES, (c + 1) * LANES) for c in range(A_WIDTH // LANES)]
    scores = []
    for sl in chunks:
        q = q_ref[0, :, sl]
        zero = jnp.zeros_like(q)
        q2 = jnp.concatenate([jnp.where((lane >= h * A_HEAD_DIM) & (lane < (h + 1) * A_HEAD_DIM), q, zero)
                              for h in range(two)], axis=0)
        kband = jnp.concatenate([kp_ref[0, :, sl], kc_ref[0, :, sl]], axis=0)
        scores.append(_dot_nt(kband, q2))
    probs, lses = [], []
    for st in scores:
        s = jnp.where(ok, st, NEG_INF)
        m = jnp.max(s, axis=0, keepdims=True)
        p = jnp.exp(s - m)
        den = jnp.sum(p, axis=0, keepdims=True)
        probs.append((p.astype(BF16), den))
        lse = m + jnp.log(den)
        lses += [lse[:, h * tq:(h + 1) * tq] for h in range(two)]
    for sl, (p, den) in zip(chunks, probs):
        vt = jnp.concatenate([vtp_ref[0, sl, :], vtc_ref[0, sl, :]], axis=1)
        ot = _dot(vt, p) / den
        out_t = ot[:, :tq]
        for h in range(1, two):
            out_t = jnp.where(row >= h * A_HEAD_DIM, ot[:, h * tq:(h + 1) * tq], out_t)
        o_ref[0, :, sl] = out_t.T.astype(BF16)
    lse_ref[0] = jnp.concatenate(lses, axis=0)


def _dilated(q, k, vt, tq):
    G, L, W = q.shape
    per = tq // BAND
    cur = lambda g, n: (g, n, 0)
    prev = lambda g, n: (g, jnp.maximum(n * per - 1, 0), 0)
    cur_t = lambda g, n: (g, 0, n)
    prev_t = lambda g, n: (g, 0, jnp.maximum(n * per - 1, 0))
    return pl.pallas_call(
        _dilated_kernel,
        grid=(G, L // tq),
        in_specs=[pl.BlockSpec((1, tq, W), cur), pl.BlockSpec((1, BAND, W), prev), pl.BlockSpec((1, tq, W), cur),
                  pl.BlockSpec((1, W, BAND), prev_t), pl.BlockSpec((1, W, tq), cur_t)],
        out_specs=[pl.BlockSpec((1, tq, W), cur), pl.BlockSpec((1, A_HEADS, tq), cur_t)],
        out_shape=[jax.ShapeDtypeStruct((G, L, W), BF16), jax.ShapeDtypeStruct((G, A_HEADS, L), F32)],
        compiler_params=_cparams(("parallel", "parallel")),
        name="dilated",
    )(q, k, k, vt, vt)


def _diff_kernel(lq1_ref, lk1_ref, lq2_ref, lk2_ref, g_ref, q_ref, k_ref, vt_ref, o_ref, acc_ref, *, tq, tk, lambda_init):
    iq = pl.program_id(2)
    q = q_ref[0]
    lane = lax.broadcasted_iota(jnp.int32, q.shape, 1)
    zero = jnp.zeros_like(q)
    q2 = jnp.concatenate([jnp.where(lane < B_QK_DIM, q, zero), jnp.where(lane >= B_QK_DIM, q, zero)], axis=0)
    acc_ref[...] = jnp.zeros_like(acc_ref)

    def block(j, carry, diag):
        start = pl.multiple_of(j * tk, tk)
        kb = k_ref[0, pl.ds(start, tk), :]
        vt = vt_ref[0, :, pl.ds(start, tk)]
        st = _dot_nt(kb, q2)
        if diag is not None:
            ki = lax.broadcasted_iota(jnp.int32, (tk, tq), 0) + diag
            qi = lax.broadcasted_iota(jnp.int32, (tk, tq), 1)
            ok = ki <= qi
        out, work = [], []
        for c in range(2):
            m, l = carry[c]
            s = st[:, c * tq:(c + 1) * tq]
            if diag is not None:
                s = jnp.where(ok, s, NEG_INF)
            m_new = jnp.maximum(m, jnp.max(s, axis=0, keepdims=True))
            a = jnp.exp(m - m_new)
            p = jnp.exp(s - m_new)
            out.append((m_new, a * l + jnp.sum(p, axis=0, keepdims=True)))
            work.append((a, p.astype(BF16)))
        for c, (a, p) in enumerate(work):
            acc_ref[c] = a * acc_ref[c] + _dot(vt, p)
        return tuple(out)

    init = tuple((jnp.full((1, tq), NEG_INF, F32), jnp.zeros((1, tq), F32)) for _ in range(2))
    per_tile = tq // tk
    carry = lax.fori_loop(0, iq * per_tile, lambda j, cr: block(j, cr, None), init)
    for r in range(per_tile):
        carry = block(iq * per_tile + r, carry, r * tk)
    (_, l0), (_, l1) = carry

    lam = (jnp.exp(jnp.sum(lq1_ref[...] * lk1_ref[...], axis=1, keepdims=True))
           - jnp.exp(jnp.sum(lq2_ref[...] * lk2_ref[...], axis=1, keepdims=True)) + lambda_init)
    o = acc_ref[0] / l0 - lam * (acc_ref[1] / l1)
    o = o * lax.rsqrt(jnp.mean(o * o, axis=0, keepdims=True) + SUBLN_EPS) * g_ref[...]
    o_ref[0] = (o * (1.0 - lambda_init)).T.astype(BF16)


def _diff_attention(qb, kb, vbt, lq1, lk1, lq2, lk2, subln_g, lambda_init, tq, tk):
    B, S, _ = qb.shape
    vec = lambda n: pl.BlockSpec((1, n), lambda b, h, i: (0, 0))
    return pl.pallas_call(
        functools.partial(_diff_kernel, tq=tq, tk=tk, lambda_init=lambda_init),
        grid=(B, B_HEADS, S // tq),
        in_specs=[vec(B_QK_DIM)] * 4 + [pl.BlockSpec((B_V_DIM, 1), lambda b, h, i: (0, 0)),
                  pl.BlockSpec((1, tq, B_V_DIM), lambda b, h, i: (b, i, h)),
                  pl.BlockSpec((1, S, B_V_DIM), lambda b, h, i: (b, 0, h)),
                  pl.BlockSpec((1, B_V_DIM, S), lambda b, h, i: (b, h, 0))],
        out_specs=pl.BlockSpec((1, tq, B_V_DIM), lambda b, h, i: (b, i, h)),
        out_shape=jax.ShapeDtypeStruct((B, S, B_V_WIDTH), BF16),
        scratch_shapes=[pltpu.VMEM((2, B_V_DIM, tq), F32)],
        compiler_params=_cparams(("parallel", "parallel", "arbitrary")),
        name="diff_attn",
    )(lq1, lk1, lq2, lk2, subln_g, qb, kb, vbt)


def _mix_out_kernel(o1_ref, o2_ref, o3_ref, l1_ref, l2_ref, l3_ref, ob_ref, x_ref, w_ref, e_ref, g_ref, b_ref, h_ref):
    ls = (l1_ref[...], l2_ref[...], l3_ref[...])
    mx = jnp.maximum(jnp.maximum(ls[0], ls[1]), ls[2])
    es = [jnp.exp(l - mx) for l in ls]
    den = es[0] + es[1] + es[2]
    e = e_ref[...]
    oa = None
    for ex, o_ref in zip(es, (o1_ref, o2_ref, o3_ref)):
        wt = ex / den
        hi = wt.astype(BF16)
        lo = (wt - hi.astype(F32)).astype(BF16)
        wexp = _dot(hi, e) + _dot(lo, e)
        term = wexp * o_ref[...].astype(F32)
        oa = term if oa is None else oa + term
    y = _dot(oa.astype(BF16), w_ref[:A_WIDTH, :]) + _dot(ob_ref[...], w_ref[A_WIDTH:, :])
    h_ref[...] = _layer_norm(DN_ALPHA * x_ref[...] + y, g_ref[...], b_ref[...])


def _mix_out(os_, ls_, ob, x2, w_out, expand, g, b, tm):
    T, D = x2.shape
    row = lambda i: (i, 0)
    const = lambda i: (0, 0)
    return pl.pallas_call(
        _mix_out_kernel,
        grid=(T // tm,),
        in_specs=[pl.BlockSpec((tm, A_WIDTH), row)] * 3 + [pl.BlockSpec((tm, LANES), row)] * 3 + [
            pl.BlockSpec((tm, B_V_WIDTH), row), pl.BlockSpec((tm, D), row),
            pl.BlockSpec(w_out.shape, const), pl.BlockSpec(expand.shape, const),
            pl.BlockSpec((1, D), const), pl.BlockSpec((1, D), const)],
        out_specs=pl.BlockSpec((tm, D), row),
        out_shape=jax.ShapeDtypeStruct((T, D), F32),
        compiler_params=_cparams(("parallel",)),
        name="mix_out",
    )(*os_, *ls_, ob, x2, w_out, expand, g, b)


def _matmul_kernel(a_ref, w_ref, o_ref):
    o_ref[...] = _dot(a_ref[...].astype(BF16), w_ref[...]).astype(o_ref.dtype)


def _mem_kv(mem2, wkv, tm):
    R, D = mem2.shape
    N = wkv.shape[1]
    return pl.pallas_call(
        _matmul_kernel,
        grid=(R // tm,),
        in_specs=[pl.BlockSpec((tm, D), lambda i: (i, 0)), pl.BlockSpec((D, N), lambda i: (0, 0))],
        out_specs=pl.BlockSpec((tm, N), lambda i: (i, 0)),
        out_shape=jax.ShapeDtypeStruct((R, N), BF16),
        compiler_params=_cparams(("parallel",)),
        name="mem_kv",
    )(mem2, wkv)


def _mem_attn_kernel(h_ref, kv_ref, wq_ref, wo_ref, g_ref, b_ref, out_ref, slab_ref):
    h = h_ref[...]
    D = h.shape[1]
    hd = D // M_HEADS
    q = (_dot(h.astype(BF16), wq_ref[...]) * (hd ** -0.5)).astype(BF16)
    heads = []
    for i in range(M_HEADS):
        k = kv_ref[:, i * hd:(i + 1) * hd]
        v = kv_ref[:, D + i * hd:D + (i + 1) * hd]
        s = _dot_nt(q[:, i * hd:(i + 1) * hd], k)
        p = jnp.exp(s - jnp.max(s, axis=1, keepdims=True))
        p = p / jnp.sum(p, axis=1, keepdims=True)
        heads.append(_dot(p.astype(BF16), v).astype(BF16))
    o = jnp.concatenate(heads, axis=1)
    h2 = _layer_norm(DN_ALPHA * h + _dot(o, wo_ref[...]), g_ref[...], b_ref[...])
    out_ref[...] = h2
    _store_slabs(slab_ref, h2)


def _mem_attn(h1, kv, wq, wo, g, b, tm, tiles_per_batch):
    T, D = h1.shape
    M = kv.shape[0] // (T // (tm * tiles_per_batch))
    row = lambda i: (i, 0)
    const = lambda i: (0, 0)
    return pl.pallas_call(
        _mem_attn_kernel,
        grid=(T // tm,),
        in_specs=[pl.BlockSpec((tm, D), row), pl.BlockSpec((M, 2 * D), lambda i: (i // tiles_per_batch, 0)),
                  pl.BlockSpec((D, D), const), pl.BlockSpec((D, D), const),
                  pl.BlockSpec((1, D), const), pl.BlockSpec((1, D), const)],
        out_specs=[pl.BlockSpec((tm, D), row), pl.BlockSpec((tm * _slab_rows(D), LANES), row)],
        out_shape=[jax.ShapeDtypeStruct((T, D), F32), jax.ShapeDtypeStruct((T * _slab_rows(D), LANES), F32)],
        compiler_params=_cparams(("parallel",)),
        name="mem_attn",
    )(h1, kv, wq, wo, g, b)


def _argmax_rows(v, row_iota, n_rows):
    m = jnp.max(v, axis=0, keepdims=True)
    idx = jnp.min(jnp.where(v == m, row_iota, n_rows), axis=0, keepdims=True)
    return m, idx


def _router_kernel(h_ref, whi_ref, wlo_ref, bias_ref, idx_ref, w_ref, rank_ref, cnt_ref, carry_ref):
    @pl.when(pl.program_id(0) == 0)
    def _():
        carry_ref[...] = jnp.zeros_like(carry_ref)

    h = h_ref[...]
    tt = h.shape[0]
    xhi = h.astype(BF16)
    xlo = (h - xhi.astype(F32)).astype(BF16)
    whi, wlo = whi_ref[...], wlo_ref[...]
    logits = _dot_nt(whi, xhi) + (_dot_nt(wlo, xhi) + _dot_nt(whi, xlo))
    scores = jax.nn.sigmoid(logits)
    biased = scores + bias_ref[...]

    gsz = N_EXPERTS // N_GROUPS
    giota = lax.broadcasted_iota(jnp.int32, (gsz, tt), 0)
    gscore = []
    for g in range(N_GROUPS):
        blk = biased[g * gsz:(g + 1) * gsz]
        m1, i1 = _argmax_rows(blk, giota, gsz)
        m2 = jnp.max(jnp.where(giota == i1, NEG_INF, blk), axis=0, keepdims=True)
        gscore.append(m1 + m2)
    gscore = jnp.concatenate(gscore, axis=0)
    riota = lax.broadcasted_iota(jnp.int32, (N_GROUPS, tt), 0)
    gsel = jnp.zeros((N_GROUPS, tt), F32)
    for _ in range(TOPK_GROUPS):
        _, gi = _argmax_rows(gscore, riota, N_GROUPS)
        hit = riota == gi
        gsel = jnp.where(hit, 1.0, gsel)
        gscore = jnp.where(hit, NEG_INF, gscore)
    masked = jnp.concatenate(
        [jnp.where(gsel[g:g + 1] > 0.5, biased[g * gsz:(g + 1) * gsz], NEG_INF) for g in range(N_GROUPS)], axis=0)

    eiota = lax.broadcasted_iota(jnp.int32, (N_EXPERTS, tt), 0)
    chosen = jnp.zeros((N_EXPERTS, tt), F32)
    idxs, ws = [], []
    for _ in range(TOP_K):
        _, ei = _argmax_rows(masked, eiota, N_EXPERTS)
        hit = eiota == ei
        idxs.append(ei)
        ws.append(jnp.sum(jnp.where(hit, scores, 0.0), axis=0, keepdims=True))
        chosen = jnp.where(hit, 1.0, chosen)
        masked = jnp.where(hit, NEG_INF, masked)
    idx = jnp.concatenate(idxs, axis=0)
    w = jnp.concatenate(ws, axis=0)
    w = w / jnp.sum(w, axis=0, keepdims=True) * ROUTED_SCALE

    before = (lax.broadcasted_iota(jnp.int32, (tt, tt), 0) < lax.broadcasted_iota(jnp.int32, (tt, tt), 1))
    within = _dot(chosen.astype(BF16), before.astype(BF16))
    erank = within + carry_ref[...]
    ranks = [jnp.sum(jnp.where(eiota == idxs[k], erank, 0.0), axis=0, keepdims=True) for k in range(TOP_K)]
    carry = carry_ref[...] + jnp.sum(chosen, axis=1, keepdims=True)
    carry_ref[...] = carry

    idx_ref[...] = idx
    w_ref[...] = w
    rank_ref[...] = jnp.concatenate(ranks, axis=0).astype(jnp.int32)
    cnt_ref[...] = carry.astype(jnp.int32)


def _router(h2, whi, wlo, bias, tt):
    T, D = h2.shape
    col = lambda i: (0, i)
    const = lambda i: (0, 0)
    return pl.pallas_call(
        _router_kernel,
        grid=(T // tt,),
        in_specs=[pl.BlockSpec((tt, D), lambda i: (i, 0)), pl.BlockSpec((N_EXPERTS, D), const),
                  pl.BlockSpec((N_EXPERTS, D), const), pl.BlockSpec((N_EXPERTS, 1), const)],
        out_specs=[pl.BlockSpec((TOP_K, tt), col), pl.BlockSpec((TOP_K, tt), col), pl.BlockSpec((TOP_K, tt), col),
                   pl.BlockSpec((N_EXPERTS, 1), const)],
        out_shape=[jax.ShapeDtypeStruct((TOP_K, T), jnp.int32), jax.ShapeDtypeStruct((TOP_K, T), F32),
                   jax.ShapeDtypeStruct((TOP_K, T), jnp.int32), jax.ShapeDtypeStruct((N_EXPERTS, 1), jnp.int32)],
        scratch_shapes=[pltpu.VMEM((N_EXPERTS, 1), F32)],
        compiler_params=_cparams(("arbitrary",)),
        name="router",
    )(h2, whi, wlo, bias)


def _dest_kernel(idx_ref, rank_ref, start_ref, dest_ref):
    idx = idx_ref[...]
    tt = idx.shape[1]
    eiota = lax.broadcasted_iota(jnp.int32, (N_EXPERTS, tt), 0)
    start = start_ref[...].astype(F32)
    rows = [jnp.sum(jnp.where(eiota == idx[k:k + 1], start, 0.0), axis=0, keepdims=True) for k in range(TOP_K)]
    dest_ref[...] = jnp.concatenate(rows, axis=0).astype(jnp.int32) + rank_ref[...]


def _dest_rows(idx, rank, start, tt):
    T = idx.shape[1]
    col = lambda i: (0, i)
    return pl.pallas_call(
        _dest_kernel,
        grid=(T // tt,),
        in_specs=[pl.BlockSpec((TOP_K, tt), col), pl.BlockSpec((TOP_K, tt), col),
                  pl.BlockSpec((N_EXPERTS, 1), lambda i: (0, 0))],
        out_specs=pl.BlockSpec((TOP_K, tt), col),
        out_shape=jax.ShapeDtypeStruct((TOP_K, T), jnp.int32),
        compiler_params=_cparams(("parallel",)),
        name="dest_rows",
    )(idx, rank, start)


def _dispatch_kernel(dest_ref, x_ref, xs_ref, sem, *, sr):
    tt = x_ref.shape[0] // sr

    def issue(t, c):
        src = x_ref.at[pl.ds(pl.multiple_of(t * sr, sr), sr)]
        for k in range(TOP_K):
            dst = xs_ref.at[pl.ds(pl.multiple_of(dest_ref[k, t] * sr, sr), sr)]
            pltpu.make_async_copy(src, dst, sem).start(priority=k % 2)
        return c

    lax.fori_loop(0, tt, issue, 0)
    n = TOP_K * tt * sr
    pltpu.make_async_copy(xs_ref.at[pl.ds(0, n)], xs_ref.at[pl.ds(0, n)], sem).wait()


def _dispatch(dest, x_slabs, n_rows, tt, sr):
    T = x_slabs.shape[0] // sr
    return pl.pallas_call(
        functools.partial(_dispatch_kernel, sr=sr),
        grid=(T // tt,),
        in_specs=[pl.BlockSpec((TOP_K, tt), lambda i: (0, i), memory_space=pltpu.SMEM),
                  pl.BlockSpec((tt * sr, LANES), lambda i: (i, 0))],
        out_specs=pl.BlockSpec(memory_space=pl.ANY),
        out_shape=jax.ShapeDtypeStruct((n_rows * sr, LANES), F32),
        scratch_shapes=[pltpu.SemaphoreType.DMA],
        compiler_params=_cparams(("arbitrary",)),
        name="dispatch",
    )(dest, x_slabs)


def _experts_kernel(te_ref, tv_ref, nu_ref, x_ref, wg_ref, wu_ref, wd_ref, y_ref, wg_bf, wu_bf, wd_bf):
    i = pl.program_id(0)

    @pl.when(i < nu_ref[0])
    def _():
        @pl.when((i == 0) | (te_ref[i] != te_ref[jnp.maximum(i - 1, 0)]))
        def _():
            wg_bf[...] = wg_ref[0].astype(BF16)
            wu_bf[...] = wu_ref[0].astype(BF16)
            wd_bf[...] = wd_ref[0].astype(BF16)

        d = wg_bf.shape[0]
        tm = x_ref.shape[0] // _slab_rows(d)
        rows = lax.broadcasted_iota(jnp.int32, (tm, 1), 0)
        x = jnp.where(rows < tv_ref[i], _load_slabs(x_ref, tm, d), 0.0).astype(BF16)
        g = _dot(x, wg_bf[...])
        act = (g * jax.nn.sigmoid(g) * _dot(x, wu_bf[...])).astype(BF16)
        _store_slabs(y_ref, _dot(act, wd_bf[...]))


def _experts(tile_e, tile_valid, n_used, xs, w_gate, w_up, w_down, tm):
    E, D, F = w_gate.shape
    sr = _slab_rows(D)
    n_tiles = xs.shape[0] // (tm * sr)

    def tile(i, te, tv, nu):
        return (jnp.minimum(i, nu[0] - 1), 0)

    def expert(i, te, tv, nu):
        return (te[jnp.minimum(i, nu[0] - 1)], 0, 0)

    return pl.pallas_call(
        _experts_kernel,
        grid_spec=pltpu.PrefetchScalarGridSpec(
            num_scalar_prefetch=3,
            grid=(n_tiles,),
            in_specs=[pl.BlockSpec((tm * sr, LANES), tile), pl.BlockSpec((1, D, F), expert),
                      pl.BlockSpec((1, D, F), expert), pl.BlockSpec((1, F, D), expert)],
            out_specs=pl.BlockSpec((tm * sr, LANES), tile),
            scratch_shapes=[pltpu.VMEM((D, F), BF16), pltpu.VMEM((D, F), BF16), pltpu.VMEM((F, D), BF16)],
        ),
        out_shape=jax.ShapeDtypeStruct(xs.shape, F32),
        compiler_params=_cparams(("arbitrary",)),
        name="experts",
    )(tile_e, tile_valid, n_used, xs, w_gate, w_up, w_down)


def _combine_kernel(dest_ref, nxt_ref, w_ref, h_ref, ys_ref, sg_ref, su_ref, sd_ref, g_ref, b_ref, out_ref, buf, sem):
    i = pl.program_id(0)
    last = pl.num_programs(0) - 1
    tt, d = h_ref.shape
    sr = _slab_rows(d)
    slot = i % 2

    def gather(idx_ref, to):
        def issue(t, c):
            for k in range(TOP_K):
                src = ys_ref.at[pl.ds(pl.multiple_of(idx_ref[k, t] * sr, sr), sr)]
                dst = buf.at[to, k, pl.ds(pl.multiple_of(t * sr, sr), sr)]
                pltpu.make_async_copy(src, dst, sem.at[to]).start(priority=k % 2)
            return c

        lax.fori_loop(0, tt, issue, 0)

    @pl.when(i == 0)
    def _():
        gather(dest_ref, 0)

    @pl.when(i < last)
    def _():
        gather(nxt_ref, 1 - slot)

    h = h_ref[...]
    hb = h.astype(BF16)
    gate = _dot(hb, sg_ref[...])
    act = (gate * jax.nn.sigmoid(gate) * _dot(hb, su_ref[...])).astype(BF16)
    shared = _dot(act, sd_ref[...])

    for k in range(TOP_K):
        pltpu.make_async_copy(ys_ref.at[pl.ds(0, tt * sr)], buf.at[slot, k], sem.at[slot]).wait()

    w = w_ref[...]
    wk = [jnp.broadcast_to(w[:, k:k + 1], (tt, LANES)) for k in range(TOP_K)]
    chunks = []
    for c in range(sr):
        acc = buf[slot, 0, pl.ds(c, tt, stride=sr), :] * wk[0]
        for k in range(1, TOP_K):
            acc = acc + buf[slot, k, pl.ds(c, tt, stride=sr), :] * wk[k]
        chunks.append(acc)
    routed = jnp.concatenate(chunks, axis=1)
    out_ref[...] = _layer_norm(DN_ALPHA * h + (routed + shared), g_ref[...], b_ref[...])


def _combine(dest, w_tok, h2, ys, sg, su, sd, g, b, tt):
    T, D = h2.shape
    sr = _slab_rows(D)
    row = lambda i: (i, 0)
    const = lambda i: (0, 0)
    n = T // tt
    return pl.pallas_call(
        _combine_kernel,
        grid=(n,),
        in_specs=[pl.BlockSpec((TOP_K, tt), lambda i: (0, i), memory_space=pltpu.SMEM),
                  pl.BlockSpec((TOP_K, tt), lambda i: (0, jnp.minimum(i + 1, n - 1)), memory_space=pltpu.SMEM),
                  pl.BlockSpec((tt, TOP_K), row), pl.BlockSpec((tt, D), row),
                  pl.BlockSpec(memory_space=pl.ANY),
                  pl.BlockSpec(sg.shape, const), pl.BlockSpec(su.shape, const), pl.BlockSpec(sd.shape, const),
                  pl.BlockSpec((1, D), const), pl.BlockSpec((1, D), const)],
        out_specs=pl.BlockSpec((tt, D), row),
        out_shape=jax.ShapeDtypeStruct((T, D), F32),
        scratch_shapes=[pltpu.VMEM((2, TOP_K, tt * sr, LANES), F32), pltpu.SemaphoreType.DMA((2,))],
        compiler_params=_cparams(("arbitrary",)),
        name="combine",
    )(dest, dest, w_tok, h2, ys, sg, su, sd, g, b)


def _rope_lane_tables(positions):
    half = ROT_DIM // 2
    inv = ROPE_THETA ** (-jnp.arange(0, ROT_DIM, 2, dtype=F32) / ROT_DIM)
    ang = positions.astype(F32).reshape(-1, 1) * inv
    cos, sin = jnp.cos(ang), jnp.sin(ang)
    T = cos.shape[0]
    rest = A_HEAD_DIM - ROT_DIM
    one_head = lambda first, second, fill: jnp.concatenate(
        [first, second, jnp.full((T, rest), fill, F32)], axis=1)
    zeros = jnp.zeros_like(sin)
    cc = one_head(cos, cos, 1.0)
    s1 = one_head(-sin, zeros, 0.0)
    s2 = one_head(zeros, sin, 0.0)
    rep = LANES // A_HEAD_DIM
    return jnp.tile(cc, (1, rep)), jnp.tile(s1, (1, rep)), jnp.tile(s2, (1, rep))


def _to_sub(t, d):
    B, S, W = t.shape
    return t.reshape(B, S // d, d, W).transpose(0, 2, 1, 3).reshape(B * d, S // d, W)


def _from_sub(t, d, B):
    G, L, W = t.shape
    return t.reshape(B, d, L, W).transpose(0, 2, 1, 3).reshape(B * L * d, W)


def _layer(x, mem, cc, s1, s2, l, w_in, w_out, lq1, lk1, lq2, lk2, subln_g, ln1_g, ln1_b,
           wq_mem, wkv_mem, wo_mem, ln2_g, ln2_b, w_router, e_bias, w_gate, w_up, w_down,
           ws_gate, ws_up, ws_down, ln3_g, ln3_b):
    B, S, D = x.shape
    T = B * S
    lambda_init = 0.8 - 0.6 * math.exp(-0.3 * l)
    x2 = x.reshape(T, D)
    tm = min(512, S)

    w_bf = w_in.astype(BF16)
    o_va, o_qb, o_vb = 2 * A_WIDTH, 3 * A_WIDTH, 3 * A_WIDTH + 2 * B_QK_WIDTH
    w_qk = jnp.concatenate([w_bf[:, :o_va], w_bf[:, o_qb:o_vb]], axis=1)
    w_vt = jnp.concatenate([w_bf[:, o_va:o_qb], w_bf[:, o_vb:]], axis=1).T
    qa, ka, qb, kb, vat, vbt = _in_proj(x2, w_qk, w_vt, cc, s1, s2, tm, S)

    outs, lses = [], []
    for window, d in A_PAIRS:
        assert window // d == BAND
        L = S // d
        q3, k3 = (_to_sub(t.reshape(B, S, A_WIDTH), d) for t in (qa, ka))
        vt3 = vat.reshape(B, A_WIDTH, L, d).transpose(0, 3, 1, 2).reshape(B * d, A_WIDTH, L)
        o, lse = _dilated(q3, k3, vt3, min(2 * BAND, L))
        outs.append(_from_sub(o, d, B))
        lse = lse.reshape(B, d, A_HEADS, L).transpose(0, 3, 1, 2).reshape(T, A_HEADS)
        lses.append(jnp.pad(lse, ((0, 0), (0, LANES - A_HEADS))))

    row = lambda v: v.reshape(1, -1).astype(F32)
    ob = _diff_attention(qb.reshape(B, S, -1), kb.reshape(B, S, -1), vbt, row(lq1), row(lk1), row(lq2), row(lk2),
                         subln_g.reshape(-1, 1).astype(F32), lambda_init, min(1024, S), min(512, S))

    expand = (jnp.arange(LANES)[:, None] == (jnp.arange(A_WIDTH)[None, :] // A_HEAD_DIM)).astype(BF16)
    h1 = _mix_out(outs, lses, ob.reshape(T, -1), x2, w_out.astype(BF16), expand, row(ln1_g), row(ln1_b), tm)

    M = mem.shape[1]
    kv = _mem_kv(mem.reshape(B * M, D), wkv_mem.astype(BF16), min(512, B * M))
    h2, h2_slabs = _mem_attn(h1, kv, wq_mem.astype(BF16), wo_mem.astype(BF16), row(ln2_g), row(ln2_b), tm, S // tm)

    wr_t = w_router.T
    wr_hi = wr_t.astype(BF16)
    wr_lo = (wr_t - wr_hi.astype(F32)).astype(BF16)
    idx, w_route, rank, counts = _router(h2, wr_hi, wr_lo, e_bias.reshape(-1, 1).astype(F32), tm)

    tile_rows = 512
    counts = counts.reshape(-1)
    padded = (counts + tile_rows - 1) // tile_rows * tile_rows
    pend = jnp.cumsum(padded)
    pstart = pend - padded
    n_tiles = (T * TOP_K) // tile_rows + N_EXPERTS
    n_rows = n_tiles * tile_rows
    tile_lo = jnp.arange(n_tiles, dtype=jnp.int32) * tile_rows
    tile_e = jnp.minimum(jnp.sum(pend[None, :] <= tile_lo[:, None], axis=1), N_EXPERTS - 1).astype(jnp.int32)
    tile_valid = jnp.clip(counts[tile_e] - (tile_lo - pstart[tile_e]), 0, tile_rows).astype(jnp.int32)
    n_used = (pend[-1] // tile_rows).astype(jnp.int32).reshape(1)

    dest = _dest_rows(idx, rank, pstart.reshape(-1, 1).astype(jnp.int32), min(2048, T))
    xs = _dispatch(dest, h2_slabs, n_rows, min(256, S), _slab_rows(D))
    ys = _experts(tile_e, tile_valid, n_used, xs, w_gate, w_up, w_down, tile_rows)
    out = _combine(dest, w_route.T, h2, ys, ws_gate.astype(BF16), ws_up.astype(BF16), ws_down.astype(BF16),
                   row(ln3_g), row(ln3_b), min(256, S))
    return out.reshape(B, S, D)


def kernel(x, mem, positions, w_in, w_out, lambda_q1, lambda_k1, lambda_q2, lambda_k2, subln_g, ln1_g, ln1_b,
           wq_mem, wkv_mem, wo_mem, ln2_g, ln2_b, w_router, e_bias, w_gate, w_up, w_down, ws_gate, ws_up, ws_down,
           ln3_g, ln3_b):
    cc, s1, s2 = _rope_lane_tables(positions)
    h = x
    for l in range(w_in.shape[0]):
        h = _layer(h, mem, cc, s1, s2, l, w_in[l], w_out[l], lambda_q1[l], lambda_k1[l], lambda_q2[l],
                   lambda_k2[l], subln_g[l], ln1_g[l], ln1_b[l], wq_mem[l], wkv_mem[l], wo_mem[l],
                   ln2_g[l], ln2_b[l], w_router[l], e_bias[l], w_gate[l], w_up[l], w_down[l],
                   ws_gate[l], ws_up[l], ws_down[l], ln3_g[l], ln3_b[l])
    return h
```

```python
import functools
import math

import jax
import jax.numpy as jnp
from jax import lax
from jax.experimental import pallas as pl
from jax.experimental.pallas import tpu as pltpu

A_HEADS = 8
A_HEAD_DIM = 64
A_PAIRS = ((128, 1), (512, 4), (2048, 16))
B_HEADS = 4
B_QK_DIM = 64
B_V_DIM = 2 * B_QK_DIM
A_WIDTH = A_HEADS * A_HEAD_DIM
B_QK_WIDTH = B_HEADS * 2 * B_QK_DIM
B_V_WIDTH = B_HEADS * B_V_DIM
ROPE_THETA = 500000.0
ROT_DIM = A_HEAD_DIM // 4
M_HEADS = 4
N_EXPERTS = 256
TOP_K = 8
N_GROUPS = 8
TOPK_GROUPS = 4
ROUTED_SCALE = 2.5
LN_EPS = 1e-5
SUBLN_EPS = 1e-5
DEPTH = 1
DN_ALPHA = (2 * DEPTH) ** 0.25

LANES = 128
BAND = 128
ONES_ROWS = 16
VMEM_LIMIT = 56 * 1024 * 1024

BF16 = jnp.bfloat16
F32 = jnp.float32
NEG_INF = float("-inf")


def _cparams(sem):
    return pltpu.CompilerParams(dimension_semantics=sem, vmem_limit_bytes=VMEM_LIMIT)


def _dot(a, b):
    return jnp.dot(a, b, preferred_element_type=F32)


def _dot_nt(a, b):
    return lax.dot_general(a, b, (((1,), (1,)), ((), ())), preferred_element_type=F32)


def _layer_norm(x, g, b):
    mu = jnp.mean(x, axis=-1, keepdims=True)
    xc = x - mu
    var = jnp.mean(xc * xc, axis=-1, keepdims=True)
    return xc * lax.rsqrt(var + LN_EPS) * g + b


def _slab_rows(d):
    return d // LANES


def _store_slabs(ref, x):
    n, d = x.shape
    s = _slab_rows(d)
    for c in range(s):
        ref[pl.ds(c, n, stride=s), :] = x[:, c * LANES:(c + 1) * LANES]


def _load_slabs(ref, n, d):
    s = _slab_rows(d)
    return jnp.concatenate([ref[pl.ds(c, n, stride=s), :] for c in range(s)], axis=1)


def _in_proj_kernel(x_ref, w_ref, wvt_ref, c_ref, s1_ref, s2_ref, qa_ref, ka_ref, qb_ref, kb_ref, vat_ref, vbt_ref):
    x = x_ref[...].astype(BF16)
    cc, s1, s2 = c_ref[...], s1_ref[...], s2_ref[...]

    def rope(t):
        outs = []
        for c in range(t.shape[1] // LANES):
            tc = t[:, c * LANES:(c + 1) * LANES]
            nxt = pltpu.roll(tc, LANES - ROT_DIM // 2, 1)
            prv = pltpu.roll(tc, ROT_DIM // 2, 1)
            outs.append(tc * cc + nxt * s1 + prv * s2)
        return jnp.concatenate(outs, axis=1)

    def proj(lo, width):
        return _dot(x, w_ref[:, lo:lo + width])

    o = 0
    qa_ref[...] = (rope(proj(o, A_WIDTH)) * (A_HEAD_DIM ** -0.5)).astype(BF16); o += A_WIDTH
    ka_ref[...] = rope(proj(o, A_WIDTH)).astype(BF16); o += A_WIDTH
    qb_ref[...] = (rope(proj(o, B_QK_WIDTH)) * (B_QK_DIM ** -0.5)).astype(BF16); o += B_QK_WIDTH
    kb_ref[...] = rope(proj(o, B_QK_WIDTH)).astype(BF16)
    vat_ref[0] = _dot_nt(wvt_ref[:A_WIDTH, :], x).astype(BF16)
    vbt_ref[0] = _dot_nt(wvt_ref[A_WIDTH:, :], x).astype(BF16)


def _in_proj(x2, w_qk, w_vt, cc, s1, s2, tm, seq):
    T, D = x2.shape
    nt = seq // tm
    row = lambda i: (i, 0)
    const = lambda i: (0, 0)
    tcol = lambda i: (i // nt, 0, i % nt)
    outs = [jax.ShapeDtypeStruct((T, A_WIDTH), BF16)] * 2 + [jax.ShapeDtypeStruct((T, B_QK_WIDTH), BF16)] * 2
    outs_t = [jax.ShapeDtypeStruct((T // seq, A_WIDTH, seq), BF16), jax.ShapeDtypeStruct((T // seq, B_V_WIDTH, seq), BF16)]
    return pl.pallas_call(
        _in_proj_kernel,
        grid=(T // tm,),
        in_specs=[pl.BlockSpec((tm, D), row), pl.BlockSpec(w_qk.shape, const), pl.BlockSpec(w_vt.shape, const),
                  pl.BlockSpec((tm, LANES), row), pl.BlockSpec((tm, LANES), row), pl.BlockSpec((tm, LANES), row)],
        out_specs=[pl.BlockSpec((tm, o.shape[1]), row) for o in outs] + [
            pl.BlockSpec((1, o.shape[1], tm), tcol) for o in outs_t],
        out_shape=outs + outs_t,
        compiler_params=_cparams(("parallel",)),
        name="in_proj",
    )(x2, w_qk, w_vt, cc, s1, s2)


def _dilated_kernel(q_ref, kp_ref, kc_ref, vtp_ref, vtc_ref, o_ref, lse_ref):
    n = pl.program_id(1)
    tq = q_ref.shape[1]
    nk = BAND + tq
    two = LANES // A_HEAD_DIM
    kr = lax.broadcasted_iota(jnp.int32, (nk, two * tq), 0)
    qi = lax.broadcasted_iota(jnp.int32, (nk, two * tq), 1) % tq
    ok = (kr >= jnp.where(n > 0, qi, jnp.maximum(qi, BAND))) & (kr <= qi + BAND)
    lane = lax.broadcasted_iota(jnp.int32, (tq, LANES), 1)
    row = lax.broadcasted_iota(jnp.int32, (LANES, tq), 0)
    chunks = [slice(c * LANES, (c + 1) * LANES) for c in range(A_WIDTH // LANES)]
    scores = []
    for sl in chunks:
        q = q_ref[0, :, sl]
        zero = jnp.zeros_like(q)
        q2 = jnp.concatenate([jnp.where((lane >= h * A_HEAD_DIM) & (lane < (h + 1) * A_HEAD_DIM), q, zero)
                              for h in range(two)], axis=0)
        kband = jnp.concatenate([kp_ref[0, :, sl], kc_ref[0, :, sl]], axis=0)
        scores.append(_dot_nt(kband, q2))
    probs, lses = [], []
    for st in scores:
        s = jnp.where(ok, st, NEG_INF)
        m = jnp.max(s, axis=0, keepdims=True)
        p = jnp.exp(s - m)
        den = jnp.sum(p, axis=0, keepdims=True)
        probs.append((p.astype(BF16), den))
        lse = m + jnp.log(den)
        lses += [lse[:, h * tq:(h + 1) * tq] for h in range(two)]
    for sl, (p, den) in zip(chunks, probs):
        vt = jnp.concatenate([vtp_ref[0, sl, :], vtc_ref[0, sl, :]], axis=1)
        ot = _dot(vt, p) / den
        out_t = ot[:, :tq]
        for h in range(1, two):
            out_t = jnp.where(row >= h * A_HEAD_DIM, ot[:, h * tq:(h + 1) * tq], out_t)
        o_ref[0, :, sl] = out_t.T.astype(BF16)
    lse_ref[0] = jnp.concatenate(lses, axis=0)


def _dilated(q, k, vt, tq):
    G, L, W = q.shape
    per = tq // BAND
    cur = lambda g, n: (g, n, 0)
    prev = lambda g, n: (g, jnp.maximum(n * per - 1, 0), 0)
    cur_t = lambda g, n: (g, 0, n)
    prev_t = lambda g, n: (g, 0, jnp.maximum(n * per - 1, 0))
    return pl.pallas_call(
        _dilated_kernel,
        grid=(G, L // tq),
        in_specs=[pl.BlockSpec((1, tq, W), cur), pl.BlockSpec((1, BAND, W), prev), pl.BlockSpec((1, tq, W), cur),
                  pl.BlockSpec((1, W, BAND), prev_t), pl.BlockSpec((1, W, tq), cur_t)],
        out_specs=[pl.BlockSpec((1, tq, W), cur), pl.BlockSpec((1, A_HEADS, tq), cur_t)],
        out_shape=[jax.ShapeDtypeStruct((G, L, W), BF16), jax.ShapeDtypeStruct((G, A_HEADS, L), F32)],
        compiler_params=_cparams(("parallel", "parallel")),
        name="dilated",
    )(q, k, k, vt, vt)


def _diff_kernel(lq1_ref, lk1_ref, lq2_ref, lk2_ref, g_ref, q_ref, k_ref, vt_ref, o_ref, acc_ref, *, tq, tk, lambda_init):
    iq = pl.program_id(2)
    q = q_ref[0]
    lane = lax.broadcasted_iota(jnp.int32, q.shape, 1)
    zero = jnp.zeros_like(q)
    q2 = jnp.concatenate([jnp.where(lane < B_QK_DIM, q, zero), jnp.where(lane >= B_QK_DIM, q, zero)], axis=0)
    acc_ref[...] = jnp.zeros_like(acc_ref)

    def block(j, carry, diag):
        start = pl.multiple_of(j * tk, tk)
        kb = k_ref[0, pl.ds(start, tk), :]
        vt = jnp.concatenate([vt_ref[0, :, pl.ds(start, tk)], jnp.ones((ONES_ROWS, tk), BF16)], axis=0)
        st = _dot_nt(kb, q2)
        if diag is not None:
            ki = lax.broadcasted_iota(jnp.int32, (tk, tq), 0) + diag
            qi = lax.broadcasted_iota(jnp.int32, (tk, tq), 1)
            ok = ki <= qi
        out, work = [], []
        for c in range(2):
            m = carry[c]
            s = st[:, c * tq:(c + 1) * tq]
            if diag is not None:
                s = jnp.where(ok, s, NEG_INF)
            m_new = jnp.maximum(m, jnp.max(s, axis=0, keepdims=True))
            out.append(m_new)
            work.append((jnp.exp(m - m_new), jnp.exp(s - m_new).astype(BF16)))
        for c, (a, p) in enumerate(work):
            acc_ref[c] = a * acc_ref[c] + _dot(vt, p)
        return tuple(out)

    init = tuple(jnp.full((1, tq), NEG_INF, F32) for _ in range(2))
    per_tile = tq // tk
    carry = lax.fori_loop(0, iq * per_tile, lambda j, cr: block(j, cr, None), init)
    for r in range(per_tile):
        carry = block(iq * per_tile + r, carry, r * tk)

    lam = (jnp.exp(jnp.sum(lq1_ref[...] * lk1_ref[...], axis=1, keepdims=True))
           - jnp.exp(jnp.sum(lq2_ref[...] * lk2_ref[...], axis=1, keepdims=True)) + lambda_init)
    nv = B_V_DIM
    o = (acc_ref[0, :nv, :] / acc_ref[0, nv:nv + 1, :]
         - lam * (acc_ref[1, :nv, :] / acc_ref[1, nv:nv + 1, :]))
    o = o * lax.rsqrt(jnp.mean(o * o, axis=0, keepdims=True) + SUBLN_EPS) * g_ref[...]
    o_ref[0] = (o * (1.0 - lambda_init)).T.astype(BF16)


def _diff_attention(qb, kb, vbt, lq1, lk1, lq2, lk2, subln_g, lambda_init, tq, tk):
    B, S, _ = qb.shape
    vec = lambda n: pl.BlockSpec((1, n), lambda b, h, i: (0, 0))
    return pl.pallas_call(
        functools.partial(_diff_kernel, tq=tq, tk=tk, lambda_init=lambda_init),
        grid=(B, B_HEADS, S // tq),
        in_specs=[vec(B_QK_DIM)] * 4 + [pl.BlockSpec((B_V_DIM, 1), lambda b, h, i: (0, 0)),
                  pl.BlockSpec((1, tq, B_V_DIM), lambda b, h, i: (b, i, h)),
                  pl.BlockSpec((1, S, B_V_DIM), lambda b, h, i: (b, 0, h)),
                  pl.BlockSpec((1, B_V_DIM, S), lambda b, h, i: (b, h, 0))],
        out_specs=pl.BlockSpec((1, tq, B_V_DIM), lambda b, h, i: (b, i, h)),
        out_shape=jax.ShapeDtypeStruct((B, S, B_V_WIDTH), BF16),
        scratch_shapes=[pltpu.VMEM((2, B_V_DIM + ONES_ROWS, tq), F32)],
        compiler_params=_cparams(("parallel", "parallel", "arbitrary")),
        name="diff_attn",
    )(lq1, lk1, lq2, lk2, subln_g, qb, kb, vbt)


def _mix_out_kernel(o1_ref, o2_ref, o3_ref, l1_ref, l2_ref, l3_ref, ob_ref, x_ref, w_ref, e_ref, g_ref, b_ref, h_ref):
    ls = (l1_ref[...], l2_ref[...], l3_ref[...])
    mx = jnp.maximum(jnp.maximum(ls[0], ls[1]), ls[2])
    es = [jnp.exp(l - mx) for l in ls]
    den = es[0] + es[1] + es[2]
    e = e_ref[...]
    oa = None
    for ex, o_ref in zip(es, (o1_ref, o2_ref, o3_ref)):
        wt = ex / den
        hi = wt.astype(BF16)
        lo = (wt - hi.astype(F32)).astype(BF16)
        wexp = _dot(hi, e) + _dot(lo, e)
        term = wexp * o_ref[...].astype(F32)
        oa = term if oa is None else oa + term
    y = _dot(oa.astype(BF16), w_ref[:A_WIDTH, :]) + _dot(ob_ref[...], w_ref[A_WIDTH:, :])
    h_ref[...] = _layer_norm(DN_ALPHA * x_ref[...] + y, g_ref[...], b_ref[...])


def _matmul_kernel(a_ref, w_ref, o_ref):
    o_ref[...] = _dot(a_ref[...].astype(BF16), w_ref[...]).astype(o_ref.dtype)


def _mem_kv(mem2, wkv, tm):
    R, D = mem2.shape
    N = wkv.shape[1]
    return pl.pallas_call(
        _matmul_kernel,
        grid=(R // tm,),
        in_specs=[pl.BlockSpec((tm, D), lambda i: (i, 0)), pl.BlockSpec((D, N), lambda i: (0, 0))],
        out_specs=pl.BlockSpec((tm, N), lambda i: (i, 0)),
        out_shape=jax.ShapeDtypeStruct((R, N), BF16),
        compiler_params=_cparams(("parallel",)),
        name="mem_kv",
    )(mem2, wkv)


def _mem_attn_kernel(h_ref, kv_ref, wq_ref, wo_ref, g_ref, b_ref, out_ref, slab_ref):
    h = h_ref[...]
    D = h.shape[1]
    hd = D // M_HEADS
    q = (_dot(h.astype(BF16), wq_ref[...]) * (hd ** -0.5)).astype(BF16)
    heads = []
    for i in range(M_HEADS):
        k = kv_ref[:, i * hd:(i + 1) * hd]
        v = kv_ref[:, D + i * hd:D + (i + 1) * hd]
        s = _dot_nt(q[:, i * hd:(i + 1) * hd], k)
        p = jnp.exp(s - jnp.max(s, axis=1, keepdims=True))
        p = p / jnp.sum(p, axis=1, keepdims=True)
        heads.append(_dot(p.astype(BF16), v).astype(BF16))
    o = jnp.concatenate(heads, axis=1)
    h2 = _layer_norm(DN_ALPHA * h + _dot(o, wo_ref[...]), g_ref[...], b_ref[...])
    out_ref[...] = h2
    _store_slabs(slab_ref, h2)


def _argmax_rows(v, row_iota, n_rows):
    m = jnp.max(v, axis=0, keepdims=True)
    idx = jnp.min(jnp.where(v == m, row_iota, n_rows), axis=0, keepdims=True)
    return m, idx


def _router_kernel(h_ref, whi_ref, wlo_ref, bias_ref, idx_ref, w_ref, rank_ref, cnt_ref, carry_ref):
    @pl.when(pl.program_id(0) == 0)
    def _():
        carry_ref[...] = jnp.zeros_like(carry_ref)

    h = h_ref[...]
    tt = h.shape[0]
    xhi = h.astype(BF16)
    xlo = (h - xhi.astype(F32)).astype(BF16)
    whi, wlo = whi_ref[...], wlo_ref[...]
    logits = _dot_nt(whi, xhi) + (_dot_nt(wlo, xhi) + _dot_nt(whi, xlo))
    scores = jax.nn.sigmoid(logits)
    biased = scores + bias_ref[...]

    gsz = N_EXPERTS // N_GROUPS
    giota = lax.broadcasted_iota(jnp.int32, (gsz, tt), 0)
    gscore = []
    for g in range(N_GROUPS):
        blk = biased[g * gsz:(g + 1) * gsz]
        m1, i1 = _argmax_rows(blk, giota, gsz)
        m2 = jnp.max(jnp.where(giota == i1, NEG_INF, blk), axis=0, keepdims=True)
        gscore.append(m1 + m2)
    gscore = jnp.concatenate(gscore, axis=0)
    riota = lax.broadcasted_iota(jnp.int32, (N_GROUPS, tt), 0)
    gsel = jnp.zeros((N_GROUPS, tt), F32)
    for _ in range(TOPK_GROUPS):
        _, gi = _argmax_rows(gscore, riota, N_GROUPS)
        hit = riota == gi
        gsel = jnp.where(hit, 1.0, gsel)
        gscore = jnp.where(hit, NEG_INF, gscore)
    masked = jnp.concatenate(
        [jnp.where(gsel[g:g + 1] > 0.5, biased[g * gsz:(g + 1) * gsz], NEG_INF) for g in range(N_GROUPS)], axis=0)

    eiota = lax.broadcasted_iota(jnp.int32, (N_EXPERTS, tt), 0)
    chosen = jnp.zeros((N_EXPERTS, tt), F32)
    idxs, ws = [], []
    for _ in range(TOP_K):
        _, ei = _argmax_rows(masked, eiota, N_EXPERTS)
        hit = eiota == ei
        idxs.append(ei)
        ws.append(jnp.sum(jnp.where(hit, scores, 0.0), axis=0, keepdims=True))
        chosen = jnp.where(hit, 1.0, chosen)
        masked = jnp.where(hit, NEG_INF, masked)
    idx = jnp.concatenate(idxs, axis=0)
    w = jnp.concatenate(ws, axis=0)
    w = w / jnp.sum(w, axis=0, keepdims=True) * ROUTED_SCALE

    before = (lax.broadcasted_iota(jnp.int32, (tt, tt), 0) < lax.broadcasted_iota(jnp.int32, (tt, tt), 1))
    within = _dot(chosen.astype(BF16), before.astype(BF16))
    erank = within + carry_ref[...]
    ranks = [jnp.sum(jnp.where(eiota == idxs[k], erank, 0.0), axis=0, keepdims=True) for k in range(TOP_K)]
    carry = carry_ref[...] + jnp.sum(chosen, axis=1, keepdims=True)
    carry_ref[...] = carry

    idx_ref[...] = idx
    w_ref[...] = w
    rank_ref[...] = jnp.concatenate(ranks, axis=0).astype(jnp.int32)
    cnt_ref[...] = carry.astype(jnp.int32)


def _token_stage_kernel(o1_ref, o2_ref, o3_ref, l1_ref, l2_ref, l3_ref, ob_ref, x_ref, wout_ref, e_ref, g1_ref, b1_ref,
                        kv_ref, wq_ref, wo_ref, g2_ref, b2_ref, whi_ref, wlo_ref, bias_ref,
                        h2_ref, slab_ref, idx_ref, w_ref, rank_ref, cnt_ref, h1_ref, carry_ref):
    _mix_out_kernel(o1_ref, o2_ref, o3_ref, l1_ref, l2_ref, l3_ref, ob_ref, x_ref, wout_ref, e_ref, g1_ref, b1_ref, h1_ref)
    _mem_attn_kernel(h1_ref, kv_ref, wq_ref, wo_ref, g2_ref, b2_ref, h2_ref, slab_ref)
    _router_kernel(h2_ref, whi_ref, wlo_ref, bias_ref, idx_ref, w_ref, rank_ref, cnt_ref, carry_ref)


def _token_stage(os_, ls_, ob, x2, w_out, expand, g1, b1, kv, wq, wo, g2, b2, whi, wlo, bias, tm, tiles_per_batch):
    T, D = x2.shape
    M = kv.shape[0] // (T // (tm * tiles_per_batch))
    row = lambda i: (i, 0)
    col = lambda i: (0, i)
    const = lambda i: (0, 0)
    full = lambda a: pl.BlockSpec(a.shape, const)
    return pl.pallas_call(
        _token_stage_kernel,
        grid=(T // tm,),
        in_specs=[pl.BlockSpec((tm, A_WIDTH), row)] * 3 + [pl.BlockSpec((tm, LANES), row)] * 3 + [
            pl.BlockSpec((tm, B_V_WIDTH), row), pl.BlockSpec((tm, D), row), full(w_out), full(expand), full(g1), full(b1),
            pl.BlockSpec((M, 2 * D), lambda i: (i // tiles_per_batch, 0)), full(wq), full(wo), full(g2), full(b2),
            full(whi), full(wlo), full(bias)],
        out_specs=[pl.BlockSpec((tm, D), row), pl.BlockSpec((tm * _slab_rows(D), LANES), row),
                   pl.BlockSpec((TOP_K, tm), col), pl.BlockSpec((TOP_K, tm), col), pl.BlockSpec((TOP_K, tm), col),
                   pl.BlockSpec((N_EXPERTS, 1), const)],
        out_shape=[jax.ShapeDtypeStruct((T, D), F32), jax.ShapeDtypeStruct((T * _slab_rows(D), LANES), F32),
                   jax.ShapeDtypeStruct((TOP_K, T), jnp.int32), jax.ShapeDtypeStruct((TOP_K, T), F32),
                   jax.ShapeDtypeStruct((TOP_K, T), jnp.int32), jax.ShapeDtypeStruct((N_EXPERTS, 1), jnp.int32)],
        scratch_shapes=[pltpu.VMEM((tm, D), F32), pltpu.VMEM((N_EXPERTS, 1), F32)],
        compiler_params=_cparams(("arbitrary",)),
        name="token_stage",
    )(*os_, *ls_, ob, x2, w_out, expand, g1, b1, kv, wq, wo, g2, b2, whi, wlo, bias)


def _dest_kernel(idx_ref, rank_ref, start_ref, dest_ref):
    idx = idx_ref[...]
    tt = idx.shape[1]
    eiota = lax.broadcasted_iota(jnp.int32, (N_EXPERTS, tt), 0)
    start = start_ref[...].astype(F32)
    rows = [jnp.sum(jnp.where(eiota == idx[k:k + 1], start, 0.0), axis=0, keepdims=True) for k in range(TOP_K)]
    dest_ref[...] = jnp.concatenate(rows, axis=0).astype(jnp.int32) + rank_ref[...]


def _dest_rows(idx, rank, start, tt):
    T = idx.shape[1]
    col = lambda i: (0, i)
    return pl.pallas_call(
        _dest_kernel,
        grid=(T // tt,),
        in_specs=[pl.BlockSpec((TOP_K, tt), col), pl.BlockSpec((TOP_K, tt), col),
                  pl.BlockSpec((N_EXPERTS, 1), lambda i: (0, 0))],
        out_specs=pl.BlockSpec((TOP_K, tt), col),
        out_shape=jax.ShapeDtypeStruct((TOP_K, T), jnp.int32),
        compiler_params=_cparams(("parallel",)),
        name="dest_rows",
    )(idx, rank, start)


def _dispatch_kernel(dest_ref, x_ref, xs_ref, sem, *, sr):
    tt = x_ref.shape[0] // sr

    def issue(t, c):
        src = x_ref.at[pl.ds(pl.multiple_of(t * sr, sr), sr)]
        for k in range(TOP_K):
            dst = xs_ref.at[pl.ds(pl.multiple_of(dest_ref[k, t] * sr, sr), sr)]
            pltpu.make_async_copy(src, dst, sem).start(priority=k % 2)
        return c

    lax.fori_loop(0, tt, issue, 0)
    n = TOP_K * tt * sr
    pltpu.make_async_copy(xs_ref.at[pl.ds(0, n)], xs_ref.at[pl.ds(0, n)], sem).wait()


def _dispatch(dest, x_slabs, n_rows, tt, sr):
    T = x_slabs.shape[0] // sr
    return pl.pallas_call(
        functools.partial(_dispatch_kernel, sr=sr),
        grid=(T // tt,),
        in_specs=[pl.BlockSpec((TOP_K, tt), lambda i: (0, i), memory_space=pltpu.SMEM),
                  pl.BlockSpec((tt * sr, LANES), lambda i: (i, 0))],
        out_specs=pl.BlockSpec(memory_space=pl.ANY),
        out_shape=jax.ShapeDtypeStruct((n_rows * sr, LANES), F32),
        scratch_shapes=[pltpu.SemaphoreType.DMA],
        compiler_params=_cparams(("arbitrary",)),
        name="dispatch",
    )(dest, x_slabs)


def _experts_kernel(te_ref, tv_ref, nu_ref, x_ref, wg_ref, wu_ref, wd_ref, y_ref, wg_bf, wu_bf, wd_bf):
    i = pl.program_id(0)

    @pl.when(i < nu_ref[0])
    def _():
        @pl.when((i == 0) | (te_ref[i] != te_ref[jnp.maximum(i - 1, 0)]))
        def _():
            wg_bf[...] = wg_ref[0].astype(BF16)
            wu_bf[...] = wu_ref[0].astype(BF16)
            wd_bf[...] = wd_ref[0].astype(BF16)

        d = wg_bf.shape[0]
        tm = x_ref.shape[0] // _slab_rows(d)
        rows = lax.broadcasted_iota(jnp.int32, (tm, 1), 0)
        x = jnp.where(rows < tv_ref[i], _load_slabs(x_ref, tm, d), 0.0).astype(BF16)
        g = _dot(x, wg_bf[...])
        act = (g * jax.nn.sigmoid(g) * _dot(x, wu_bf[...])).astype(BF16)
        _store_slabs(y_ref, _dot(act, wd_bf[...]))


def _experts(tile_e, tile_valid, n_used, xs, w_gate, w_up, w_down, tm):
    E, D, F = w_gate.shape
    sr = _slab_rows(D)
    n_tiles = xs.shape[0] // (tm * sr)

    def tile(i, te, tv, nu):
        return (jnp.minimum(i, nu[0] - 1), 0)

    def expert(i, te, tv, nu):
        return (te[jnp.minimum(i, nu[0] - 1)], 0, 0)

    return pl.pallas_call(
        _experts_kernel,
        grid_spec=pltpu.PrefetchScalarGridSpec(
            num_scalar_prefetch=3,
            grid=(n_tiles,),
            in_specs=[pl.BlockSpec((tm * sr, LANES), tile), pl.BlockSpec((1, D, F), expert),
                      pl.BlockSpec((1, D, F), expert), pl.BlockSpec((1, F, D), expert)],
            out_specs=pl.BlockSpec((tm * sr, LANES), tile),
            scratch_shapes=[pltpu.VMEM((D, F), BF16), pltpu.VMEM((D, F), BF16), pltpu.VMEM((F, D), BF16)],
        ),
        out_shape=jax.ShapeDtypeStruct(xs.shape, F32),
        compiler_params=_cparams(("arbitrary",)),
        name="experts",
    )(tile_e, tile_valid, n_used, xs, w_gate, w_up, w_down)


def _combine_kernel(dest_ref, nxt_ref, w_ref, h_ref, ys_ref, sg_ref, su_ref, sd_ref, g_ref, b_ref, out_ref, buf, sem):
    i = pl.program_id(0)
    last = pl.num_programs(0) - 1
    tt, d = h_ref.shape
    sr = _slab_rows(d)
    slot = i % 2

    def gather(idx_ref, to):
        def issue(t, c):
            for k in range(TOP_K):
                src = ys_ref.at[pl.ds(pl.multiple_of(idx_ref[k, t] * sr, sr), sr)]
                dst = buf.at[to, k, pl.ds(pl.multiple_of(t * sr, sr), sr)]
                pltpu.make_async_copy(src, dst, sem.at[to]).start(priority=k % 2)
            return c

        lax.fori_loop(0, tt, issue, 0)

    @pl.when(i == 0)
    def _():
        gather(dest_ref, 0)

    @pl.when(i < last)
    def _():
        gather(nxt_ref, 1 - slot)

    h = h_ref[...]
    hb = h.astype(BF16)
    gate = _dot(hb, sg_ref[...])
    act = (gate * jax.nn.sigmoid(gate) * _dot(hb, su_ref[...])).astype(BF16)
    shared = _dot(act, sd_ref[...])

    for k in range(TOP_K):
        pltpu.make_async_copy(ys_ref.at[pl.ds(0, tt * sr)], buf.at[slot, k], sem.at[slot]).wait()

    w = w_ref[...]
    wk = [jnp.broadcast_to(w[:, k:k + 1], (tt, LANES)) for k in range(TOP_K)]
    chunks = []
    for c in range(sr):
        acc = buf[slot, 0, pl.ds(c, tt, stride=sr), :] * wk[0]
        for k in range(1, TOP_K):
            acc = acc + buf[slot, k, pl.ds(c, tt, stride=sr), :] * wk[k]
        chunks.append(acc)
    routed = jnp.concatenate(chunks, axis=1)
    out_ref[...] = _layer_norm(DN_ALPHA * h + (routed + shared), g_ref[...], b_ref[...])


def _combine(dest, w_tok, h2, ys, sg, su, sd, g, b, tt):
    T, D = h2.shape
    sr = _slab_rows(D)
    row = lambda i: (i, 0)
    const = lambda i: (0, 0)
    n = T // tt
    return pl.pallas_call(
        _combine_kernel,
        grid=(n,),
        in_specs=[pl.BlockSpec((TOP_K, tt), lambda i: (0, i), memory_space=pltpu.SMEM),
                  pl.BlockSpec((TOP_K, tt), lambda i: (0, jnp.minimum(i + 1, n - 1)), memory_space=pltpu.SMEM),
                  pl.BlockSpec((tt, TOP_K), row), pl.BlockSpec((tt, D), row),
                  pl.BlockSpec(memory_space=pl.ANY),
                  pl.BlockSpec(sg.shape, const), pl.BlockSpec(su.shape, const), pl.BlockSpec(sd.shape, const),
                  pl.BlockSpec((1, D), const), pl.BlockSpec((1, D), const)],
        out_specs=pl.BlockSpec((tt, D), row),
        out_shape=jax.ShapeDtypeStruct((T, D), F32),
        scratch_shapes=[pltpu.VMEM((2, TOP_K, tt * sr, LANES), F32), pltpu.SemaphoreType.DMA((2,))],
        compiler_params=_cparams(("arbitrary",)),
        name="combine",
    )(dest, dest, w_tok, h2, ys, sg, su, sd, g, b)


def _rope_lane_tables(positions):
    half = ROT_DIM // 2
    inv = ROPE_THETA ** (-jnp.arange(0, ROT_DIM, 2, dtype=F32) / ROT_DIM)
    ang = positions.astype(F32).reshape(-1, 1) * inv
    cos, sin = jnp.cos(ang), jnp.sin(ang)
    T = cos.shape[0]
    rest = A_HEAD_DIM - ROT_DIM
    one_head = lambda first, second, fill: jnp.concatenate(
        [first, second, jnp.full((T, rest), fill, F32)], axis=1)
    zeros = jnp.zeros_like(sin)
    cc = one_head(cos, cos, 1.0)
    s1 = one_head(-sin, zeros, 0.0)
    s2 = one_head(zeros, sin, 0.0)
    rep = LANES // A_HEAD_DIM
    return jnp.tile(cc, (1, rep)), jnp.tile(s1, (1, rep)), jnp.tile(s2, (1, rep))


def _to_sub(t, d):
    B, S, W = t.shape
    return t.reshape(B, S // d, d, W).transpose(0, 2, 1, 3).reshape(B * d, S // d, W)


def _from_sub(t, d, B):
    G, L, W = t.shape
    return t.reshape(B, d, L, W).transpose(0, 2, 1, 3).reshape(B * L * d, W)


def _layer(x, mem, cc, s1, s2, l, w_in, w_out, lq1, lk1, lq2, lk2, subln_g, ln1_g, ln1_b,
           wq_mem, wkv_mem, wo_mem, ln2_g, ln2_b, w_router, e_bias, w_gate, w_up, w_down,
           ws_gate, ws_up, ws_down, ln3_g, ln3_b):
    B, S, D = x.shape
    T = B * S
    lambda_init = 0.8 - 0.6 * math.exp(-0.3 * l)
    x2 = x.reshape(T, D)
    tm = min(512, S)

    w_bf = w_in.astype(BF16)
    o_va, o_qb, o_vb = 2 * A_WIDTH, 3 * A_WIDTH, 3 * A_WIDTH + 2 * B_QK_WIDTH
    w_qk = jnp.concatenate([w_bf[:, :o_va], w_bf[:, o_qb:o_vb]], axis=1)
    w_vt = jnp.concatenate([w_bf[:, o_va:o_qb], w_bf[:, o_vb:]], axis=1).T
    qa, ka, qb, kb, vat, vbt = _in_proj(x2, w_qk, w_vt, cc, s1, s2, tm, S)

    outs, lses = [], []
    for window, d in A_PAIRS:
        assert window // d == BAND
        L = S // d
        q3, k3 = (_to_sub(t.reshape(B, S, A_WIDTH), d) for t in (qa, ka))
        vt3 = vat.reshape(B, A_WIDTH, L, d).transpose(0, 3, 1, 2).reshape(B * d, A_WIDTH, L)
        o, lse = _dilated(q3, k3, vt3, min(2 * BAND, L))
        outs.append(_from_sub(o, d, B))
        lse = lse.reshape(B, d, A_HEADS, L).transpose(0, 3, 1, 2).reshape(T, A_HEADS)
        lses.append(jnp.pad(lse, ((0, 0), (0, LANES - A_HEADS))))

    row = lambda v: v.reshape(1, -1).astype(F32)
    ob = _diff_attention(qb.reshape(B, S, -1), kb.reshape(B, S, -1), vbt, row(lq1), row(lk1), row(lq2), row(lk2),
                         subln_g.reshape(-1, 1).astype(F32), lambda_init, min(1024, S), min(512, S))

    expand = (jnp.arange(LANES)[:, None] == (jnp.arange(A_WIDTH)[None, :] // A_HEAD_DIM)).astype(BF16)
    M = mem.shape[1]
    kv = _mem_kv(mem.reshape(B * M, D), wkv_mem.astype(BF16), min(512, B * M))
    wr_t = w_router.T
    wr_hi = wr_t.astype(BF16)
    wr_lo = (wr_t - wr_hi.astype(F32)).astype(BF16)
    h2, h2_slabs, idx, w_route, rank, counts = _token_stage(
        outs, lses, ob.reshape(T, -1), x2, w_out.astype(BF16), expand, row(ln1_g), row(ln1_b),
        kv, wq_mem.astype(BF16), wo_mem.astype(BF16), row(ln2_g), row(ln2_b),
        wr_hi, wr_lo, e_bias.reshape(-1, 1).astype(F32), tm, S // tm)

    tile_rows = 512
    counts = counts.reshape(-1)
    padded = (counts + tile_rows - 1) // tile_rows * tile_rows
    pend = jnp.cumsum(padded)
    pstart = pend - padded
    n_tiles = (T * TOP_K) // tile_rows + N_EXPERTS
    n_rows = n_tiles * tile_rows
    tile_lo = jnp.arange(n_tiles, dtype=jnp.int32) * tile_rows
    tile_e = jnp.minimum(jnp.sum(pend[None, :] <= tile_lo[:, None], axis=1), N_EXPERTS - 1).astype(jnp.int32)
    tile_valid = jnp.clip(counts[tile_e] - (tile_lo - pstart[tile_e]), 0, tile_rows).astype(jnp.int32)
    n_used = (pend[-1] // tile_rows).astype(jnp.int32).reshape(1)

    dest = _dest_rows(idx, rank, pstart.reshape(-1, 1).astype(jnp.int32), min(2048, T))
    xs = _dispatch(dest, h2_slabs, n_rows, min(256, S), _slab_rows(D))
    ys = _experts(tile_e, tile_valid, n_used, xs, w_gate, w_up, w_down, tile_rows)
    out = _combine(dest, w_route.T, h2, ys, ws_gate.astype(BF16), ws_up.astype(BF16), ws_down.astype(BF16),
                   row(ln3_g), row(ln3_b), min(256, S))
    return out.reshape(B, S, D)


def kernel(x, mem, positions, w_in, w_out, lambda_q1, lambda_k1, lambda_q2, lambda_k2, subln_g, ln1_g, ln1_b,
           wq_mem, wkv_mem, wo_mem, ln2_g, ln2_b, w_router, e_bias, w_gate, w_up, w_down, ws_gate, ws_up, ws_down,
           ln3_g, ln3_b):
    cc, s1, s2 = _rope_lane_tables(positions)
    h = x
    for l in range(w_in.shape[0]):
        h = _layer(h, mem, cc, s1, s2, l, w_in[l], w_out[l], lambda_q1[l], lambda_k1[l], lambda_q2[l],
                   lambda_k2[l], subln_g[l], ln1_g[l], ln1_b[l], wq_mem[l], wkv_mem[l], wo_mem[l],
                   ln2_g[l], ln2_b[l], w_router[l], e_bias[l], w_gate[l], w_up[l], w_down[l],
                   ws_gate[l], ws_up[l], ws_down[l], ln3_g[l], ln3_b[l])
    return h
```

```python
import functools
import math

import jax
import jax.numpy as jnp
from jax import lax
from jax.experimental import pallas as pl
from jax.experimental.pallas import tpu as pltpu

A_HEADS = 8
A_HEAD_DIM = 64
A_PAIRS = ((128, 1), (512, 4), (2048, 16))
B_HEADS = 4
B_QK_DIM = 64
B_V_DIM = 2 * B_QK_DIM
A_WIDTH = A_HEADS * A_HEAD_DIM
B_QK_WIDTH = B_HEADS * 2 * B_QK_DIM
B_V_WIDTH = B_HEADS * B_V_DIM
ROPE_THETA = 500000.0
ROT_DIM = A_HEAD_DIM // 4
M_HEADS = 4
N_EXPERTS = 256
TOP_K = 8
N_GROUPS = 8
TOPK_GROUPS = 4
ROUTED_SCALE = 2.5
LN_EPS = 1e-5
SUBLN_EPS = 1e-5
DEPTH = 1
DN_ALPHA = (2 * DEPTH) ** 0.25

LANES = 128
BAND = 128
ONES_ROWS = 16
VMEM_LIMIT = 56 * 1024 * 1024

BF16 = jnp.bfloat16
F32 = jnp.float32
NEG_INF = float("-inf")


def _cparams(sem):
    return pltpu.CompilerParams(dimension_semantics=sem, vmem_limit_bytes=VMEM_LIMIT)


def _dot(a, b):
    return jnp.dot(a, b, preferred_element_type=F32)


def _dot_nt(a, b):
    return lax.dot_general(a, b, (((1,), (1,)), ((), ())), preferred_element_type=F32)


def _layer_norm(x, g, b):
    mu = jnp.mean(x, axis=-1, keepdims=True)
    xc = x - mu
    var = jnp.mean(xc * xc, axis=-1, keepdims=True)
    return xc * lax.rsqrt(var + LN_EPS) * g + b


def _slab_rows(d):
    return d // LANES


def _store_slabs(ref, x):
    n, d = x.shape
    s = _slab_rows(d)
    for c in range(s):
        ref[pl.ds(c, n, stride=s), :] = x[:, c * LANES:(c + 1) * LANES]


def _load_slabs(ref, n, d):
    s = _slab_rows(d)
    return jnp.concatenate([ref[pl.ds(c, n, stride=s), :] for c in range(s)], axis=1)


def _to_strided(nat_ref, t, sub_refs, dils):
    tm, w = t.shape
    chunks = range(w // LANES)
    for c in chunks:
        nat_ref[c] = t[:, c * LANES:(c + 1) * LANES]
    for d, ref in zip(dils, sub_refs):
        for r in range(d):
            rows = [nat_ref[c, pl.ds(r, tm // d, stride=d), :] for c in chunks]
            ref[0, r] = jnp.concatenate(rows, axis=1).astype(ref.dtype)


def _from_strided(nat_ref, sub_ref, d):
    n, w = sub_ref.shape[2], sub_ref.shape[3]
    chunks = range(w // LANES)
    for r in range(d):
        for c in chunks:
            nat_ref[c, pl.ds(r, n, stride=d), :] = sub_ref[0, r, :, c * LANES:(c + 1) * LANES].astype(F32)
    return jnp.concatenate([nat_ref[c] for c in chunks], axis=1)


def _in_proj_kernel(x_ref, w_ref, wvt_ref, c_ref, s1_ref, s2_ref, qa_ref, ka_ref, qb_ref, kb_ref, vat_ref, vbt_ref,
                    *rest, dils):
    sub_q, sub_k, nat_ref = rest[0:-1:2], rest[1:-1:2], rest[-1]
    x = x_ref[...].astype(BF16)
    cc, s1, s2 = c_ref[...], s1_ref[...], s2_ref[...]

    def rope(t):
        outs = []
        for c in range(t.shape[1] // LANES):
            tc = t[:, c * LANES:(c + 1) * LANES]
            nxt = pltpu.roll(tc, LANES - ROT_DIM // 2, 1)
            prv = pltpu.roll(tc, ROT_DIM // 2, 1)
            outs.append(tc * cc + nxt * s1 + prv * s2)
        return jnp.concatenate(outs, axis=1)

    def proj(lo, width):
        return _dot(x, w_ref[:, lo:lo + width])

    o = 0
    qa = rope(proj(o, A_WIDTH)) * (A_HEAD_DIM ** -0.5); o += A_WIDTH
    qa_ref[...] = qa.astype(BF16)
    _to_strided(nat_ref, qa, sub_q, dils)
    ka = rope(proj(o, A_WIDTH)); o += A_WIDTH
    ka_ref[...] = ka.astype(BF16)
    _to_strided(nat_ref, ka, sub_k, dils)
    qb_ref[...] = (rope(proj(o, B_QK_WIDTH)) * (B_QK_DIM ** -0.5)).astype(BF16); o += B_QK_WIDTH
    kb_ref[...] = rope(proj(o, B_QK_WIDTH)).astype(BF16)
    vat_ref[0] = _dot_nt(wvt_ref[:A_WIDTH, :], x).astype(BF16)
    vbt_ref[0] = _dot_nt(wvt_ref[A_WIDTH:, :], x).astype(BF16)


def _in_proj(x2, w_qk, w_vt, cc, s1, s2, tm, seq, dils):
    T, D = x2.shape
    nt = seq // tm
    nb = T // seq
    row = lambda i: (i, 0)
    const = lambda i: (0, 0)
    tcol = lambda i: (i // nt, 0, i % nt)
    outs = [jax.ShapeDtypeStruct((T, A_WIDTH), BF16)] * 2 + [jax.ShapeDtypeStruct((T, B_QK_WIDTH), BF16)] * 2
    outs_t = [jax.ShapeDtypeStruct((nb, A_WIDTH, seq), BF16), jax.ShapeDtypeStruct((nb, B_V_WIDTH, seq), BF16)]
    outs_s = [jax.ShapeDtypeStruct((nb, d, seq // d, A_WIDTH), BF16) for d in dils for _ in range(2)]
    return pl.pallas_call(
        functools.partial(_in_proj_kernel, dils=dils),
        grid=(T // tm,),
        in_specs=[pl.BlockSpec((tm, D), row), pl.BlockSpec(w_qk.shape, const), pl.BlockSpec(w_vt.shape, const),
                  pl.BlockSpec((tm, LANES), row), pl.BlockSpec((tm, LANES), row), pl.BlockSpec((tm, LANES), row)],
        out_specs=[pl.BlockSpec((tm, o.shape[1]), row) for o in outs] + [
            pl.BlockSpec((1, o.shape[1], tm), tcol) for o in outs_t] + [
            pl.BlockSpec((1, o.shape[1], tm // o.shape[1], A_WIDTH), lambda i: (i // nt, 0, i % nt, 0)) for o in outs_s],
        out_shape=outs + outs_t + outs_s,
        scratch_shapes=[pltpu.VMEM((A_WIDTH // LANES, tm, LANES), F32)],
        compiler_params=_cparams(("parallel",)),
        name="in_proj",
    )(x2, w_qk, w_vt, cc, s1, s2)


def _dilated_kernel(q_ref, kp_ref, kc_ref, vtp_ref, vtc_ref, o_ref, lse_ref):
    n = pl.program_id(1)
    tq = q_ref.shape[1]
    nk = BAND + tq
    two = LANES // A_HEAD_DIM
    kr = lax.broadcasted_iota(jnp.int32, (nk, two * tq), 0)
    qi = lax.broadcasted_iota(jnp.int32, (nk, two * tq), 1) % tq
    ok = (kr >= jnp.where(n > 0, qi, jnp.maximum(qi, BAND))) & (kr <= qi + BAND)
    lane = lax.broadcasted_iota(jnp.int32, (tq, LANES), 1)
    row = lax.broadcasted_iota(jnp.int32, (LANES, tq), 0)
    chunks = [slice(c * LANES, (c + 1) * LANES) for c in range(A_WIDTH // LANES)]
    scores = []
    for sl in chunks:
        q = q_ref[0, :, sl]
        zero = jnp.zeros_like(q)
        q2 = jnp.concatenate([jnp.where((lane >= h * A_HEAD_DIM) & (lane < (h + 1) * A_HEAD_DIM), q, zero)
                              for h in range(two)], axis=0)
        kband = jnp.concatenate([kp_ref[0, :, sl], kc_ref[0, :, sl]], axis=0)
        scores.append(_dot_nt(kband, q2))
    probs, lses = [], []
    for st in scores:
        s = jnp.where(ok, st, NEG_INF)
        m = jnp.max(s, axis=0, keepdims=True)
        p = jnp.exp(s - m)
        den = jnp.sum(p, axis=0, keepdims=True)
        probs.append((p.astype(BF16), den))
        lse = m + jnp.log(den)
        lses += [lse[:, h * tq:(h + 1) * tq] for h in range(two)]
    for sl, (p, den) in zip(chunks, probs):
        vt = jnp.concatenate([vtp_ref[0, sl, :], vtc_ref[0, sl, :]], axis=1)
        ot = _dot(vt, p) / den
        out_t = ot[:, :tq]
        for h in range(1, two):
            out_t = jnp.where(row >= h * A_HEAD_DIM, ot[:, h * tq:(h + 1) * tq], out_t)
        o_ref[0, :, sl] = out_t.T.astype(BF16)
    lse_ref[0] = jnp.concatenate(lses, axis=0)


def _dilated(q, k, vt, tq):
    G, L, W = q.shape
    per = tq // BAND
    cur = lambda g, n: (g, n, 0)
    prev = lambda g, n: (g, jnp.maximum(n * per - 1, 0), 0)
    cur_t = lambda g, n: (g, 0, n)
    prev_t = lambda g, n: (g, 0, jnp.maximum(n * per - 1, 0))
    return pl.pallas_call(
        _dilated_kernel,
        grid=(G, L // tq),
        in_specs=[pl.BlockSpec((1, tq, W), cur), pl.BlockSpec((1, BAND, W), prev), pl.BlockSpec((1, tq, W), cur),
                  pl.BlockSpec((1, W, BAND), prev_t), pl.BlockSpec((1, W, tq), cur_t)],
        out_specs=[pl.BlockSpec((1, tq, W), cur), pl.BlockSpec((1, A_HEADS, tq), cur_t)],
        out_shape=[jax.ShapeDtypeStruct((G, L, W), BF16), jax.ShapeDtypeStruct((G, A_HEADS, L), F32)],
        compiler_params=_cparams(("parallel", "parallel")),
        name="dilated",
    )(q, k, k, vt, vt)


def _diff_kernel(lq1_ref, lk1_ref, lq2_ref, lk2_ref, g_ref, q_ref, k_ref, vt_ref, o_ref, acc_ref, *, tq, tk, lambda_init):
    iq = pl.program_id(2)
    q = q_ref[0]
    lane = lax.broadcasted_iota(jnp.int32, q.shape, 1)
    zero = jnp.zeros_like(q)
    q2 = jnp.concatenate([jnp.where(lane < B_QK_DIM, q, zero), jnp.where(lane >= B_QK_DIM, q, zero)], axis=0)
    acc_ref[...] = jnp.zeros_like(acc_ref)

    def block(j, carry, diag):
        start = pl.multiple_of(j * tk, tk)
        kb = k_ref[0, pl.ds(start, tk), :]
        vt = jnp.concatenate([vt_ref[0, :, pl.ds(start, tk)], jnp.ones((ONES_ROWS, tk), BF16)], axis=0)
        st = _dot_nt(kb, q2)
        if diag is not None:
            ki = lax.broadcasted_iota(jnp.int32, (tk, tq), 0) + diag
            qi = lax.broadcasted_iota(jnp.int32, (tk, tq), 1)
            ok = ki <= qi
        out, work = [], []
        for c in range(2):
            m = carry[c]
            s = st[:, c * tq:(c + 1) * tq]
            if diag is not None:
                s = jnp.where(ok, s, NEG_INF)
            m_new = jnp.maximum(m, jnp.max(s, axis=0, keepdims=True))
            out.append(m_new)
            work.append((jnp.exp(m - m_new), jnp.exp(s - m_new).astype(BF16)))
        for c, (a, p) in enumerate(work):
            acc_ref[c] = a * acc_ref[c] + _dot(vt, p)
        return tuple(out)

    init = tuple(jnp.full((1, tq), NEG_INF, F32) for _ in range(2))
    per_tile = tq // tk
    carry = lax.fori_loop(0, iq * per_tile, lambda j, cr: block(j, cr, None), init)
    for r in range(per_tile):
        carry = block(iq * per_tile + r, carry, r * tk)

    lam = (jnp.exp(jnp.sum(lq1_ref[...] * lk1_ref[...], axis=1, keepdims=True))
           - jnp.exp(jnp.sum(lq2_ref[...] * lk2_ref[...], axis=1, keepdims=True)) + lambda_init)
    nv = B_V_DIM
    o = (acc_ref[0, :nv, :] / acc_ref[0, nv:nv + 1, :]
         - lam * (acc_ref[1, :nv, :] / acc_ref[1, nv:nv + 1, :]))
    o = o * lax.rsqrt(jnp.mean(o * o, axis=0, keepdims=True) + SUBLN_EPS) * g_ref[...]
    o_ref[0] = (o * (1.0 - lambda_init)).T.astype(BF16)


def _diff_attention(qb, kb, vbt, lq1, lk1, lq2, lk2, subln_g, lambda_init, tq, tk):
    B, S, _ = qb.shape
    vec = lambda n: pl.BlockSpec((1, n), lambda b, h, i: (0, 0))
    return pl.pallas_call(
        functools.partial(_diff_kernel, tq=tq, tk=tk, lambda_init=lambda_init),
        grid=(B, B_HEADS, S // tq),
        in_specs=[vec(B_QK_DIM)] * 4 + [pl.BlockSpec((B_V_DIM, 1), lambda b, h, i: (0, 0)),
                  pl.BlockSpec((1, tq, B_V_DIM), lambda b, h, i: (b, i, h)),
                  pl.BlockSpec((1, S, B_V_DIM), lambda b, h, i: (b, 0, h)),
                  pl.BlockSpec((1, B_V_DIM, S), lambda b, h, i: (b, h, 0))],
        out_specs=pl.BlockSpec((1, tq, B_V_DIM), lambda b, h, i: (b, i, h)),
        out_shape=jax.ShapeDtypeStruct((B, S, B_V_WIDTH), BF16),
        scratch_shapes=[pltpu.VMEM((2, B_V_DIM + ONES_ROWS, tq), F32)],
        compiler_params=_cparams(("parallel", "parallel", "arbitrary")),
        name="diff_attn",
    )(lq1, lk1, lq2, lk2, subln_g, qb, kb, vbt)


def _mix_out_kernel(o1_ref, o2_ref, o3_ref, l1_ref, l2_ref, l3_ref, ob_ref, x_ref, w_ref, e_ref, g_ref, b_ref, h_ref,
                    nat_ref):
    ls = (l1_ref[...], l2_ref[...], l3_ref[...])
    mx = jnp.maximum(jnp.maximum(ls[0], ls[1]), ls[2])
    es = [jnp.exp(l - mx) for l in ls]
    den = es[0] + es[1] + es[2]
    e = e_ref[...]
    oa = None
    for ex, o_ref, (_, d) in zip(es, (o1_ref, o2_ref, o3_ref), A_PAIRS):
        wt = ex / den
        hi = wt.astype(BF16)
        lo = (wt - hi.astype(F32)).astype(BF16)
        wexp = _dot(hi, e) + _dot(lo, e)
        o = o_ref[...].astype(F32) if d == 1 else _from_strided(nat_ref, o_ref, d)
        term = wexp * o
        oa = term if oa is None else oa + term
    y = _dot(oa.astype(BF16), w_ref[:A_WIDTH, :]) + _dot(ob_ref[...], w_ref[A_WIDTH:, :])
    h_ref[...] = _layer_norm(DN_ALPHA * x_ref[...] + y, g_ref[...], b_ref[...])


def _matmul_kernel(a_ref, w_ref, o_ref):
    o_ref[...] = _dot(a_ref[...].astype(BF16), w_ref[...]).astype(o_ref.dtype)


def _mem_kv(mem2, wkv, tm):
    R, D = mem2.shape
    N = wkv.shape[1]
    return pl.pallas_call(
        _matmul_kernel,
        grid=(R // tm,),
        in_specs=[pl.BlockSpec((tm, D), lambda i: (i, 0)), pl.BlockSpec((D, N), lambda i: (0, 0))],
        out_specs=pl.BlockSpec((tm, N), lambda i: (i, 0)),
        out_shape=jax.ShapeDtypeStruct((R, N), BF16),
        compiler_params=_cparams(("parallel",)),
        name="mem_kv",
    )(mem2, wkv)


def _mem_attn_kernel(h_ref, kv_ref, wq_ref, wo_ref, g_ref, b_ref, out_ref, slab_ref):
    h = h_ref[...]
    D = h.shape[1]
    hd = D // M_HEADS
    q = (_dot(h.astype(BF16), wq_ref[...]) * (hd ** -0.5)).astype(BF16)
    heads = []
    for i in range(M_HEADS):
        k = kv_ref[:, i * hd:(i + 1) * hd]
        v = kv_ref[:, D + i * hd:D + (i + 1) * hd]
        s = _dot_nt(q[:, i * hd:(i + 1) * hd], k)
        p = jnp.exp(s - jnp.max(s, axis=1, keepdims=True))
        p = p / jnp.sum(p, axis=1, keepdims=True)
        heads.append(_dot(p.astype(BF16), v).astype(BF16))
    o = jnp.concatenate(heads, axis=1)
    h2 = _layer_norm(DN_ALPHA * h + _dot(o, wo_ref[...]), g_ref[...], b_ref[...])
    out_ref[...] = h2
    _store_slabs(slab_ref, h2)


def _argmax_rows(v, row_iota, n_rows):
    m = jnp.max(v, axis=0, keepdims=True)
    idx = jnp.min(jnp.where(v == m, row_iota, n_rows), axis=0, keepdims=True)
    return m, idx


def _router_kernel(h_ref, whi_ref, wlo_ref, bias_ref, idx_ref, w_ref, rank_ref, cnt_ref, carry_ref):
    @pl.when(pl.program_id(0) == 0)
    def _():
        carry_ref[...] = jnp.zeros_like(carry_ref)

    h = h_ref[...]
    tt = h.shape[0]
    xhi = h.astype(BF16)
    xlo = (h - xhi.astype(F32)).astype(BF16)
    whi, wlo = whi_ref[...], wlo_ref[...]
    logits = _dot_nt(whi, xhi) + (_dot_nt(wlo, xhi) + _dot_nt(whi, xlo))
    scores = jax.nn.sigmoid(logits)
    biased = scores + bias_ref[...]

    gsz = N_EXPERTS // N_GROUPS
    giota = lax.broadcasted_iota(jnp.int32, (gsz, tt), 0)
    gscore = []
    for g in range(N_GROUPS):
        blk = biased[g * gsz:(g + 1) * gsz]
        m1, i1 = _argmax_rows(blk, giota, gsz)
        m2 = jnp.max(jnp.where(giota == i1, NEG_INF, blk), axis=0, keepdims=True)
        gscore.append(m1 + m2)
    gscore = jnp.concatenate(gscore, axis=0)
    riota = lax.broadcasted_iota(jnp.int32, (N_GROUPS, tt), 0)
    gsel = jnp.zeros((N_GROUPS, tt), F32)
    for _ in range(TOPK_GROUPS):
        _, gi = _argmax_rows(gscore, riota, N_GROUPS)
        hit = riota == gi
        gsel = jnp.where(hit, 1.0, gsel)
        gscore = jnp.where(hit, NEG_INF, gscore)
    masked = jnp.concatenate(
        [jnp.where(gsel[g:g + 1] > 0.5, biased[g * gsz:(g + 1) * gsz], NEG_INF) for g in range(N_GROUPS)], axis=0)

    eiota = lax.broadcasted_iota(jnp.int32, (N_EXPERTS, tt), 0)
    chosen = jnp.zeros((N_EXPERTS, tt), F32)
    idxs, ws = [], []
    for _ in range(TOP_K):
        _, ei = _argmax_rows(masked, eiota, N_EXPERTS)
        hit = eiota == ei
        idxs.append(ei)
        ws.append(jnp.sum(jnp.where(hit, scores, 0.0), axis=0, keepdims=True))
        chosen = jnp.where(hit, 1.0, chosen)
        masked = jnp.where(hit, NEG_INF, masked)
    idx = jnp.concatenate(idxs, axis=0)
    w = jnp.concatenate(ws, axis=0)
    w = w / jnp.sum(w, axis=0, keepdims=True) * ROUTED_SCALE

    before = (lax.broadcasted_iota(jnp.int32, (tt, tt), 0) < lax.broadcasted_iota(jnp.int32, (tt, tt), 1))
    within = _dot(chosen.astype(BF16), before.astype(BF16))
    erank = within + carry_ref[...]
    ranks = [jnp.sum(jnp.where(eiota == idxs[k], erank, 0.0), axis=0, keepdims=True) for k in range(TOP_K)]
    carry = carry_ref[...] + jnp.sum(chosen, axis=1, keepdims=True)
    carry_ref[...] = carry

    idx_ref[...] = idx
    w_ref[...] = w
    rank_ref[...] = jnp.concatenate(ranks, axis=0).astype(jnp.int32)
    cnt_ref[...] = carry.astype(jnp.int32)


def _token_stage_kernel(o1_ref, o2_ref, o3_ref, l1_ref, l2_ref, l3_ref, ob_ref, x_ref, wout_ref, e_ref, g1_ref, b1_ref,
                        kv_ref, wq_ref, wo_ref, g2_ref, b2_ref, whi_ref, wlo_ref, bias_ref,
                        h2_ref, slab_ref, idx_ref, w_ref, rank_ref, cnt_ref, h1_ref, carry_ref, nat_ref):
    _mix_out_kernel(o1_ref, o2_ref, o3_ref, l1_ref, l2_ref, l3_ref, ob_ref, x_ref, wout_ref, e_ref, g1_ref, b1_ref, h1_ref,
                    nat_ref)
    _mem_attn_kernel(h1_ref, kv_ref, wq_ref, wo_ref, g2_ref, b2_ref, h2_ref, slab_ref)
    _router_kernel(h2_ref, whi_ref, wlo_ref, bias_ref, idx_ref, w_ref, rank_ref, cnt_ref, carry_ref)


def _token_stage(os_, ls_, ob, x2, w_out, expand, g1, b1, kv, wq, wo, g2, b2, whi, wlo, bias, tm, tiles_per_batch):
    T, D = x2.shape
    M = kv.shape[0] // (T // (tm * tiles_per_batch))
    row = lambda i: (i, 0)
    col = lambda i: (0, i)
    const = lambda i: (0, 0)
    full = lambda a: pl.BlockSpec(a.shape, const)

    def o_spec(o):
        if o.ndim == 2:
            return pl.BlockSpec((tm, A_WIDTH), row)
        d = o.shape[1]
        return pl.BlockSpec((1, d, tm // d, A_WIDTH), lambda i: (i // tiles_per_batch, 0, i % tiles_per_batch, 0))

    return pl.pallas_call(
        _token_stage_kernel,
        grid=(T // tm,),
        in_specs=[o_spec(o) for o in os_] + [pl.BlockSpec((tm, LANES), row)] * 3 + [
            pl.BlockSpec((tm, B_V_WIDTH), row), pl.BlockSpec((tm, D), row), full(w_out), full(expand), full(g1), full(b1),
            pl.BlockSpec((M, 2 * D), lambda i: (i // tiles_per_batch, 0)), full(wq), full(wo), full(g2), full(b2),
            full(whi), full(wlo), full(bias)],
        out_specs=[pl.BlockSpec((tm, D), row), pl.BlockSpec((tm * _slab_rows(D), LANES), row),
                   pl.BlockSpec((TOP_K, tm), col), pl.BlockSpec((TOP_K, tm), col), pl.BlockSpec((TOP_K, tm), col),
                   pl.BlockSpec((N_EXPERTS, 1), const)],
        out_shape=[jax.ShapeDtypeStruct((T, D), F32), jax.ShapeDtypeStruct((T * _slab_rows(D), LANES), F32),
                   jax.ShapeDtypeStruct((TOP_K, T), jnp.int32), jax.ShapeDtypeStruct((TOP_K, T), F32),
                   jax.ShapeDtypeStruct((TOP_K, T), jnp.int32), jax.ShapeDtypeStruct((N_EXPERTS, 1), jnp.int32)],
        scratch_shapes=[pltpu.VMEM((tm, D), F32), pltpu.VMEM((N_EXPERTS, 1), F32),
                        pltpu.VMEM((A_WIDTH // LANES, tm, LANES), F32)],
        compiler_params=_cparams(("arbitrary",)),
        name="token_stage",
    )(*os_, *ls_, ob, x2, w_out, expand, g1, b1, kv, wq, wo, g2, b2, whi, wlo, bias)


def _dest_kernel(idx_ref, rank_ref, start_ref, dest_ref):
    idx = idx_ref[...]
    tt = idx.shape[1]
    eiota = lax.broadcasted_iota(jnp.int32, (N_EXPERTS, tt), 0)
    start = start_ref[...].astype(F32)
    rows = [jnp.sum(jnp.where(eiota == idx[k:k + 1], start, 0.0), axis=0, keepdims=True) for k in range(TOP_K)]
    dest_ref[...] = jnp.concatenate(rows, axis=0).astype(jnp.int32) + rank_ref[...]


def _dest_rows(idx, rank, start, tt):
    T = idx.shape[1]
    col = lambda i: (0, i)
    return pl.pallas_call(
        _dest_kernel,
        grid=(T // tt,),
        in_specs=[pl.BlockSpec((TOP_K, tt), col), pl.BlockSpec((TOP_K, tt), col),
                  pl.BlockSpec((N_EXPERTS, 1), lambda i: (0, 0))],
        out_specs=pl.BlockSpec((TOP_K, tt), col),
        out_shape=jax.ShapeDtypeStruct((TOP_K, T), jnp.int32),
        compiler_params=_cparams(("parallel",)),
        name="dest_rows",
    )(idx, rank, start)


def _dispatch_kernel(dest_ref, x_ref, xs_ref, sem, *, sr):
    tt = x_ref.shape[0] // sr

    def issue(t, c):
        src = x_ref.at[pl.ds(pl.multiple_of(t * sr, sr), sr)]
        for k in range(TOP_K):
            dst = xs_ref.at[pl.ds(pl.multiple_of(dest_ref[k, t] * sr, sr), sr)]
            pltpu.make_async_copy(src, dst, sem).start(priority=k % 2)
        return c

    lax.fori_loop(0, tt, issue, 0)
    n = TOP_K * tt * sr
    pltpu.make_async_copy(xs_ref.at[pl.ds(0, n)], xs_ref.at[pl.ds(0, n)], sem).wait()


def _dispatch(dest, x_slabs, n_rows, tt, sr):
    T = x_slabs.shape[0] // sr
    return pl.pallas_call(
        functools.partial(_dispatch_kernel, sr=sr),
        grid=(T // tt,),
        in_specs=[pl.BlockSpec((TOP_K, tt), lambda i: (0, i), memory_space=pltpu.SMEM),
                  pl.BlockSpec((tt * sr, LANES), lambda i: (i, 0))],
        out_specs=pl.BlockSpec(memory_space=pl.ANY),
        out_shape=jax.ShapeDtypeStruct((n_rows * sr, LANES), F32),
        scratch_shapes=[pltpu.SemaphoreType.DMA],
        compiler_params=_cparams(("arbitrary",)),
        name="dispatch",
    )(dest, x_slabs)


def _experts_kernel(te_ref, tv_ref, nu_ref, x_ref, wg_ref, wu_ref, wd_ref, y_ref, wg_bf, wu_bf, wd_bf):
    i = pl.program_id(0)

    @pl.when(i < nu_ref[0])
    def _():
        @pl.when((i == 0) | (te_ref[i] != te_ref[jnp.maximum(i - 1, 0)]))
        def _():
            wg_bf[...] = wg_ref[0].astype(BF16)
            wu_bf[...] = wu_ref[0].astype(BF16)
            wd_bf[...] = wd_ref[0].astype(BF16)

        d = wg_bf.shape[0]
        tm = x_ref.shape[0] // _slab_rows(d)
        rows = lax.broadcasted_iota(jnp.int32, (tm, 1), 0)
        x = jnp.where(rows < tv_ref[i], _load_slabs(x_ref, tm, d), 0.0).astype(BF16)
        g = _dot(x, wg_bf[...])
        act = (g * jax.nn.sigmoid(g) * _dot(x, wu_bf[...])).astype(BF16)
        _store_slabs(y_ref, _dot(act, wd_bf[...]))


def _experts(tile_e, tile_valid, n_used, xs, w_gate, w_up, w_down, tm):
    E, D, F = w_gate.shape
    sr = _slab_rows(D)
    n_tiles = xs.shape[0] // (tm * sr)

    def tile(i, te, tv, nu):
        return (jnp.minimum(i, nu[0] - 1), 0)

    def expert(i, te, tv, nu):
        return (te[jnp.minimum(i, nu[0] - 1)], 0, 0)

    return pl.pallas_call(
        _experts_kernel,
        grid_spec=pltpu.PrefetchScalarGridSpec(
            num_scalar_prefetch=3,
            grid=(n_tiles,),
            in_specs=[pl.BlockSpec((tm * sr, LANES), tile), pl.BlockSpec((1, D, F), expert),
                      pl.BlockSpec((1, D, F), expert), pl.BlockSpec((1, F, D), expert)],
            out_specs=pl.BlockSpec((tm * sr, LANES), tile),
            scratch_shapes=[pltpu.VMEM((D, F), BF16), pltpu.VMEM((D, F), BF16), pltpu.VMEM((F, D), BF16)],
        ),
        out_shape=jax.ShapeDtypeStruct(xs.shape, F32),
        compiler_params=_cparams(("arbitrary",)),
        name="experts",
    )(tile_e, tile_valid, n_used, xs, w_gate, w_up, w_down)


def _combine_kernel(dest_ref, nxt_ref, w_ref, h_ref, ys_ref, sg_ref, su_ref, sd_ref, g_ref, b_ref, out_ref, buf, sem):
    i = pl.program_id(0)
    last = pl.num_programs(0) - 1
    tt, d = h_ref.shape
    sr = _slab_rows(d)
    slot = i % 2

    def gather(idx_ref, to):
        def issue(t, c):
            for k in range(TOP_K):
                src = ys_ref.at[pl.ds(pl.multiple_of(idx_ref[k, t] * sr, sr), sr)]
                dst = buf.at[to, k, pl.ds(pl.multiple_of(t * sr, sr), sr)]
                pltpu.make_async_copy(src, dst, sem.at[to]).start(priority=k % 2)
            return c

        lax.fori_loop(0, tt, issue, 0)

    @pl.when(i == 0)
    def _():
        gather(dest_ref, 0)

    @pl.when(i < last)
    def _():
        gather(nxt_ref, 1 - slot)

    h = h_ref[...]
    hb = h.astype(BF16)
    gate = _dot(hb, sg_ref[...])
    act = (gate * jax.nn.sigmoid(gate) * _dot(hb, su_ref[...])).astype(BF16)
    shared = _dot(act, sd_ref[...])

    for k in range(TOP_K):
        pltpu.make_async_copy(ys_ref.at[pl.ds(0, tt * sr)], buf.at[slot, k], sem.at[slot]).wait()

    w = w_ref[...]
    wk = [jnp.broadcast_to(w[:, k:k + 1], (tt, LANES)) for k in range(TOP_K)]
    chunks = []
    for c in range(sr):
        acc = buf[slot, 0, pl.ds(c, tt, stride=sr), :] * wk[0]
        for k in range(1, TOP_K):
            acc = acc + buf[slot, k, pl.ds(c, tt, stride=sr), :] * wk[k]
        chunks.append(acc)
    routed = jnp.concatenate(chunks, axis=1)
    out_ref[...] = _layer_norm(DN_ALPHA * h + (routed + shared), g_ref[...], b_ref[...])


def _combine(dest, w_tok, h2, ys, sg, su, sd, g, b, tt):
    T, D = h2.shape
    sr = _slab_rows(D)
    row = lambda i: (i, 0)
    const = lambda i: (0, 0)
    n = T // tt
    return pl.pallas_call(
        _combine_kernel,
        grid=(n,),
        in_specs=[pl.BlockSpec((TOP_K, tt), lambda i: (0, i), memory_space=pltpu.SMEM),
                  pl.BlockSpec((TOP_K, tt), lambda i: (0, jnp.minimum(i + 1, n - 1)), memory_space=pltpu.SMEM),
                  pl.BlockSpec((tt, TOP_K), row), pl.BlockSpec((tt, D), row),
                  pl.BlockSpec(memory_space=pl.ANY),
                  pl.BlockSpec(sg.shape, const), pl.BlockSpec(su.shape, const), pl.BlockSpec(sd.shape, const),
                  pl.BlockSpec((1, D), const), pl.BlockSpec((1, D), const)],
        out_specs=pl.BlockSpec((tt, D), row),
        out_shape=jax.ShapeDtypeStruct((T, D), F32),
        scratch_shapes=[pltpu.VMEM((2, TOP_K, tt * sr, LANES), F32), pltpu.SemaphoreType.DMA((2,))],
        compiler_params=_cparams(("arbitrary",)),
        name="combine",
    )(dest, dest, w_tok, h2, ys, sg, su, sd, g, b)


def _rope_lane_tables(positions):
    half = ROT_DIM // 2
    inv = ROPE_THETA ** (-jnp.arange(0, ROT_DIM, 2, dtype=F32) / ROT_DIM)
    ang = positions.astype(F32).reshape(-1, 1) * inv
    cos, sin = jnp.cos(ang), jnp.sin(ang)
    T = cos.shape[0]
    rest = A_HEAD_DIM - ROT_DIM
    one_head = lambda first, second, fill: jnp.concatenate(
        [first, second, jnp.full((T, rest), fill, F32)], axis=1)
    zeros = jnp.zeros_like(sin)
    cc = one_head(cos, cos, 1.0)
    s1 = one_head(-sin, zeros, 0.0)
    s2 = one_head(zeros, sin, 0.0)
    rep = LANES // A_HEAD_DIM
    return jnp.tile(cc, (1, rep)), jnp.tile(s1, (1, rep)), jnp.tile(s2, (1, rep))


def _layer(x, mem, cc, s1, s2, l, w_in, w_out, lq1, lk1, lq2, lk2, subln_g, ln1_g, ln1_b,
           wq_mem, wkv_mem, wo_mem, ln2_g, ln2_b, w_router, e_bias, w_gate, w_up, w_down,
           ws_gate, ws_up, ws_down, ln3_g, ln3_b):
    B, S, D = x.shape
    T = B * S
    lambda_init = 0.8 - 0.6 * math.exp(-0.3 * l)
    x2 = x.reshape(T, D)
    tm = min(512, S)

    w_bf = w_in.astype(BF16)
    o_va, o_qb, o_vb = 2 * A_WIDTH, 3 * A_WIDTH, 3 * A_WIDTH + 2 * B_QK_WIDTH
    w_qk = jnp.concatenate([w_bf[:, :o_va], w_bf[:, o_qb:o_vb]], axis=1)
    w_vt = jnp.concatenate([w_bf[:, o_va:o_qb], w_bf[:, o_vb:]], axis=1).T
    dils = tuple(d for _, d in A_PAIRS if d > 1)
    qa, ka, qb, kb, vat, vbt, *strided = _in_proj(x2, w_qk, w_vt, cc, s1, s2, tm, S, dils)
    strided_qk = {d: (strided[2 * n], strided[2 * n + 1]) for n, d in enumerate(dils)}

    outs, lses = [], []
    for window, d in A_PAIRS:
        assert window // d == BAND
        L = S // d
        q3, k3 = (qa, ka) if d == 1 else strided_qk[d]
        q3, k3 = q3.reshape(B * d, L, A_WIDTH), k3.reshape(B * d, L, A_WIDTH)
        vt3 = vat.reshape(B, A_WIDTH, L, d).transpose(0, 3, 1, 2).reshape(B * d, A_WIDTH, L)
        o, lse = _dilated(q3, k3, vt3, min(2 * BAND, L))
        outs.append(o.reshape(T, A_WIDTH) if d == 1 else o.reshape(B, d, L, A_WIDTH))
        lse = lse.reshape(B, d, A_HEADS, L).transpose(0, 3, 1, 2).reshape(T, A_HEADS)
        lses.append(jnp.pad(lse, ((0, 0), (0, LANES - A_HEADS))))

    row = lambda v: v.reshape(1, -1).astype(F32)
    ob = _diff_attention(qb.reshape(B, S, -1), kb.reshape(B, S, -1), vbt, row(lq1), row(lk1), row(lq2), row(lk2),
                         subln_g.reshape(-1, 1).astype(F32), lambda_init, min(1024, S), min(512, S))

    expand = (jnp.arange(LANES)[:, None] == (jnp.arange(A_WIDTH)[None, :] // A_HEAD_DIM)).astype(BF16)
    M = mem.shape[1]
    kv = _mem_kv(mem.reshape(B * M, D), wkv_mem.astype(BF16), min(512, B * M))
    wr_t = w_router.T
    wr_hi = wr_t.astype(BF16)
    wr_lo = (wr_t - wr_hi.astype(F32)).astype(BF16)
    h2, h2_slabs, idx, w_route, rank, counts = _token_stage(
        outs, lses, ob.reshape(T, -1), x2, w_out.astype(BF16), expand, row(ln1_g), row(ln1_b),
        kv, wq_mem.astype(BF16), wo_mem.astype(BF16), row(ln2_g), row(ln2_b),
        wr_hi, wr_lo, e_bias.reshape(-1, 1).astype(F32), tm, S // tm)

    tile_rows = 512
    counts = counts.reshape(-1)
    padded = (counts + tile_rows - 1) // tile_rows * tile_rows
    pend = jnp.cumsum(padded)
    pstart = pend - padded
    n_tiles = (T * TOP_K) // tile_rows + N_EXPERTS
    n_rows = n_tiles * tile_rows
    tile_lo = jnp.arange(n_tiles, dtype=jnp.int32) * tile_rows
    tile_e = jnp.minimum(jnp.sum(pend[None, :] <= tile_lo[:, None], axis=1), N_EXPERTS - 1).astype(jnp.int32)
    tile_valid = jnp.clip(counts[tile_e] - (tile_lo - pstart[tile_e]), 0, tile_rows).astype(jnp.int32)
    n_used = (pend[-1] // tile_rows).astype(jnp.int32).reshape(1)

    dest = _dest_rows(idx, rank, pstart.reshape(-1, 1).astype(jnp.int32), min(2048, T))
    xs = _dispatch(dest, h2_slabs, n_rows, min(512, S), _slab_rows(D))
    ys = _experts(tile_e, tile_valid, n_used, xs, w_gate, w_up, w_down, tile_rows)
    out = _combine(dest, w_route.T, h2, ys, ws_gate.astype(BF16), ws_up.astype(BF16), ws_down.astype(BF16),
                   row(ln3_g), row(ln3_b), min(512, S))
    return out.reshape(B, S, D)


def kernel(x, mem, positions, w_in, w_out, lambda_q1, lambda_k1, lambda_q2, lambda_k2, subln_g, ln1_g, ln1_b,
           wq_mem, wkv_mem, wo_mem, ln2_g, ln2_b, w_router, e_bias, w_gate, w_up, w_down, ws_gate, ws_up, ws_down,
           ln3_g, ln3_b):
    cc, s1, s2 = _rope_lane_tables(positions)
    h = x
    for l in range(w_in.shape[0]):
        h = _layer(h, mem, cc, s1, s2, l, w_in[l], w_out[l], lambda_q1[l], lambda_k1[l], lambda_q2[l],
                   lambda_k2[l], subln_g[l], ln1_g[l], ln1_b[l], wq_mem[l], wkv_mem[l], wo_mem[l],
                   ln2_g[l], ln2_b[l], w_router[l], e_bias[l], w_gate[l], w_up[l], w_down[l],
                   ws_gate[l], ws_up[l], ws_down[l], ln3_g[l], ln3_b[l])
    return h
```

```python
import functools
import math

import jax
import jax.numpy as jnp
from jax import lax
from jax.experimental import pallas as pl
from jax.experimental.pallas import tpu as pltpu

A_HEADS = 8
A_HEAD_DIM = 64
A_PAIRS = ((128, 1), (512, 4), (2048, 16))
B_HEADS = 4
B_QK_DIM = 64
B_V_DIM = 2 * B_QK_DIM
A_WIDTH = A_HEADS * A_HEAD_DIM
B_QK_WIDTH = B_HEADS * 2 * B_QK_DIM
B_V_WIDTH = B_HEADS * B_V_DIM
ROPE_THETA = 500000.0
ROT_DIM = A_HEAD_DIM // 4
M_HEADS = 4
N_EXPERTS = 256
TOP_K = 8
N_GROUPS = 8
TOPK_GROUPS = 4
ROUTED_SCALE = 2.5
LN_EPS = 1e-5
SUBLN_EPS = 1e-5
DEPTH = 1
DN_ALPHA = (2 * DEPTH) ** 0.25

LANES = 128
BAND = 128
ONES_ROWS = 16
COMBINE_GROUP = 32
VMEM_LIMIT = 56 * 1024 * 1024

BF16 = jnp.bfloat16
F32 = jnp.float32
NEG_INF = float("-inf")


def _cparams(sem):
    return pltpu.CompilerParams(dimension_semantics=sem, vmem_limit_bytes=VMEM_LIMIT)


def _dot(a, b):
    return jnp.dot(a, b, preferred_element_type=F32)


def _dot_nt(a, b):
    return lax.dot_general(a, b, (((1,), (1,)), ((), ())), preferred_element_type=F32)


def _layer_norm(x, g, b):
    mu = jnp.mean(x, axis=-1, keepdims=True)
    xc = x - mu
    var = jnp.mean(xc * xc, axis=-1, keepdims=True)
    return xc * lax.rsqrt(var + LN_EPS) * g + b


def _slab_rows(d):
    return d // LANES


def _store_slabs(ref, x):
    n, d = x.shape
    s = _slab_rows(d)
    for c in range(s):
        ref[pl.ds(c, n, stride=s), :] = x[:, c * LANES:(c + 1) * LANES]


def _load_slabs(ref, n, d):
    s = _slab_rows(d)
    return jnp.concatenate([ref[pl.ds(c, n, stride=s), :] for c in range(s)], axis=1)


def _to_strided(nat_ref, t, sub_refs, dils):
    tm, w = t.shape
    chunks = range(w // LANES)
    for c in chunks:
        nat_ref[c] = t[:, c * LANES:(c + 1) * LANES]
    for d, ref in zip(dils, sub_refs):
        for r in range(d):
            rows = [nat_ref[c, pl.ds(r, tm // d, stride=d), :] for c in chunks]
            ref[0, r] = jnp.concatenate(rows, axis=1).astype(ref.dtype)


def _from_strided(nat_ref, sub_ref, d):
    n, w = sub_ref.shape[2], sub_ref.shape[3]
    chunks = range(w // LANES)
    for r in range(d):
        for c in chunks:
            nat_ref[c, pl.ds(r, n, stride=d), :] = sub_ref[0, r, :, c * LANES:(c + 1) * LANES].astype(F32)
    return jnp.concatenate([nat_ref[c] for c in chunks], axis=1)


def _in_proj_kernel(x_ref, w_ref, wvt_ref, c_ref, s1_ref, s2_ref, qa_ref, ka_ref, qb_ref, kb_ref, vat_ref, vbt_ref,
                    *rest, dils):
    sub_q, sub_k, nat_ref = rest[0:-1:2], rest[1:-1:2], rest[-1]
    x = x_ref[...].astype(BF16)
    cc, s1, s2 = c_ref[...], s1_ref[...], s2_ref[...]

    def rope(t):
        outs = []
        for c in range(t.shape[1] // LANES):
            tc = t[:, c * LANES:(c + 1) * LANES]
            nxt = pltpu.roll(tc, LANES - ROT_DIM // 2, 1)
            prv = pltpu.roll(tc, ROT_DIM // 2, 1)
            outs.append(tc * cc + nxt * s1 + prv * s2)
        return jnp.concatenate(outs, axis=1)

    def proj(lo, width):
        return _dot(x, w_ref[:, lo:lo + width])

    o = 0
    qa = rope(proj(o, A_WIDTH)) * (A_HEAD_DIM ** -0.5); o += A_WIDTH
    qa_ref[...] = qa.astype(BF16)
    _to_strided(nat_ref, qa, sub_q, dils)
    ka = rope(proj(o, A_WIDTH)); o += A_WIDTH
    ka_ref[...] = ka.astype(BF16)
    _to_strided(nat_ref, ka, sub_k, dils)
    qb_ref[...] = (rope(proj(o, B_QK_WIDTH)) * (B_QK_DIM ** -0.5)).astype(BF16); o += B_QK_WIDTH
    kb_ref[...] = rope(proj(o, B_QK_WIDTH)).astype(BF16)
    vat_ref[0] = _dot_nt(wvt_ref[:A_WIDTH, :], x).astype(BF16)
    vbt_ref[0] = _dot_nt(wvt_ref[A_WIDTH:, :], x).astype(BF16)


def _in_proj(x2, w_qk, w_vt, cc, s1, s2, tm, seq, dils):
    T, D = x2.shape
    nt = seq // tm
    nb = T // seq
    row = lambda i: (i, 0)
    const = lambda i: (0, 0)
    tcol = lambda i: (i // nt, 0, i % nt)
    outs = [jax.ShapeDtypeStruct((T, A_WIDTH), BF16)] * 2 + [jax.ShapeDtypeStruct((T, B_QK_WIDTH), BF16)] * 2
    outs_t = [jax.ShapeDtypeStruct((nb, A_WIDTH, seq), BF16), jax.ShapeDtypeStruct((nb, B_V_WIDTH, seq), BF16)]
    outs_s = [jax.ShapeDtypeStruct((nb, d, seq // d, A_WIDTH), BF16) for d in dils for _ in range(2)]
    return pl.pallas_call(
        functools.partial(_in_proj_kernel, dils=dils),
        grid=(T // tm,),
        in_specs=[pl.BlockSpec((tm, D), row), pl.BlockSpec(w_qk.shape, const), pl.BlockSpec(w_vt.shape, const),
                  pl.BlockSpec((tm, LANES), row), pl.BlockSpec((tm, LANES), row), pl.BlockSpec((tm, LANES), row)],
        out_specs=[pl.BlockSpec((tm, o.shape[1]), row) for o in outs] + [
            pl.BlockSpec((1, o.shape[1], tm), tcol) for o in outs_t] + [
            pl.BlockSpec((1, o.shape[1], tm // o.shape[1], A_WIDTH), lambda i: (i // nt, 0, i % nt, 0)) for o in outs_s],
        out_shape=outs + outs_t + outs_s,
        scratch_shapes=[pltpu.VMEM((A_WIDTH // LANES, tm, LANES), F32)],
        compiler_params=_cparams(("parallel",)),
        name="in_proj",
    )(x2, w_qk, w_vt, cc, s1, s2)


def _dilated_kernel(q_ref, kp_ref, kc_ref, vtp_ref, vtc_ref, o_ref, lse_ref):
    n = pl.program_id(1)
    tq = q_ref.shape[1]
    nk = BAND + tq
    two = LANES // A_HEAD_DIM
    kr = lax.broadcasted_iota(jnp.int32, (nk, two * tq), 0)
    qi = lax.broadcasted_iota(jnp.int32, (nk, two * tq), 1) % tq
    ok = (kr >= jnp.where(n > 0, qi, jnp.maximum(qi, BAND))) & (kr <= qi + BAND)
    lane = lax.broadcasted_iota(jnp.int32, (tq, LANES), 1)
    row = lax.broadcasted_iota(jnp.int32, (LANES, tq), 0)
    chunks = [slice(c * LANES, (c + 1) * LANES) for c in range(A_WIDTH // LANES)]
    scores = []
    for sl in chunks:
        q = q_ref[0, :, sl]
        zero = jnp.zeros_like(q)
        q2 = jnp.concatenate([jnp.where((lane >= h * A_HEAD_DIM) & (lane < (h + 1) * A_HEAD_DIM), q, zero)
                              for h in range(two)], axis=0)
        kband = jnp.concatenate([kp_ref[0, :, sl], kc_ref[0, :, sl]], axis=0)
        scores.append(_dot_nt(kband, q2))
    probs, lses = [], []
    for st in scores:
        s = jnp.where(ok, st, NEG_INF)
        m = jnp.max(s, axis=0, keepdims=True)
        p = jnp.exp(s - m)
        den = jnp.sum(p, axis=0, keepdims=True)
        probs.append((p.astype(BF16), den))
        lse = m + jnp.log(den)
        lses += [lse[:, h * tq:(h + 1) * tq] for h in range(two)]
    for sl, (p, den) in zip(chunks, probs):
        vt = jnp.concatenate([vtp_ref[0, sl, :], vtc_ref[0, sl, :]], axis=1)
        ot = _dot(vt, p) / den
        out_t = ot[:, :tq]
        for h in range(1, two):
            out_t = jnp.where(row >= h * A_HEAD_DIM, ot[:, h * tq:(h + 1) * tq], out_t)
        o_ref[0, :, sl] = out_t.T.astype(BF16)
    lse_ref[0] = jnp.concatenate(lses, axis=0)


def _dilated(q, k, vt, tq):
    G, L, W = q.shape
    per = tq // BAND
    cur = lambda g, n: (g, n, 0)
    prev = lambda g, n: (g, jnp.maximum(n * per - 1, 0), 0)
    cur_t = lambda g, n: (g, 0, n)
    prev_t = lambda g, n: (g, 0, jnp.maximum(n * per - 1, 0))
    return pl.pallas_call(
        _dilated_kernel,
        grid=(G, L // tq),
        in_specs=[pl.BlockSpec((1, tq, W), cur), pl.BlockSpec((1, BAND, W), prev), pl.BlockSpec((1, tq, W), cur),
                  pl.BlockSpec((1, W, BAND), prev_t), pl.BlockSpec((1, W, tq), cur_t)],
        out_specs=[pl.BlockSpec((1, tq, W), cur), pl.BlockSpec((1, A_HEADS, tq), cur_t)],
        out_shape=[jax.ShapeDtypeStruct((G, L, W), BF16), jax.ShapeDtypeStruct((G, A_HEADS, L), F32)],
        compiler_params=_cparams(("parallel", "parallel")),
        name="dilated",
    )(q, k, k, vt, vt)


def _diff_kernel(lq1_ref, lk1_ref, lq2_ref, lk2_ref, g_ref, q_ref, k_ref, vt_ref, o_ref, acc_ref, *, tq, tk, lambda_init):
    iq = pl.program_id(2)
    q = q_ref[0]
    lane = lax.broadcasted_iota(jnp.int32, q.shape, 1)
    zero = jnp.zeros_like(q)
    q2 = jnp.concatenate([jnp.where(lane < B_QK_DIM, q, zero), jnp.where(lane >= B_QK_DIM, q, zero)], axis=0)
    acc_ref[...] = jnp.zeros_like(acc_ref)

    def block(j, carry, diag):
        start = pl.multiple_of(j * tk, tk)
        kb = k_ref[0, pl.ds(start, tk), :]
        vt = jnp.concatenate([vt_ref[0, :, pl.ds(start, tk)], jnp.ones((ONES_ROWS, tk), BF16)], axis=0)
        st = _dot_nt(kb, q2)
        if diag is not None:
            ki = lax.broadcasted_iota(jnp.int32, (tk, tq), 0) + diag
            qi = lax.broadcasted_iota(jnp.int32, (tk, tq), 1)
            ok = ki <= qi
        out, work = [], []
        for c in range(2):
            m = carry[c]
            s = st[:, c * tq:(c + 1) * tq]
            if diag is not None:
                s = jnp.where(ok, s, NEG_INF)
            m_new = jnp.maximum(m, jnp.max(s, axis=0, keepdims=True))
            out.append(m_new)
            work.append((jnp.exp(m - m_new), jnp.exp(s - m_new).astype(BF16)))
        for c, (a, p) in enumerate(work):
            acc_ref[c] = a * acc_ref[c] + _dot(vt, p)
        return tuple(out)

    init = tuple(jnp.full((1, tq), NEG_INF, F32) for _ in range(2))
    per_tile = tq // tk
    carry = lax.fori_loop(0, iq * per_tile, lambda j, cr: block(j, cr, None), init)
    for r in range(per_tile):
        carry = block(iq * per_tile + r, carry, r * tk)

    lam = (jnp.exp(jnp.sum(lq1_ref[...] * lk1_ref[...], axis=1, keepdims=True))
           - jnp.exp(jnp.sum(lq2_ref[...] * lk2_ref[...], axis=1, keepdims=True)) + lambda_init)
    nv = B_V_DIM
    o = (acc_ref[0, :nv, :] / acc_ref[0, nv:nv + 1, :]
         - lam * (acc_ref[1, :nv, :] / acc_ref[1, nv:nv + 1, :]))
    o = o * lax.rsqrt(jnp.mean(o * o, axis=0, keepdims=True) + SUBLN_EPS) * g_ref[...]
    o_ref[0] = (o * (1.0 - lambda_init)).T.astype(BF16)


def _diff_attention(qb, kb, vbt, lq1, lk1, lq2, lk2, subln_g, lambda_init, tq, tk):
    B, S, _ = qb.shape
    vec = lambda n: pl.BlockSpec((1, n), lambda b, h, i: (0, 0))
    return pl.pallas_call(
        functools.partial(_diff_kernel, tq=tq, tk=tk, lambda_init=lambda_init),
        grid=(B, B_HEADS, S // tq),
        in_specs=[vec(B_QK_DIM)] * 4 + [pl.BlockSpec((B_V_DIM, 1), lambda b, h, i: (0, 0)),
                  pl.BlockSpec((1, tq, B_V_DIM), lambda b, h, i: (b, i, h)),
                  pl.BlockSpec((1, S, B_V_DIM), lambda b, h, i: (b, 0, h)),
                  pl.BlockSpec((1, B_V_DIM, S), lambda b, h, i: (b, h, 0))],
        out_specs=pl.BlockSpec((1, tq, B_V_DIM), lambda b, h, i: (b, i, h)),
        out_shape=jax.ShapeDtypeStruct((B, S, B_V_WIDTH), BF16),
        scratch_shapes=[pltpu.VMEM((2, B_V_DIM + ONES_ROWS, tq), F32)],
        compiler_params=_cparams(("parallel", "parallel", "arbitrary")),
        name="diff_attn",
    )(lq1, lk1, lq2, lk2, subln_g, qb, kb, vbt)


def _mix_out_kernel(o1_ref, o2_ref, o3_ref, l1_ref, l2_ref, l3_ref, ob_ref, x_ref, w_ref, e_ref, g_ref, b_ref, h_ref,
                    nat_ref):
    ls = (l1_ref[...], l2_ref[...], l3_ref[...])
    mx = jnp.maximum(jnp.maximum(ls[0], ls[1]), ls[2])
    es = [jnp.exp(l - mx) for l in ls]
    den = es[0] + es[1] + es[2]
    e = e_ref[...]
    oa = None
    for ex, o_ref, (_, d) in zip(es, (o1_ref, o2_ref, o3_ref), A_PAIRS):
        wt = ex / den
        hi = wt.astype(BF16)
        lo = (wt - hi.astype(F32)).astype(BF16)
        wexp = _dot(hi, e) + _dot(lo, e)
        o = o_ref[...].astype(F32) if d == 1 else _from_strided(nat_ref, o_ref, d)
        term = wexp * o
        oa = term if oa is None else oa + term
    y = _dot(oa.astype(BF16), w_ref[:A_WIDTH, :]) + _dot(ob_ref[...], w_ref[A_WIDTH:, :])
    h_ref[...] = _layer_norm(DN_ALPHA * x_ref[...] + y, g_ref[...], b_ref[...])


def _matmul_kernel(a_ref, w_ref, o_ref):
    o_ref[...] = _dot(a_ref[...].astype(BF16), w_ref[...]).astype(o_ref.dtype)


def _mem_kv(mem2, wkv, tm):
    R, D = mem2.shape
    N = wkv.shape[1]
    return pl.pallas_call(
        _matmul_kernel,
        grid=(R // tm,),
        in_specs=[pl.BlockSpec((tm, D), lambda i: (i, 0)), pl.BlockSpec((D, N), lambda i: (0, 0))],
        out_specs=pl.BlockSpec((tm, N), lambda i: (i, 0)),
        out_shape=jax.ShapeDtypeStruct((R, N), BF16),
        compiler_params=_cparams(("parallel",)),
        name="mem_kv",
    )(mem2, wkv)


def _mem_attn_kernel(h_ref, kv_ref, wq_ref, wo_ref, g_ref, b_ref, out_ref, slab_ref):
    h = h_ref[...]
    D = h.shape[1]
    hd = D // M_HEADS
    q = (_dot(h.astype(BF16), wq_ref[...]) * (hd ** -0.5)).astype(BF16)
    heads = []
    for i in range(M_HEADS):
        k = kv_ref[:, i * hd:(i + 1) * hd]
        v = kv_ref[:, D + i * hd:D + (i + 1) * hd]
        s = _dot_nt(q[:, i * hd:(i + 1) * hd], k)
        p = jnp.exp(s - jnp.max(s, axis=1, keepdims=True))
        p = p / jnp.sum(p, axis=1, keepdims=True)
        heads.append(_dot(p.astype(BF16), v).astype(BF16))
    o = jnp.concatenate(heads, axis=1)
    h2 = _layer_norm(DN_ALPHA * h + _dot(o, wo_ref[...]), g_ref[...], b_ref[...])
    out_ref[...] = h2
    _store_slabs(slab_ref, h2)


def _argmax_rows(v, row_iota, n_rows):
    m = jnp.max(v, axis=0, keepdims=True)
    idx = jnp.min(jnp.where(v == m, row_iota, n_rows), axis=0, keepdims=True)
    return m, idx


def _router_kernel(h_ref, whi_ref, wlo_ref, bias_ref, idx_ref, w_ref, rank_ref, cnt_ref, carry_ref):
    @pl.when(pl.program_id(0) == 0)
    def _():
        carry_ref[...] = jnp.zeros_like(carry_ref)

    h = h_ref[...]
    tt = h.shape[0]
    xhi = h.astype(BF16)
    xlo = (h - xhi.astype(F32)).astype(BF16)
    whi, wlo = whi_ref[...], wlo_ref[...]
    logits = _dot_nt(whi, xhi) + (_dot_nt(wlo, xhi) + _dot_nt(whi, xlo))
    scores = jax.nn.sigmoid(logits)
    biased = scores + bias_ref[...]

    gsz = N_EXPERTS // N_GROUPS
    giota = lax.broadcasted_iota(jnp.int32, (gsz, tt), 0)
    gscore = []
    for g in range(N_GROUPS):
        blk = biased[g * gsz:(g + 1) * gsz]
        m1, i1 = _argmax_rows(blk, giota, gsz)
        m2 = jnp.max(jnp.where(giota == i1, NEG_INF, blk), axis=0, keepdims=True)
        gscore.append(m1 + m2)
    gscore = jnp.concatenate(gscore, axis=0)
    riota = lax.broadcasted_iota(jnp.int32, (N_GROUPS, tt), 0)
    gsel = jnp.zeros((N_GROUPS, tt), F32)
    for _ in range(TOPK_GROUPS):
        _, gi = _argmax_rows(gscore, riota, N_GROUPS)
        hit = riota == gi
        gsel = jnp.where(hit, 1.0, gsel)
        gscore = jnp.where(hit, NEG_INF, gscore)
    masked = jnp.concatenate(
        [jnp.where(gsel[g:g + 1] > 0.5, biased[g * gsz:(g + 1) * gsz], NEG_INF) for g in range(N_GROUPS)], axis=0)

    eiota = lax.broadcasted_iota(jnp.int32, (N_EXPERTS, tt), 0)
    chosen = jnp.zeros((N_EXPERTS, tt), F32)
    idxs, ws = [], []
    for _ in range(TOP_K):
        _, ei = _argmax_rows(masked, eiota, N_EXPERTS)
        hit = eiota == ei
        idxs.append(ei)
        ws.append(jnp.sum(jnp.where(hit, scores, 0.0), axis=0, keepdims=True))
        chosen = jnp.where(hit, 1.0, chosen)
        masked = jnp.where(hit, NEG_INF, masked)
    idx = jnp.concatenate(idxs, axis=0)
    w = jnp.concatenate(ws, axis=0)
    w = w / jnp.sum(w, axis=0, keepdims=True) * ROUTED_SCALE

    before = (lax.broadcasted_iota(jnp.int32, (tt, tt), 0) < lax.broadcasted_iota(jnp.int32, (tt, tt), 1))
    within = _dot(chosen.astype(BF16), before.astype(BF16))
    erank = within + carry_ref[...]
    ranks = [jnp.sum(jnp.where(eiota == idxs[k], erank, 0.0), axis=0, keepdims=True) for k in range(TOP_K)]
    carry = carry_ref[...] + jnp.sum(chosen, axis=1, keepdims=True)
    carry_ref[...] = carry

    idx_ref[...] = idx
    w_ref[...] = w
    rank_ref[...] = jnp.concatenate(ranks, axis=0).astype(jnp.int32)
    cnt_ref[...] = carry.astype(jnp.int32)


def _token_stage_kernel(o1_ref, o2_ref, o3_ref, l1_ref, l2_ref, l3_ref, ob_ref, x_ref, wout_ref, e_ref, g1_ref, b1_ref,
                        kv_ref, wq_ref, wo_ref, g2_ref, b2_ref, whi_ref, wlo_ref, bias_ref,
                        h2_ref, slab_ref, idx_ref, w_ref, rank_ref, cnt_ref, h1_ref, carry_ref, nat_ref):
    _mix_out_kernel(o1_ref, o2_ref, o3_ref, l1_ref, l2_ref, l3_ref, ob_ref, x_ref, wout_ref, e_ref, g1_ref, b1_ref, h1_ref,
                    nat_ref)
    _mem_attn_kernel(h1_ref, kv_ref, wq_ref, wo_ref, g2_ref, b2_ref, h2_ref, slab_ref)
    _router_kernel(h2_ref, whi_ref, wlo_ref, bias_ref, idx_ref, w_ref, rank_ref, cnt_ref, carry_ref)


def _token_stage(os_, ls_, ob, x2, w_out, expand, g1, b1, kv, wq, wo, g2, b2, whi, wlo, bias, tm, tiles_per_batch):
    T, D = x2.shape
    M = kv.shape[0] // (T // (tm * tiles_per_batch))
    row = lambda i: (i, 0)
    col = lambda i: (0, i)
    const = lambda i: (0, 0)
    full = lambda a: pl.BlockSpec(a.shape, const)

    def o_spec(o):
        if o.ndim == 2:
            return pl.BlockSpec((tm, A_WIDTH), row)
        d = o.shape[1]
        return pl.BlockSpec((1, d, tm // d, A_WIDTH), lambda i: (i // tiles_per_batch, 0, i % tiles_per_batch, 0))

    return pl.pallas_call(
        _token_stage_kernel,
        grid=(T // tm,),
        in_specs=[o_spec(o) for o in os_] + [pl.BlockSpec((tm, LANES), row)] * 3 + [
            pl.BlockSpec((tm, B_V_WIDTH), row), pl.BlockSpec((tm, D), row), full(w_out), full(expand), full(g1), full(b1),
            pl.BlockSpec((M, 2 * D), lambda i: (i // tiles_per_batch, 0)), full(wq), full(wo), full(g2), full(b2),
            full(whi), full(wlo), full(bias)],
        out_specs=[pl.BlockSpec((tm, D), row), pl.BlockSpec((tm * _slab_rows(D), LANES), row),
                   pl.BlockSpec((TOP_K, tm), col), pl.BlockSpec((TOP_K, tm), col), pl.BlockSpec((TOP_K, tm), col),
                   pl.BlockSpec((N_EXPERTS, 1), const)],
        out_shape=[jax.ShapeDtypeStruct((T, D), F32), jax.ShapeDtypeStruct((T * _slab_rows(D), LANES), F32),
                   jax.ShapeDtypeStruct((TOP_K, T), jnp.int32), jax.ShapeDtypeStruct((TOP_K, T), F32),
                   jax.ShapeDtypeStruct((TOP_K, T), jnp.int32), jax.ShapeDtypeStruct((N_EXPERTS, 1), jnp.int32)],
        scratch_shapes=[pltpu.VMEM((tm, D), F32), pltpu.VMEM((N_EXPERTS, 1), F32),
                        pltpu.VMEM((A_WIDTH // LANES, tm, LANES), F32)],
        compiler_params=_cparams(("arbitrary",)),
        name="token_stage",
    )(*os_, *ls_, ob, x2, w_out, expand, g1, b1, kv, wq, wo, g2, b2, whi, wlo, bias)


def _dest_kernel(idx_ref, rank_ref, start_ref, dest_ref):
    idx = idx_ref[...]
    tt = idx.shape[1]
    eiota = lax.broadcasted_iota(jnp.int32, (N_EXPERTS, tt), 0)
    start = start_ref[...].astype(F32)
    rows = [jnp.sum(jnp.where(eiota == idx[k:k + 1], start, 0.0), axis=0, keepdims=True) for k in range(TOP_K)]
    dest_ref[...] = jnp.concatenate(rows, axis=0).astype(jnp.int32) + rank_ref[...]


def _dest_rows(idx, rank, start, tt):
    T = idx.shape[1]
    col = lambda i: (0, i)
    return pl.pallas_call(
        _dest_kernel,
        grid=(T // tt,),
        in_specs=[pl.BlockSpec((TOP_K, tt), col), pl.BlockSpec((TOP_K, tt), col),
                  pl.BlockSpec((N_EXPERTS, 1), lambda i: (0, 0))],
        out_specs=pl.BlockSpec((TOP_K, tt), col),
        out_shape=jax.ShapeDtypeStruct((TOP_K, T), jnp.int32),
        compiler_params=_cparams(("parallel",)),
        name="dest_rows",
    )(idx, rank, start)


def _dispatch_kernel(dest_ref, x_ref, xs_ref, sem, *, sr):
    tt = x_ref.shape[0] // sr

    def issue(t, c):
        src = x_ref.at[pl.ds(pl.multiple_of(t * sr, sr), sr)]
        for k in range(TOP_K):
            dst = xs_ref.at[pl.ds(pl.multiple_of(dest_ref[k, t] * sr, sr), sr)]
            pltpu.make_async_copy(src, dst, sem).start(priority=k % 2)
        return c

    lax.fori_loop(0, tt, issue, 0)
    n = TOP_K * tt * sr
    pltpu.make_async_copy(xs_ref.at[pl.ds(0, n)], xs_ref.at[pl.ds(0, n)], sem).wait()


def _dispatch(dest, x_slabs, n_rows, tt, sr):
    T = x_slabs.shape[0] // sr
    return pl.pallas_call(
        functools.partial(_dispatch_kernel, sr=sr),
        grid=(T // tt,),
        in_specs=[pl.BlockSpec((TOP_K, tt), lambda i: (0, i), memory_space=pltpu.SMEM),
                  pl.BlockSpec((tt * sr, LANES), lambda i: (i, 0))],
        out_specs=pl.BlockSpec(memory_space=pl.ANY),
        out_shape=jax.ShapeDtypeStruct((n_rows * sr, LANES), F32),
        scratch_shapes=[pltpu.SemaphoreType.DMA],
        compiler_params=_cparams(("arbitrary",)),
        name="dispatch",
    )(dest, x_slabs)


def _experts_kernel(te_ref, tv_ref, nu_ref, x_ref, wg_ref, wu_ref, wd_ref, y_ref, wg_bf, wu_bf, wd_bf):
    i = pl.program_id(0)

    @pl.when(i < nu_ref[0])
    def _():
        @pl.when((i == 0) | (te_ref[i] != te_ref[jnp.maximum(i - 1, 0)]))
        def _():
            wg_bf[...] = wg_ref[0].astype(BF16)
            wu_bf[...] = wu_ref[0].astype(BF16)
            wd_bf[...] = wd_ref[0].astype(BF16)

        d = wg_bf.shape[0]
        tm = x_ref.shape[0] // _slab_rows(d)
        rows = lax.broadcasted_iota(jnp.int32, (tm, 1), 0)
        x = jnp.where(rows < tv_ref[i], _load_slabs(x_ref, tm, d), 0.0).astype(BF16)
        g = _dot(x, wg_bf[...])
        act = (g * jax.nn.sigmoid(g) * _dot(x, wu_bf[...])).astype(BF16)
        _store_slabs(y_ref, _dot(act, wd_bf[...]))


def _experts(tile_e, tile_valid, n_used, xs, w_gate, w_up, w_down, tm):
    E, D, F = w_gate.shape
    sr = _slab_rows(D)
    n_tiles = xs.shape[0] // (tm * sr)

    def tile(i, te, tv, nu):
        return (jnp.minimum(i, nu[0] - 1), 0)

    def expert(i, te, tv, nu):
        return (te[jnp.minimum(i, nu[0] - 1)], 0, 0)

    return pl.pallas_call(
        _experts_kernel,
        grid_spec=pltpu.PrefetchScalarGridSpec(
            num_scalar_prefetch=3,
            grid=(n_tiles,),
            in_specs=[pl.BlockSpec((tm * sr, LANES), tile), pl.BlockSpec((1, D, F), expert),
                      pl.BlockSpec((1, D, F), expert), pl.BlockSpec((1, F, D), expert)],
            out_specs=pl.BlockSpec((tm * sr, LANES), tile),
            scratch_shapes=[pltpu.VMEM((D, F), BF16), pltpu.VMEM((D, F), BF16), pltpu.VMEM((F, D), BF16)],
        ),
        out_shape=jax.ShapeDtypeStruct(xs.shape, F32),
        compiler_params=_cparams(("arbitrary",)),
        name="experts",
    )(tile_e, tile_valid, n_used, xs, w_gate, w_up, w_down)


def _combine_kernel(dest_ref, nxt_ref, w_ref, h_ref, ys_ref, sg_ref, su_ref, sd_ref, g_ref, b_ref, out_ref,
                    buf, routed_ref, sem):
    i = pl.program_id(0)
    last = pl.num_programs(0) - 1
    tt, d = h_ref.shape
    sr = _slab_rows(d)
    slot = i % 2
    other = 1 - slot

    def start_row(idx_ref, t, to):
        for k in range(TOP_K):
            src = ys_ref.at[pl.ds(pl.multiple_of(idx_ref[k, t] * sr, sr), sr)]
            dst = buf.at[to, k, pl.ds(pl.multiple_of(t * sr, sr), sr)]
            pltpu.make_async_copy(src, dst, sem.at[to]).start(priority=k % 2)

    def wait_slot(s):
        for k in range(TOP_K):
            pltpu.make_async_copy(ys_ref.at[pl.ds(0, tt * sr)], buf.at[s, k], sem.at[s]).wait()

    @pl.when(i == 0)
    def _():
        def prime(t, c):
            start_row(dest_ref, t, 0)
            return c

        lax.fori_loop(0, tt, prime, 0)

    wait_slot(slot)

    def group(gi, carry):
        t0 = pl.multiple_of(gi * COMBINE_GROUP, COMBINE_GROUP)
        for j in range(COMBINE_GROUP):
            start_row(nxt_ref, t0 + j, other)
        w = w_ref[pl.ds(t0, COMBINE_GROUP), :]
        wk = [jnp.broadcast_to(w[:, k:k + 1], (COMBINE_GROUP, LANES)) for k in range(TOP_K)]
        for c in range(sr):
            acc = buf[slot, 0, pl.ds(t0 * sr + c, COMBINE_GROUP, stride=sr), :] * wk[0]
            for k in range(1, TOP_K):
                acc = acc + buf[slot, k, pl.ds(t0 * sr + c, COMBINE_GROUP, stride=sr), :] * wk[k]
            routed_ref[pl.ds(t0, COMBINE_GROUP), c * LANES:(c + 1) * LANES] = acc
        return carry

    lax.fori_loop(0, tt // COMBINE_GROUP, group, 0)

    h = h_ref[...]
    hb = h.astype(BF16)
    gate = _dot(hb, sg_ref[...])
    act = (gate * jax.nn.sigmoid(gate) * _dot(hb, su_ref[...])).astype(BF16)
    shared = _dot(act, sd_ref[...])
    out_ref[...] = _layer_norm(DN_ALPHA * h + (routed_ref[...] + shared), g_ref[...], b_ref[...])

    @pl.when(i == last)
    def _():
        wait_slot(other)


def _combine(dest, w_tok, h2, ys, sg, su, sd, g, b, tt):
    T, D = h2.shape
    sr = _slab_rows(D)
    row = lambda i: (i, 0)
    const = lambda i: (0, 0)
    n = T // tt
    return pl.pallas_call(
        _combine_kernel,
        grid=(n,),
        in_specs=[pl.BlockSpec((TOP_K, tt), lambda i: (0, i), memory_space=pltpu.SMEM),
                  pl.BlockSpec((TOP_K, tt), lambda i: (0, jnp.minimum(i + 1, n - 1)), memory_space=pltpu.SMEM),
                  pl.BlockSpec((tt, TOP_K), row), pl.BlockSpec((tt, D), row),
                  pl.BlockSpec(memory_space=pl.ANY),
                  pl.BlockSpec(sg.shape, const), pl.BlockSpec(su.shape, const), pl.BlockSpec(sd.shape, const),
                  pl.BlockSpec((1, D), const), pl.BlockSpec((1, D), const)],
        out_specs=pl.BlockSpec((tt, D), row),
        out_shape=jax.ShapeDtypeStruct((T, D), F32),
        scratch_shapes=[pltpu.VMEM((2, TOP_K, tt * sr, LANES), F32), pltpu.VMEM((tt, D), F32),
                        pltpu.SemaphoreType.DMA((2,))],
        compiler_params=_cparams(("arbitrary",)),
        name="combine",
    )(dest, dest, w_tok, h2, ys, sg, su, sd, g, b)


def _rope_lane_tables(positions):
    half = ROT_DIM // 2
    inv = ROPE_THETA ** (-jnp.arange(0, ROT_DIM, 2, dtype=F32) / ROT_DIM)
    ang = positions.astype(F32).reshape(-1, 1) * inv
    cos, sin = jnp.cos(ang), jnp.sin(ang)
    T = cos.shape[0]
    rest = A_HEAD_DIM - ROT_DIM
    one_head = lambda first, second, fill: jnp.concatenate(
        [first, second, jnp.full((T, rest), fill, F32)], axis=1)
    zeros = jnp.zeros_like(sin)
    cc = one_head(cos, cos, 1.0)
    s1 = one_head(-sin, zeros, 0.0)
    s2 = one_head(zeros, sin, 0.0)
    rep = LANES // A_HEAD_DIM
    return jnp.tile(cc, (1, rep)), jnp.tile(s1, (1, rep)), jnp.tile(s2, (1, rep))


def _layer(x, mem, cc, s1, s2, l, w_in, w_out, lq1, lk1, lq2, lk2, subln_g, ln1_g, ln1_b,
           wq_mem, wkv_mem, wo_mem, ln2_g, ln2_b, w_router, e_bias, w_gate, w_up, w_down,
           ws_gate, ws_up, ws_down, ln3_g, ln3_b):
    B, S, D = x.shape
    T = B * S
    lambda_init = 0.8 - 0.6 * math.exp(-0.3 * l)
    x2 = x.reshape(T, D)
    tm = min(512, S)

    w_bf = w_in.astype(BF16)
    o_va, o_qb, o_vb = 2 * A_WIDTH, 3 * A_WIDTH, 3 * A_WIDTH + 2 * B_QK_WIDTH
    w_qk = jnp.concatenate([w_bf[:, :o_va], w_bf[:, o_qb:o_vb]], axis=1)
    w_vt = jnp.concatenate([w_bf[:, o_va:o_qb], w_bf[:, o_vb:]], axis=1).T
    dils = tuple(d for _, d in A_PAIRS if d > 1)
    qa, ka, qb, kb, vat, vbt, *strided = _in_proj(x2, w_qk, w_vt, cc, s1, s2, tm, S, dils)
    strided_qk = {d: (strided[2 * n], strided[2 * n + 1]) for n, d in enumerate(dils)}

    outs, lses = [], []
    for window, d in A_PAIRS:
        assert window // d == BAND
        L = S // d
        q3, k3 = (qa, ka) if d == 1 else strided_qk[d]
        q3, k3 = q3.reshape(B * d, L, A_WIDTH), k3.reshape(B * d, L, A_WIDTH)
        vt3 = vat.reshape(B, A_WIDTH, L, d).transpose(0, 3, 1, 2).reshape(B * d, A_WIDTH, L)
        o, lse = _dilated(q3, k3, vt3, min(2 * BAND, L))
        outs.append(o.reshape(T, A_WIDTH) if d == 1 else o.reshape(B, d, L, A_WIDTH))
        lse = lse.reshape(B, d, A_HEADS, L).transpose(0, 3, 1, 2).reshape(T, A_HEADS)
        lses.append(jnp.pad(lse, ((0, 0), (0, LANES - A_HEADS))))

    row = lambda v: v.reshape(1, -1).astype(F32)
    ob = _diff_attention(qb.reshape(B, S, -1), kb.reshape(B, S, -1), vbt, row(lq1), row(lk1), row(lq2), row(lk2),
                         subln_g.reshape(-1, 1).astype(F32), lambda_init, min(1024, S), min(512, S))

    expand = (jnp.arange(LANES)[:, None] == (jnp.arange(A_WIDTH)[None, :] // A_HEAD_DIM)).astype(BF16)
    M = mem.shape[1]
    kv = _mem_kv(mem.reshape(B * M, D), wkv_mem.astype(BF16), min(512, B * M))
    wr_t = w_router.T
    wr_hi = wr_t.astype(BF16)
    wr_lo = (wr_t - wr_hi.astype(F32)).astype(BF16)
    h2, h2_slabs, idx, w_route, rank, counts = _token_stage(
        outs, lses, ob.reshape(T, -1), x2, w_out.astype(BF16), expand, row(ln1_g), row(ln1_b),
        kv, wq_mem.astype(BF16), wo_mem.astype(BF16), row(ln2_g), row(ln2_b),
        wr_hi, wr_lo, e_bias.reshape(-1, 1).astype(F32), tm, S // tm)

    tile_rows = 512
    counts = counts.reshape(-1)
    padded = (counts + tile_rows - 1) // tile_rows * tile_rows
    pend = jnp.cumsum(padded)
    pstart = pend - padded
    n_tiles = (T * TOP_K) // tile_rows + N_EXPERTS
    n_rows = n_tiles * tile_rows
    tile_lo = jnp.arange(n_tiles, dtype=jnp.int32) * tile_rows
    tile_e = jnp.minimum(jnp.sum(pend[None, :] <= tile_lo[:, None], axis=1), N_EXPERTS - 1).astype(jnp.int32)
    tile_valid = jnp.clip(counts[tile_e] - (tile_lo - pstart[tile_e]), 0, tile_rows).astype(jnp.int32)
    n_used = (pend[-1] // tile_rows).astype(jnp.int32).reshape(1)

    dest = _dest_rows(idx, rank, pstart.reshape(-1, 1).astype(jnp.int32), min(2048, T))
    xs = _dispatch(dest, h2_slabs, n_rows, min(512, S), _slab_rows(D))
    ys = _experts(tile_e, tile_valid, n_used, xs, w_gate, w_up, w_down, tile_rows)
    out = _combine(dest, w_route.T, h2, ys, ws_gate.astype(BF16), ws_up.astype(BF16), ws_down.astype(BF16),
                   row(ln3_g), row(ln3_b), min(512, S))
    return out.reshape(B, S, D)


def kernel(x, mem, positions, w_in, w_out, lambda_q1, lambda_k1, lambda_q2, lambda_k2, subln_g, ln1_g, ln1_b,
           wq_mem, wkv_mem, wo_mem, ln2_g, ln2_b, w_router, e_bias, w_gate, w_up, w_down, ws_gate, ws_up, ws_down,
           ln3_g, ln3_b):
    cc, s1, s2 = _rope_lane_tables(positions)
    h = x
    for l in range(w_in.shape[0]):
        h = _layer(h, mem, cc, s1, s2, l, w_in[l], w_out[l], lambda_q1[l], lambda_k1[l], lambda_q2[l],
                   lambda_k2[l], subln_g[l], ln1_g[l], ln1_b[l], wq_mem[l], wkv_mem[l], wo_mem[l],
                   ln2_g[l], ln2_b[l], w_router[l], e_bias[l], w_gate[l], w_up[l], w_down[l],
                   ws_gate[l], ws_up[l], ws_down[l], ln3_g[l], ln3_b[l])
    return h
```

```python
import functools
import math

import jax
import jax.numpy as jnp
from jax import lax
from jax.experimental import pallas as pl
from jax.experimental.pallas import tpu as pltpu

A_HEADS = 8
A_HEAD_DIM = 64
A_PAIRS = ((128, 1), (512, 4), (2048, 16))
B_HEADS = 4
B_QK_DIM = 64
B_V_DIM = 2 * B_QK_DIM
A_WIDTH = A_HEADS * A_HEAD_DIM
B_QK_WIDTH = B_HEADS * 2 * B_QK_DIM
B_V_WIDTH = B_HEADS * B_V_DIM
ROPE_THETA = 500000.0
ROT_DIM = A_HEAD_DIM // 4
M_HEADS = 4
N_EXPERTS = 256
TOP_K = 8
N_GROUPS = 8
TOPK_GROUPS = 4
ROUTED_SCALE = 2.5
LN_EPS = 1e-5
SUBLN_EPS = 1e-5
DEPTH = 1
DN_ALPHA = (2 * DEPTH) ** 0.25

LANES = 128
BAND = 128
ONES_ROWS = 16
COMBINE_GROUP = 32
VMEM_LIMIT = 56 * 1024 * 1024

BF16 = jnp.bfloat16
F32 = jnp.float32
NEG_INF = float("-inf")


def _cparams(sem):
    return pltpu.CompilerParams(dimension_semantics=sem, vmem_limit_bytes=VMEM_LIMIT)


def _dot(a, b):
    return jnp.dot(a, b, preferred_element_type=F32)


def _dot_nt(a, b):
    return lax.dot_general(a, b, (((1,), (1,)), ((), ())), preferred_element_type=F32)


def _dot_tn(a, b):
    return lax.dot_general(a, b, (((0,), (0,)), ((), ())), preferred_element_type=F32)


def _layer_norm(x, g, b):
    mu = jnp.mean(x, axis=-1, keepdims=True)
    xc = x - mu
    var = jnp.mean(xc * xc, axis=-1, keepdims=True)
    return xc * lax.rsqrt(var + LN_EPS) * g + b


def _slab_rows(d):
    return d // LANES


def _store_slabs(ref, x):
    n, d = x.shape
    s = _slab_rows(d)
    for c in range(s):
        ref[pl.ds(c, n, stride=s), :] = x[:, c * LANES:(c + 1) * LANES]


def _load_slabs(ref, n, d):
    s = _slab_rows(d)
    return jnp.concatenate([ref[pl.ds(c, n, stride=s), :] for c in range(s)], axis=1)


def _to_strided(nat_ref, t, sub_refs, dils):
    tm, w = t.shape
    chunks = range(w // LANES)
    for c in chunks:
        nat_ref[c] = t[:, c * LANES:(c + 1) * LANES]
    for d, ref in zip(dils, sub_refs):
        for r in range(d):
            rows = [nat_ref[c, pl.ds(r, tm // d, stride=d), :] for c in chunks]
            ref[0, r] = jnp.concatenate(rows, axis=1).astype(ref.dtype)


def _from_strided(nat_ref, sub_ref, d):
    n, w = sub_ref.shape[2], sub_ref.shape[3]
    chunks = range(w // LANES)
    for r in range(d):
        for c in chunks:
            nat_ref[c, pl.ds(r, n, stride=d), :] = sub_ref[0, r, :, c * LANES:(c + 1) * LANES].astype(F32)
    return jnp.concatenate([nat_ref[c] for c in chunks], axis=1)


def _in_proj_kernel(x_ref, w_ref, wvt_ref, c_ref, s1_ref, s2_ref, qa_ref, ka_ref, va_ref, qb_ref, kb_ref, vbt_ref,
                    *rest, dils):
    sub_q, sub_k, sub_v, nat_ref = rest[0:-1:3], rest[1:-1:3], rest[2:-1:3], rest[-1]
    x = x_ref[...].astype(BF16)
    cc, s1, s2 = c_ref[...], s1_ref[...], s2_ref[...]

    def rope(t):
        outs = []
        for c in range(t.shape[1] // LANES):
            tc = t[:, c * LANES:(c + 1) * LANES]
            nxt = pltpu.roll(tc, LANES - ROT_DIM // 2, 1)
            prv = pltpu.roll(tc, ROT_DIM // 2, 1)
            outs.append(tc * cc + nxt * s1 + prv * s2)
        return jnp.concatenate(outs, axis=1)

    def proj(lo, width):
        return _dot(x, w_ref[:, lo:lo + width])

    o = 0
    qa = rope(proj(o, A_WIDTH)) * (A_HEAD_DIM ** -0.5); o += A_WIDTH
    qa_ref[...] = qa.astype(BF16)
    _to_strided(nat_ref, qa, sub_q, dils)
    ka = rope(proj(o, A_WIDTH)); o += A_WIDTH
    ka_ref[...] = ka.astype(BF16)
    _to_strided(nat_ref, ka, sub_k, dils)
    va = proj(o, A_WIDTH); o += A_WIDTH
    va_ref[...] = va.astype(BF16)
    _to_strided(nat_ref, va, sub_v, dils)
    qb_ref[...] = (rope(proj(o, B_QK_WIDTH)) * (B_QK_DIM ** -0.5)).astype(BF16); o += B_QK_WIDTH
    kb_ref[...] = rope(proj(o, B_QK_WIDTH)).astype(BF16)
    vbt_ref[0] = _dot_nt(wvt_ref[...], x).astype(BF16)


def _in_proj(x2, w_qk, w_vt, cc, s1, s2, tm, seq, dils):
    T, D = x2.shape
    nt = seq // tm
    nb = T // seq
    row = lambda i: (i, 0)
    const = lambda i: (0, 0)
    tcol = lambda i: (i // nt, 0, i % nt)
    outs = [jax.ShapeDtypeStruct((T, A_WIDTH), BF16)] * 3 + [jax.ShapeDtypeStruct((T, B_QK_WIDTH), BF16)] * 2
    outs_t = [jax.ShapeDtypeStruct((nb, B_V_WIDTH, seq), BF16)]
    outs_s = [jax.ShapeDtypeStruct((nb, d, seq // d, A_WIDTH), BF16) for d in dils for _ in range(3)]
    return pl.pallas_call(
        functools.partial(_in_proj_kernel, dils=dils),
        grid=(T // tm,),
        in_specs=[pl.BlockSpec((tm, D), row), pl.BlockSpec(w_qk.shape, const), pl.BlockSpec(w_vt.shape, const),
                  pl.BlockSpec((tm, LANES), row), pl.BlockSpec((tm, LANES), row), pl.BlockSpec((tm, LANES), row)],
        out_specs=[pl.BlockSpec((tm, o.shape[1]), row) for o in outs] + [
            pl.BlockSpec((1, o.shape[1], tm), tcol) for o in outs_t] + [
            pl.BlockSpec((1, o.shape[1], tm // o.shape[1], A_WIDTH), lambda i: (i // nt, 0, i % nt, 0)) for o in outs_s],
        out_shape=outs + outs_t + outs_s,
        scratch_shapes=[pltpu.VMEM((A_WIDTH // LANES, tm, LANES), F32)],
        compiler_params=_cparams(("parallel",)),
        name="in_proj",
    )(x2, w_qk, w_vt, cc, s1, s2)


def _dilated_kernel(q_ref, kp_ref, kc_ref, vp_ref, vc_ref, o_ref, lse_ref):
    n = pl.program_id(1)
    tq = q_ref.shape[1]
    nk = BAND + tq
    two = LANES // A_HEAD_DIM
    kr = lax.broadcasted_iota(jnp.int32, (nk, two * tq), 0)
    qi = lax.broadcasted_iota(jnp.int32, (nk, two * tq), 1) % tq
    ok = (kr >= jnp.where(n > 0, qi, jnp.maximum(qi, BAND))) & (kr <= qi + BAND)
    lane = lax.broadcasted_iota(jnp.int32, (tq, LANES), 1)
    row = lax.broadcasted_iota(jnp.int32, (LANES, tq), 0)
    chunks = [slice(c * LANES, (c + 1) * LANES) for c in range(A_WIDTH // LANES)]
    scores = []
    for sl in chunks:
        q = q_ref[0, :, sl]
        zero = jnp.zeros_like(q)
        q2 = jnp.concatenate([jnp.where((lane >= h * A_HEAD_DIM) & (lane < (h + 1) * A_HEAD_DIM), q, zero)
                              for h in range(two)], axis=0)
        kband = jnp.concatenate([kp_ref[0, :, sl], kc_ref[0, :, sl]], axis=0)
        scores.append(_dot_nt(kband, q2))
    probs, lses = [], []
    for st in scores:
        s = jnp.where(ok, st, NEG_INF)
        m = jnp.max(s, axis=0, keepdims=True)
        p = jnp.exp(s - m)
        den = jnp.sum(p, axis=0, keepdims=True)
        probs.append((p.astype(BF16), den))
        lse = m + jnp.log(den)
        lses += [lse[:, h * tq:(h + 1) * tq] for h in range(two)]
    for sl, (p, den) in zip(chunks, probs):
        vband = jnp.concatenate([vp_ref[0, :, sl], vc_ref[0, :, sl]], axis=0)
        ot = _dot_tn(vband, p) / den
        out_t = ot[:, :tq]
        for h in range(1, two):
            out_t = jnp.where(row >= h * A_HEAD_DIM, ot[:, h * tq:(h + 1) * tq], out_t)
        o_ref[0, :, sl] = out_t.T.astype(BF16)
    lses.append(jnp.zeros((LANES - len(lses), tq), F32))
    lse_ref[0] = jnp.concatenate(lses, axis=0).T


def _dilated(q, k, v, tq):
    G, L, W = q.shape
    per = tq // BAND
    cur = lambda g, n: (g, n, 0)
    prev = lambda g, n: (g, jnp.maximum(n * per - 1, 0), 0)
    return pl.pallas_call(
        _dilated_kernel,
        grid=(G, L // tq),
        in_specs=[pl.BlockSpec((1, tq, W), cur), pl.BlockSpec((1, BAND, W), prev), pl.BlockSpec((1, tq, W), cur),
                  pl.BlockSpec((1, BAND, W), prev), pl.BlockSpec((1, tq, W), cur)],
        out_specs=[pl.BlockSpec((1, tq, W), cur), pl.BlockSpec((1, tq, LANES), cur)],
        out_shape=[jax.ShapeDtypeStruct((G, L, W), BF16), jax.ShapeDtypeStruct((G, L, LANES), F32)],
        compiler_params=_cparams(("parallel", "parallel")),
        name="dilated",
    )(q, k, k, v, v)


def _diff_kernel(lq1_ref, lk1_ref, lq2_ref, lk2_ref, g_ref, q_ref, k_ref, vt_ref, o_ref, acc_ref, *, tq, tk, lambda_init):
    iq = pl.program_id(2)
    q = q_ref[0]
    lane = lax.broadcasted_iota(jnp.int32, q.shape, 1)
    zero = jnp.zeros_like(q)
    q2 = jnp.concatenate([jnp.where(lane < B_QK_DIM, q, zero), jnp.where(lane >= B_QK_DIM, q, zero)], axis=0)
    acc_ref[...] = jnp.zeros_like(acc_ref)

    def block(j, carry, diag):
        start = pl.multiple_of(j * tk, tk)
        kb = k_ref[0, pl.ds(start, tk), :]
        vt = jnp.concatenate([vt_ref[0, :, pl.ds(start, tk)], jnp.ones((ONES_ROWS, tk), BF16)], axis=0)
        st = _dot_nt(kb, q2)
        if diag is not None:
            ki = lax.broadcasted_iota(jnp.int32, (tk, tq), 0) + diag
            qi = lax.broadcasted_iota(jnp.int32, (tk, tq), 1)
            ok = ki <= qi
        out, work = [], []
        for c in range(2):
            m = carry[c]
            s = st[:, c * tq:(c + 1) * tq]
            if diag is not None:
                s = jnp.where(ok, s, NEG_INF)
            m_new = jnp.maximum(m, jnp.max(s, axis=0, keepdims=True))
            out.append(m_new)
            work.append((jnp.exp(m - m_new), jnp.exp(s - m_new).astype(BF16)))
        for c, (a, p) in enumerate(work):
            acc_ref[c] = a * acc_ref[c] + _dot(vt, p)
        return tuple(out)

    init = tuple(jnp.full((1, tq), NEG_INF, F32) for _ in range(2))
    per_tile = tq // tk
    carry = lax.fori_loop(0, iq * per_tile, lambda j, cr: block(j, cr, None), init)
    for r in range(per_tile):
        carry = block(iq * per_tile + r, carry, r * tk)

    lam = (jnp.exp(jnp.sum(lq1_ref[...] * lk1_ref[...], axis=1, keepdims=True))
           - jnp.exp(jnp.sum(lq2_ref[...] * lk2_ref[...], axis=1, keepdims=True)) + lambda_init)
    nv = B_V_DIM
    o = (acc_ref[0, :nv, :] / acc_ref[0, nv:nv + 1, :]
         - lam * (acc_ref[1, :nv, :] / acc_ref[1, nv:nv + 1, :]))
    o = o * lax.rsqrt(jnp.mean(o * o, axis=0, keepdims=True) + SUBLN_EPS) * g_ref[...]
    o_ref[0] = (o * (1.0 - lambda_init)).T.astype(BF16)


def _diff_attention(qb, kb, vbt, lq1, lk1, lq2, lk2, subln_g, lambda_init, tq, tk):
    B, S, _ = qb.shape
    vec = lambda n: pl.BlockSpec((1, n), lambda b, h, i: (0, 0))
    return pl.pallas_call(
        functools.partial(_diff_kernel, tq=tq, tk=tk, lambda_init=lambda_init),
        grid=(B, B_HEADS, S // tq),
        in_specs=[vec(B_QK_DIM)] * 4 + [pl.BlockSpec((B_V_DIM, 1), lambda b, h, i: (0, 0)),
                  pl.BlockSpec((1, tq, B_V_DIM), lambda b, h, i: (b, i, h)),
                  pl.BlockSpec((1, S, B_V_DIM), lambda b, h, i: (b, 0, h)),
                  pl.BlockSpec((1, B_V_DIM, S), lambda b, h, i: (b, h, 0))],
        out_specs=pl.BlockSpec((1, tq, B_V_DIM), lambda b, h, i: (b, i, h)),
        out_shape=jax.ShapeDtypeStruct((B, S, B_V_WIDTH), BF16),
        scratch_shapes=[pltpu.VMEM((2, B_V_DIM + ONES_ROWS, tq), F32)],
        compiler_params=_cparams(("parallel", "parallel", "arbitrary")),
        name="diff_attn",
    )(lq1, lk1, lq2, lk2, subln_g, qb, kb, vbt)


def _mix_out_kernel(o1_ref, o2_ref, o3_ref, l1_ref, l2_ref, l3_ref, ob_ref, x_ref, w_ref, e_ref, g_ref, b_ref, h_ref,
                    nat_ref):
    ls = [l_ref[...] if d == 1 else _from_strided(nat_ref, l_ref, d)
          for l_ref, (_, d) in zip((l1_ref, l2_ref, l3_ref), A_PAIRS)]
    mx = jnp.maximum(jnp.maximum(ls[0], ls[1]), ls[2])
    es = [jnp.exp(l - mx) for l in ls]
    den = es[0] + es[1] + es[2]
    e = e_ref[...]
    oa = None
    for ex, o_ref, (_, d) in zip(es, (o1_ref, o2_ref, o3_ref), A_PAIRS):
        wt = ex / den
        hi = wt.astype(BF16)
        lo = (wt - hi.astype(F32)).astype(BF16)
        wexp = _dot(hi, e) + _dot(lo, e)
        o = o_ref[...].astype(F32) if d == 1 else _from_strided(nat_ref, o_ref, d)
        term = wexp * o
        oa = term if oa is None else oa + term
    y = _dot(oa.astype(BF16), w_ref[:A_WIDTH, :]) + _dot(ob_ref[...], w_ref[A_WIDTH:, :])
    h_ref[...] = _layer_norm(DN_ALPHA * x_ref[...] + y, g_ref[...], b_ref[...])


def _matmul_kernel(a_ref, w_ref, o_ref):
    o_ref[...] = _dot(a_ref[...].astype(BF16), w_ref[...]).astype(o_ref.dtype)


def _mem_kv(mem2, wkv, tm):
    R, D = mem2.shape
    N = wkv.shape[1]
    return pl.pallas_call(
        _matmul_kernel,
        grid=(R // tm,),
        in_specs=[pl.BlockSpec((tm, D), lambda i: (i, 0)), pl.BlockSpec((D, N), lambda i: (0, 0))],
        out_specs=pl.BlockSpec((tm, N), lambda i: (i, 0)),
        out_shape=jax.ShapeDtypeStruct((R, N), BF16),
        compiler_params=_cparams(("parallel",)),
        name="mem_kv",
    )(mem2, wkv)


def _mem_attn_kernel(h_ref, kv_ref, wq_ref, wo_ref, g_ref, b_ref, out_ref, slab_ref):
    h = h_ref[...]
    D = h.shape[1]
    hd = D // M_HEADS
    q = (_dot(h.astype(BF16), wq_ref[...]) * (hd ** -0.5)).astype(BF16)
    heads = []
    for i in range(M_HEADS):
        k = kv_ref[:, i * hd:(i + 1) * hd]
        v = kv_ref[:, D + i * hd:D + (i + 1) * hd]
        s = _dot_nt(q[:, i * hd:(i + 1) * hd], k)
        p = jnp.exp(s - jnp.max(s, axis=1, keepdims=True))
        p = p / jnp.sum(p, axis=1, keepdims=True)
        heads.append(_dot(p.astype(BF16), v).astype(BF16))
    o = jnp.concatenate(heads, axis=1)
    h2 = _layer_norm(DN_ALPHA * h + _dot(o, wo_ref[...]), g_ref[...], b_ref[...])
    out_ref[...] = h2
    _store_slabs(slab_ref, h2)


def _argmax_rows(v, row_iota, n_rows):
    m = jnp.max(v, axis=0, keepdims=True)
    idx = jnp.min(jnp.where(v == m, row_iota, n_rows), axis=0, keepdims=True)
    return m, idx


def _router_kernel(h_ref, whi_ref, wlo_ref, bias_ref, idx_ref, w_ref, rank_ref, cnt_ref, carry_ref):
    @pl.when(pl.program_id(0) == 0)
    def _():
        carry_ref[...] = jnp.zeros_like(carry_ref)

    h = h_ref[...]
    tt = h.shape[0]
    xhi = h.astype(BF16)
    xlo = (h - xhi.astype(F32)).astype(BF16)
    whi, wlo = whi_ref[...], wlo_ref[...]
    logits = _dot_nt(whi, xhi) + (_dot_nt(wlo, xhi) + _dot_nt(whi, xlo))
    scores = jax.nn.sigmoid(logits)
    biased = scores + bias_ref[...]

    gsz = N_EXPERTS // N_GROUPS
    giota = lax.broadcasted_iota(jnp.int32, (gsz, tt), 0)
    gscore = []
    for g in range(N_GROUPS):
        blk = biased[g * gsz:(g + 1) * gsz]
        m1, i1 = _argmax_rows(blk, giota, gsz)
        m2 = jnp.max(jnp.where(giota == i1, NEG_INF, blk), axis=0, keepdims=True)
        gscore.append(m1 + m2)
    gscore = jnp.concatenate(gscore, axis=0)
    riota = lax.broadcasted_iota(jnp.int32, (N_GROUPS, tt), 0)
    gsel = jnp.zeros((N_GROUPS, tt), F32)
    for _ in range(TOPK_GROUPS):
        _, gi = _argmax_rows(gscore, riota, N_GROUPS)
        hit = riota == gi
        gsel = jnp.where(hit, 1.0, gsel)
        gscore = jnp.where(hit, NEG_INF, gscore)
    masked = jnp.concatenate(
        [jnp.where(gsel[g:g + 1] > 0.5, biased[g * gsz:(g + 1) * gsz], NEG_INF) for g in range(N_GROUPS)], axis=0)

    eiota = lax.broadcasted_iota(jnp.int32, (N_EXPERTS, tt), 0)
    chosen = jnp.zeros((N_EXPERTS, tt), F32)
    idxs, ws = [], []
    for _ in range(TOP_K):
        _, ei = _argmax_rows(masked, eiota, N_EXPERTS)
        hit = eiota == ei
        idxs.append(ei)
        ws.append(jnp.sum(jnp.where(hit, scores, 0.0), axis=0, keepdims=True))
        chosen = jnp.where(hit, 1.0, chosen)
        masked = jnp.where(hit, NEG_INF, masked)
    idx = jnp.concatenate(idxs, axis=0)
    w = jnp.concatenate(ws, axis=0)
    w = w / jnp.sum(w, axis=0, keepdims=True) * ROUTED_SCALE

    before = (lax.broadcasted_iota(jnp.int32, (tt, tt), 0) < lax.broadcasted_iota(jnp.int32, (tt, tt), 1))
    within = _dot(chosen.astype(BF16), before.astype(BF16))
    erank = within + carry_ref[...]
    ranks = [jnp.sum(jnp.where(eiota == idxs[k], erank, 0.0), axis=0, keepdims=True) for k in range(TOP_K)]
    carry = carry_ref[...] + jnp.sum(chosen, axis=1, keepdims=True)
    carry_ref[...] = carry

    idx_ref[...] = idx
    w_ref[...] = w
    rank_ref[...] = jnp.concatenate(ranks, axis=0).astype(jnp.int32)
    cnt_ref[...] = carry.astype(jnp.int32)


def _token_stage_kernel(o1_ref, o2_ref, o3_ref, l1_ref, l2_ref, l3_ref, ob_ref, x_ref, wout_ref, e_ref, g1_ref, b1_ref,
                        kv_ref, wq_ref, wo_ref, g2_ref, b2_ref, whi_ref, wlo_ref, bias_ref,
                        h2_ref, slab_ref, idx_ref, w_ref, rank_ref, cnt_ref, h1_ref, carry_ref, nat_ref):
    _mix_out_kernel(o1_ref, o2_ref, o3_ref, l1_ref, l2_ref, l3_ref, ob_ref, x_ref, wout_ref, e_ref, g1_ref, b1_ref, h1_ref,
                    nat_ref)
    _mem_attn_kernel(h1_ref, kv_ref, wq_ref, wo_ref, g2_ref, b2_ref, h2_ref, slab_ref)
    _router_kernel(h2_ref, whi_ref, wlo_ref, bias_ref, idx_ref, w_ref, rank_ref, cnt_ref, carry_ref)


def _token_stage(os_, ls_, ob, x2, w_out, expand, g1, b1, kv, wq, wo, g2, b2, whi, wlo, bias, tm, tiles_per_batch):
    T, D = x2.shape
    M = kv.shape[0] // (T // (tm * tiles_per_batch))
    row = lambda i: (i, 0)
    col = lambda i: (0, i)
    const = lambda i: (0, 0)
    full = lambda a: pl.BlockSpec(a.shape, const)

    def o_spec(o):
        if o.ndim == 2:
            return pl.BlockSpec((tm, o.shape[1]), row)
        d = o.shape[1]
        return pl.BlockSpec((1, d, tm // d, o.shape[3]), lambda i: (i // tiles_per_batch, 0, i % tiles_per_batch, 0))

    return pl.pallas_call(
        _token_stage_kernel,
        grid=(T // tm,),
        in_specs=[o_spec(o) for o in os_] + [o_spec(l) for l in ls_] + [
            pl.BlockSpec((tm, B_V_WIDTH), row), pl.BlockSpec((tm, D), row), full(w_out), full(expand), full(g1), full(b1),
            pl.BlockSpec((M, 2 * D), lambda i: (i // tiles_per_batch, 0)), full(wq), full(wo), full(g2), full(b2),
            full(whi), full(wlo), full(bias)],
        out_specs=[pl.BlockSpec((tm, D), row), pl.BlockSpec((tm * _slab_rows(D), LANES), row),
                   pl.BlockSpec((TOP_K, tm), col), pl.BlockSpec((TOP_K, tm), col), pl.BlockSpec((TOP_K, tm), col),
                   pl.BlockSpec((N_EXPERTS, 1), const)],
        out_shape=[jax.ShapeDtypeStruct((T, D), F32), jax.ShapeDtypeStruct((T * _slab_rows(D), LANES), F32),
                   jax.ShapeDtypeStruct((TOP_K, T), jnp.int32), jax.ShapeDtypeStruct((TOP_K, T), F32),
                   jax.ShapeDtypeStruct((TOP_K, T), jnp.int32), jax.ShapeDtypeStruct((N_EXPERTS, 1), jnp.int32)],
        scratch_shapes=[pltpu.VMEM((tm, D), F32), pltpu.VMEM((N_EXPERTS, 1), F32),
                        pltpu.VMEM((A_WIDTH // LANES, tm, LANES), F32)],
        compiler_params=_cparams(("arbitrary",)),
        name="token_stage",
    )(*os_, *ls_, ob, x2, w_out, expand, g1, b1, kv, wq, wo, g2, b2, whi, wlo, bias)


def _dest_kernel(idx_ref, rank_ref, start_ref, dest_ref):
    idx = idx_ref[...]
    tt = idx.shape[1]
    eiota = lax.broadcasted_iota(jnp.int32, (N_EXPERTS, tt), 0)
    start = start_ref[...].astype(F32)
    rows = [jnp.sum(jnp.where(eiota == idx[k:k + 1], start, 0.0), axis=0, keepdims=True) for k in range(TOP_K)]
    dest_ref[...] = jnp.concatenate(rows, axis=0).astype(jnp.int32) + rank_ref[...]


def _dest_rows(idx, rank, start, tt):
    T = idx.shape[1]
    col = lambda i: (0, i)
    return pl.pallas_call(
        _dest_kernel,
        grid=(T // tt,),
        in_specs=[pl.BlockSpec((TOP_K, tt), col), pl.BlockSpec((TOP_K, tt), col),
                  pl.BlockSpec((N_EXPERTS, 1), lambda i: (0, 0))],
        out_specs=pl.BlockSpec((TOP_K, tt), col),
        out_shape=jax.ShapeDtypeStruct((TOP_K, T), jnp.int32),
        compiler_params=_cparams(("parallel",)),
        name="dest_rows",
    )(idx, rank, start)


def _dispatch_kernel(dest_ref, x_ref, xs_ref, sem, *, sr):
    tt = x_ref.shape[0] // sr

    def issue(t, c):
        src = x_ref.at[pl.ds(pl.multiple_of(t * sr, sr), sr)]
        for k in range(TOP_K):
            dst = xs_ref.at[pl.ds(pl.multiple_of(dest_ref[k, t] * sr, sr), sr)]
            pltpu.make_async_copy(src, dst, sem).start(priority=k % 2)
        return c

    lax.fori_loop(0, tt, issue, 0)
    n = TOP_K * tt * sr
    pltpu.make_async_copy(xs_ref.at[pl.ds(0, n)], xs_ref.at[pl.ds(0, n)], sem).wait()


def _dispatch(dest, x_slabs, n_rows, tt, sr):
    T = x_slabs.shape[0] // sr
    return pl.pallas_call(
        functools.partial(_dispatch_kernel, sr=sr),
        grid=(T // tt,),
        in_specs=[pl.BlockSpec((TOP_K, tt), lambda i: (0, i), memory_space=pltpu.SMEM),
                  pl.BlockSpec((tt * sr, LANES), lambda i: (i, 0))],
        out_specs=pl.BlockSpec(memory_space=pl.ANY),
        out_shape=jax.ShapeDtypeStruct((n_rows * sr, LANES), F32),
        scratch_shapes=[pltpu.SemaphoreType.DMA],
        compiler_params=_cparams(("arbitrary",)),
        name="dispatch",
    )(dest, x_slabs)


def _experts_kernel(te_ref, tv_ref, nu_ref, x_ref, wg_ref, wu_ref, wd_ref, y_ref, wg_bf, wu_bf, wd_bf):
    i = pl.program_id(0)

    @pl.when(i < nu_ref[0])
    def _():
        @pl.when((i == 0) | (te_ref[i] != te_ref[jnp.maximum(i - 1, 0)]))
        def _():
            wg_bf[...] = wg_ref[0].astype(BF16)
            wu_bf[...] = wu_ref[0].astype(BF16)
            wd_bf[...] = wd_ref[0].astype(BF16)

        d = wg_bf.shape[0]
        tm = x_ref.shape[0] // _slab_rows(d)
        rows = lax.broadcasted_iota(jnp.int32, (tm, 1), 0)
        x = jnp.where(rows < tv_ref[i], _load_slabs(x_ref, tm, d), 0.0).astype(BF16)
        g = _dot(x, wg_bf[...])
        act = (g * jax.nn.sigmoid(g) * _dot(x, wu_bf[...])).astype(BF16)
        _store_slabs(y_ref, _dot(act, wd_bf[...]))


def _experts(tile_e, tile_valid, n_used, xs, w_gate, w_up, w_down, tm):
    E, D, F = w_gate.shape
    sr = _slab_rows(D)
    n_tiles = xs.shape[0] // (tm * sr)

    def tile(i, te, tv, nu):
        return (jnp.minimum(i, nu[0] - 1), 0)

    def expert(i, te, tv, nu):
        return (te[jnp.minimum(i, nu[0] - 1)], 0, 0)

    return pl.pallas_call(
        _experts_kernel,
        grid_spec=pltpu.PrefetchScalarGridSpec(
            num_scalar_prefetch=3,
            grid=(n_tiles,),
            in_specs=[pl.BlockSpec((tm * sr, LANES), tile), pl.BlockSpec((1, D, F), expert),
                      pl.BlockSpec((1, D, F), expert), pl.BlockSpec((1, F, D), expert)],
            out_specs=pl.BlockSpec((tm * sr, LANES), tile),
            scratch_shapes=[pltpu.VMEM((D, F), BF16), pltpu.VMEM((D, F), BF16), pltpu.VMEM((F, D), BF16)],
        ),
        out_shape=jax.ShapeDtypeStruct(xs.shape, F32),
        compiler_params=_cparams(("arbitrary",)),
        name="experts",
    )(tile_e, tile_valid, n_used, xs, w_gate, w_up, w_down)


def _combine_kernel(dest_ref, nxt_ref, w_ref, h_ref, ys_ref, sg_ref, su_ref, sd_ref, g_ref, b_ref, out_ref,
                    buf, routed_ref, sem):
    i = pl.program_id(0)
    last = pl.num_programs(0) - 1
    tt, d = h_ref.shape
    sr = _slab_rows(d)
    slot = i % 2
    other = 1 - slot

    def start_row(idx_ref, t, to):
        for k in range(TOP_K):
            src = ys_ref.at[pl.ds(pl.multiple_of(idx_ref[k, t] * sr, sr), sr)]
            dst = buf.at[to, k, pl.ds(pl.multiple_of(t * sr, sr), sr)]
            pltpu.make_async_copy(src, dst, sem.at[to]).start(priority=k % 2)

    def wait_slot(s):
        for k in range(TOP_K):
            pltpu.make_async_copy(ys_ref.at[pl.ds(0, tt * sr)], buf.at[s, k], sem.at[s]).wait()

    @pl.when(i == 0)
    def _():
        def prime(t, c):
            start_row(dest_ref, t, 0)
            return c

        lax.fori_loop(0, tt, prime, 0)

    wait_slot(slot)

    def group(gi, carry):
        t0 = pl.multiple_of(gi * COMBINE_GROUP, COMBINE_GROUP)
        for j in range(COMBINE_GROUP):
            start_row(nxt_ref, t0 + j, other)
        w = w_ref[pl.ds(t0, COMBINE_GROUP), :]
        wk = [jnp.broadcast_to(w[:, k:k + 1], (COMBINE_GROUP, LANES)) for k in range(TOP_K)]
        for c in range(sr):
            acc = buf[slot, 0, pl.ds(t0 * sr + c, COMBINE_GROUP, stride=sr), :] * wk[0]
            for k in range(1, TOP_K):
                acc = acc + buf[slot, k, pl.ds(t0 * sr + c, COMBINE_GROUP, stride=sr), :] * wk[k]
            routed_ref[pl.ds(t0, COMBINE_GROUP), c * LANES:(c + 1) * LANES] = acc
        return carry

    lax.fori_loop(0, tt // COMBINE_GROUP, group, 0)

    h = h_ref[...]
    hb = h.astype(BF16)
    gate = _dot(hb, sg_ref[...])
    act = (gate * jax.nn.sigmoid(gate) * _dot(hb, su_ref[...])).astype(BF16)
    shared = _dot(act, sd_ref[...])
    out_ref[...] = _layer_norm(DN_ALPHA * h + (routed_ref[...] + shared), g_ref[...], b_ref[...])

    @pl.when(i == last)
    def _():
        wait_slot(other)


def _combine(dest, w_tok, h2, ys, sg, su, sd, g, b, tt):
    T, D = h2.shape
    sr = _slab_rows(D)
    row = lambda i: (i, 0)
    const = lambda i: (0, 0)
    n = T // tt
    return pl.pallas_call(
        _combine_kernel,
        grid=(n,),
        in_specs=[pl.BlockSpec((TOP_K, tt), lambda i: (0, i), memory_space=pltpu.SMEM),
                  pl.BlockSpec((TOP_K, tt), lambda i: (0, jnp.minimum(i + 1, n - 1)), memory_space=pltpu.SMEM),
                  pl.BlockSpec((tt, TOP_K), row), pl.BlockSpec((tt, D), row),
                  pl.BlockSpec(memory_space=pl.ANY),
                  pl.BlockSpec(sg.shape, const), pl.BlockSpec(su.shape, const), pl.BlockSpec(sd.shape, const),
                  pl.BlockSpec((1, D), const), pl.BlockSpec((1, D), const)],
        out_specs=pl.BlockSpec((tt, D), row),
        out_shape=jax.ShapeDtypeStruct((T, D), F32),
        scratch_shapes=[pltpu.VMEM((2, TOP_K, tt * sr, LANES), F32), pltpu.VMEM((tt, D), F32),
                        pltpu.SemaphoreType.DMA((2,))],
        compiler_params=_cparams(("arbitrary",)),
        name="combine",
    )(dest, dest, w_tok, h2, ys, sg, su, sd, g, b)


def _rope_lane_tables(positions):
    half = ROT_DIM // 2
    inv = ROPE_THETA ** (-jnp.arange(0, ROT_DIM, 2, dtype=F32) / ROT_DIM)
    ang = positions.astype(F32).reshape(-1, 1) * inv
    cos = jnp.tile(jnp.cos(ang), (1, LANES // half))
    sin = jnp.tile(jnp.sin(ang), (1, LANES // half))
    j = jnp.arange(LANES) % A_HEAD_DIM
    cc = jnp.where(j < ROT_DIM, cos, 1.0)
    s1 = jnp.where(j < half, -sin, 0.0)
    s2 = jnp.where((j >= half) & (j < ROT_DIM), sin, 0.0)
    return cc, s1, s2


def _layer(x, mem, cc, s1, s2, l, w_in, w_out, lq1, lk1, lq2, lk2, subln_g, ln1_g, ln1_b,
           wq_mem, wkv_mem, wo_mem, ln2_g, ln2_b, w_router, e_bias, w_gate, w_up, w_down,
           ws_gate, ws_up, ws_down, ln3_g, ln3_b):
    B, S, D = x.shape
    T = B * S
    lambda_init = 0.8 - 0.6 * math.exp(-0.3 * l)
    x2 = x.reshape(T, D)
    tm = min(512, S)

    w_bf = w_in.astype(BF16)
    o_vb = 3 * A_WIDTH + 2 * B_QK_WIDTH
    dils = tuple(d for _, d in A_PAIRS if d > 1)
    qa, ka, va, qb, kb, vbt, *strided = _in_proj(x2, w_bf[:, :o_vb], w_bf[:, o_vb:].T, cc, s1, s2, tm, S, dils)
    strided_qkv = {d: strided[3 * n:3 * n + 3] for n, d in enumerate(dils)}

    outs, lses = [], []
    for window, d in A_PAIRS:
        assert window // d == BAND
        L = S // d
        qkv = (qa, ka, va) if d == 1 else strided_qkv[d]
        o, lse = _dilated(*(t.reshape(B * d, L, A_WIDTH) for t in qkv), min(2 * BAND, L))
        shape = (lambda w: (T, w)) if d == 1 else (lambda w: (B, d, L, w))
        outs.append(o.reshape(shape(A_WIDTH)))
        lses.append(lse.reshape(shape(LANES)))

    row = lambda v: v.reshape(1, -1).astype(F32)
    ob = _diff_attention(qb.reshape(B, S, -1), kb.reshape(B, S, -1), vbt, row(lq1), row(lk1), row(lq2), row(lk2),
                         subln_g.reshape(-1, 1).astype(F32), lambda_init, min(1024, S), min(1024, S))

    expand = (jnp.arange(LANES)[:, None] == (jnp.arange(A_WIDTH)[None, :] // A_HEAD_DIM)).astype(BF16)
    M = mem.shape[1]
    kv = _mem_kv(mem.reshape(B * M, D), wkv_mem.astype(BF16), min(512, B * M))
    wr_t = w_router.T
    wr_hi = wr_t.astype(BF16)
    wr_lo = (wr_t - wr_hi.astype(F32)).astype(BF16)
    h2, h2_slabs, idx, w_route, rank, counts = _token_stage(
        outs, lses, ob.reshape(T, -1), x2, w_out.astype(BF16), expand, row(ln1_g), row(ln1_b),
        kv, wq_mem.astype(BF16), wo_mem.astype(BF16), row(ln2_g), row(ln2_b),
        wr_hi, wr_lo, e_bias.reshape(-1, 1).astype(F32), tm, S // tm)

    tile_rows = 512
    counts = counts.reshape(-1)
    padded = (counts + tile_rows - 1) // tile_rows * tile_rows
    pend = jnp.cumsum(padded)
    pstart = pend - padded
    n_tiles = (T * TOP_K) // tile_rows + N_EXPERTS
    n_rows = n_tiles * tile_rows
    tile_lo = jnp.arange(n_tiles, dtype=jnp.int32) * tile_rows
    tile_e = jnp.minimum(jnp.sum(pend[None, :] <= tile_lo[:, None], axis=1), N_EXPERTS - 1).astype(jnp.int32)
    tile_valid = jnp.clip(counts[tile_e] - (tile_lo - pstart[tile_e]), 0, tile_rows).astype(jnp.int32)
    n_used = (pend[-1] // tile_rows).astype(jnp.int32).reshape(1)

    dest = _dest_rows(idx, rank, pstart.reshape(-1, 1).astype(jnp.int32), min(2048, T))
    xs = _dispatch(dest, h2_slabs, n_rows, min(512, S), _slab_rows(D))
    ys = _experts(tile_e, tile_valid, n_used, xs, w_gate, w_up, w_down, tile_rows)
    out = _combine(dest, w_route.T, h2, ys, ws_gate.astype(BF16), ws_up.astype(BF16), ws_down.astype(BF16),
                   row(ln3_g), row(ln3_b), min(512, S))
    return out.reshape(B, S, D)


def kernel(x, mem, positions, w_in, w_out, lambda_q1, lambda_k1, lambda_q2, lambda_k2, subln_g, ln1_g, ln1_b,
           wq_mem, wkv_mem, wo_mem, ln2_g, ln2_b, w_router, e_bias, w_gate, w_up, w_down, ws_gate, ws_up, ws_down,
           ln3_g, ln3_b):
    cc, s1, s2 = _rope_lane_tables(positions)
    h = x
    for l in range(w_in.shape[0]):
        h = _layer(h, mem, cc, s1, s2, l, w_in[l], w_out[l], lambda_q1[l], lambda_k1[l], lambda_q2[l],
                   lambda_k2[l], subln_g[l], ln1_g[l], ln1_b[l], wq_mem[l], wkv_mem[l], wo_mem[l],
                   ln2_g[l], ln2_b[l], w_router[l], e_bias[l], w_gate[l], w_up[l], w_down[l],
                   ws_gate[l], ws_up[l], ws_down[l], ln3_g[l], ln3_b[l])
    return h
```

```python
import functools
import math

import jax
import jax.numpy as jnp
from jax import lax
from jax.experimental import pallas as pl
from jax.experimental.pallas import tpu as pltpu

A_HEADS = 8
A_HEAD_DIM = 64
A_PAIRS = ((128, 1), (512, 4), (2048, 16))
B_HEADS = 4
B_QK_DIM = 64
B_V_DIM = 2 * B_QK_DIM
A_WIDTH = A_HEADS * A_HEAD_DIM
B_QK_WIDTH = B_HEADS * 2 * B_QK_DIM
B_V_WIDTH = B_HEADS * B_V_DIM
ROPE_THETA = 500000.0
ROT_DIM = A_HEAD_DIM // 4
M_HEADS = 4
N_EXPERTS = 256
TOP_K = 8
N_GROUPS = 8
TOPK_GROUPS = 4
ROUTED_SCALE = 2.5
LN_EPS = 1e-5
SUBLN_EPS = 1e-5
DEPTH = 1
DN_ALPHA = (2 * DEPTH) ** 0.25

LANES = 128
BAND = 128
ONES_ROWS = 16
COMBINE_GROUP = 32
VMEM_LIMIT = 56 * 1024 * 1024

BF16 = jnp.bfloat16
F32 = jnp.float32
NEG_INF = float("-inf")


def _cparams(sem):
    return pltpu.CompilerParams(dimension_semantics=sem, vmem_limit_bytes=VMEM_LIMIT)


def _dot(a, b):
    return jnp.dot(a, b, preferred_element_type=F32)


def _dot_nt(a, b):
    return lax.dot_general(a, b, (((1,), (1,)), ((), ())), preferred_element_type=F32)


def _dot_tn(a, b):
    return lax.dot_general(a, b, (((0,), (0,)), ((), ())), preferred_element_type=F32)


def _layer_norm(x, g, b):
    mu = jnp.mean(x, axis=-1, keepdims=True)
    xc = x - mu
    var = jnp.mean(xc * xc, axis=-1, keepdims=True)
    return xc * lax.rsqrt(var + LN_EPS) * g + b


def _slab_rows(d):
    return d // LANES


def _store_slabs(ref, x):
    n, d = x.shape
    s = _slab_rows(d)
    for c in range(s):
        ref[pl.ds(c, n, stride=s), :] = x[:, c * LANES:(c + 1) * LANES]


def _load_slabs(ref, n, d):
    s = _slab_rows(d)
    return jnp.concatenate([ref[pl.ds(c, n, stride=s), :] for c in range(s)], axis=1)


def _to_strided(nat_ref, t, sub_refs, dils):
    tm, w = t.shape
    chunks = range(w // LANES)
    for c in chunks:
        nat_ref[c] = t[:, c * LANES:(c + 1) * LANES]
    for d, ref in zip(dils, sub_refs):
        for r in range(d):
            rows = [nat_ref[c, pl.ds(r, tm // d, stride=d), :] for c in chunks]
            ref[0, r] = jnp.concatenate(rows, axis=1).astype(ref.dtype)


def _from_strided(nat_ref, sub_ref, d):
    n, w = sub_ref.shape[2], sub_ref.shape[3]
    chunks = range(w // LANES)
    for r in range(d):
        for c in chunks:
            nat_ref[c, pl.ds(r, n, stride=d), :] = sub_ref[0, r, :, c * LANES:(c + 1) * LANES].astype(F32)
    return jnp.concatenate([nat_ref[c] for c in chunks], axis=1)


def _in_proj_kernel(x_ref, w_ref, wvt_ref, c_ref, s1_ref, s2_ref, qa_ref, ka_ref, va_ref, qb_ref, kb_ref, vbt_ref,
                    *rest, dils):
    sub_q, sub_k, sub_v, nat_ref = rest[0:-1:3], rest[1:-1:3], rest[2:-1:3], rest[-1]
    x = x_ref[...].astype(BF16)
    cc, s1, s2 = c_ref[...], s1_ref[...], s2_ref[...]

    def rope(t):
        outs = []
        for c in range(t.shape[1] // LANES):
            tc = t[:, c * LANES:(c + 1) * LANES]
            nxt = pltpu.roll(tc, LANES - ROT_DIM // 2, 1)
            prv = pltpu.roll(tc, ROT_DIM // 2, 1)
            outs.append(tc * cc + nxt * s1 + prv * s2)
        return jnp.concatenate(outs, axis=1)

    def proj(lo, width):
        return _dot(x, w_ref[:, lo:lo + width])

    o = 0
    qa = rope(proj(o, A_WIDTH)) * (A_HEAD_DIM ** -0.5); o += A_WIDTH
    qa_ref[...] = qa.astype(BF16)
    _to_strided(nat_ref, qa, sub_q, dils)
    ka = rope(proj(o, A_WIDTH)); o += A_WIDTH
    ka_ref[...] = ka.astype(BF16)
    _to_strided(nat_ref, ka, sub_k, dils)
    va = proj(o, A_WIDTH); o += A_WIDTH
    va_ref[...] = va.astype(BF16)
    _to_strided(nat_ref, va, sub_v, dils)
    qb_ref[...] = (rope(proj(o, B_QK_WIDTH)) * (B_QK_DIM ** -0.5)).astype(BF16); o += B_QK_WIDTH
    kb_ref[...] = rope(proj(o, B_QK_WIDTH)).astype(BF16)
    vbt_ref[0] = _dot_nt(wvt_ref[...], x).astype(BF16)


def _in_proj(x2, w_qk, w_vt, cc, s1, s2, tm, seq, dils):
    T, D = x2.shape
    nt = seq // tm
    nb = T // seq
    row = lambda i: (i, 0)
    const = lambda i: (0, 0)
    tcol = lambda i: (i // nt, 0, i % nt)
    outs = [jax.ShapeDtypeStruct((T, A_WIDTH), BF16)] * 3 + [jax.ShapeDtypeStruct((T, B_QK_WIDTH), BF16)] * 2
    outs_t = [jax.ShapeDtypeStruct((nb, B_V_WIDTH, seq), BF16)]
    outs_s = [jax.ShapeDtypeStruct((nb, d, seq // d, A_WIDTH), BF16) for d in dils for _ in range(3)]
    return pl.pallas_call(
        functools.partial(_in_proj_kernel, dils=dils),
        grid=(T // tm,),
        in_specs=[pl.BlockSpec((tm, D), row), pl.BlockSpec(w_qk.shape, const), pl.BlockSpec(w_vt.shape, const),
                  pl.BlockSpec((tm, LANES), row), pl.BlockSpec((tm, LANES), row), pl.BlockSpec((tm, LANES), row)],
        out_specs=[pl.BlockSpec((tm, o.shape[1]), row) for o in outs] + [
            pl.BlockSpec((1, o.shape[1], tm), tcol) for o in outs_t] + [
            pl.BlockSpec((1, o.shape[1], tm // o.shape[1], A_WIDTH), lambda i: (i // nt, 0, i % nt, 0)) for o in outs_s],
        out_shape=outs + outs_t + outs_s,
        scratch_shapes=[pltpu.VMEM((A_WIDTH // LANES, tm, LANES), F32)],
        compiler_params=_cparams(("parallel",)),
        name="in_proj",
    )(x2, w_qk, w_vt, cc, s1, s2)


def _dilated_kernel(q_ref, kp_ref, kc_ref, vp_ref, vc_ref, o_ref, lse_ref):
    n = pl.program_id(1)
    tq = q_ref.shape[1]
    nk = 2 * BAND
    two = LANES // A_HEAD_DIM
    kr = lax.broadcasted_iota(jnp.int32, (nk, two * BAND), 0)
    qi = lax.broadcasted_iota(jnp.int32, (nk, two * BAND), 1) % BAND
    ok_inner = (kr >= qi) & (kr <= qi + BAND)
    ok_first = (kr >= jnp.where(n > 0, qi, BAND)) & (kr <= qi + BAND)
    lane = lax.broadcasted_iota(jnp.int32, (BAND, LANES), 1)
    row = lax.broadcasted_iota(jnp.int32, (LANES, BAND), 0)
    chunks = [slice(c * LANES, (c + 1) * LANES) for c in range(A_WIDTH // LANES)]
    blocks = [(sl, b) for sl in chunks for b in range(tq // BAND)]
    scores = []
    for sl, b in blocks:
        q = q_ref[0, b * BAND:(b + 1) * BAND, sl]
        zero = jnp.zeros_like(q)
        q2 = jnp.concatenate([jnp.where((lane >= h * A_HEAD_DIM) & (lane < (h + 1) * A_HEAD_DIM), q, zero)
                              for h in range(two)], axis=0)
        prev = kp_ref[0, :, sl] if b == 0 else kc_ref[0, (b - 1) * BAND:b * BAND, sl]
        kband = jnp.concatenate([prev, kc_ref[0, b * BAND:(b + 1) * BAND, sl]], axis=0)
        scores.append(_dot_nt(kband, q2))
    probs = []
    lses = [[None] * (tq // BAND) for _ in range(A_HEADS)]
    for (sl, b), st in zip(blocks, scores):
        s = jnp.where(ok_first if b == 0 else ok_inner, st, NEG_INF)
        m = jnp.max(s, axis=0, keepdims=True)
        p = jnp.exp(s - m)
        den = jnp.sum(p, axis=0, keepdims=True)
        probs.append((p.astype(BF16), den))
        lse = m + jnp.log(den)
        for h in range(two):
            lses[sl.start // A_HEAD_DIM + h][b] = lse[:, h * BAND:(h + 1) * BAND]
    for (sl, b), (p, den) in zip(blocks, probs):
        prev = vp_ref[0, :, sl] if b == 0 else vc_ref[0, (b - 1) * BAND:b * BAND, sl]
        vband = jnp.concatenate([prev, vc_ref[0, b * BAND:(b + 1) * BAND, sl]], axis=0)
        ot = _dot_tn(vband, p) / den
        out_t = ot[:, :BAND]
        for h in range(1, two):
            out_t = jnp.where(row >= h * A_HEAD_DIM, ot[:, h * BAND:(h + 1) * BAND], out_t)
        o_ref[0, b * BAND:(b + 1) * BAND, sl] = out_t.T.astype(BF16)
    stats = [jnp.concatenate(per_head, axis=1) for per_head in lses]
    stats.append(jnp.zeros((LANES - A_HEADS, tq), F32))
    lse_ref[0] = jnp.concatenate(stats, axis=0).T


def _dilated(q, k, v, tq):
    G, L, W = q.shape
    per = tq // BAND
    cur = lambda g, n: (g, n, 0)
    prev = lambda g, n: (g, jnp.maximum(n * per - 1, 0), 0)
    return pl.pallas_call(
        _dilated_kernel,
        grid=(G, L // tq),
        in_specs=[pl.BlockSpec((1, tq, W), cur), pl.BlockSpec((1, BAND, W), prev), pl.BlockSpec((1, tq, W), cur),
                  pl.BlockSpec((1, BAND, W), prev), pl.BlockSpec((1, tq, W), cur)],
        out_specs=[pl.BlockSpec((1, tq, W), cur), pl.BlockSpec((1, tq, LANES), cur)],
        out_shape=[jax.ShapeDtypeStruct((G, L, W), BF16), jax.ShapeDtypeStruct((G, L, LANES), F32)],
        compiler_params=_cparams(("parallel", "parallel")),
        name="dilated",
    )(q, k, k, v, v)


def _diff_kernel(lq1_ref, lk1_ref, lq2_ref, lk2_ref, g_ref, q_ref, k_ref, vt_ref, o_ref, acc_ref, *, tq, tk, lambda_init):
    iq = pl.program_id(2)
    q = q_ref[0]
    lane = lax.broadcasted_iota(jnp.int32, q.shape, 1)
    zero = jnp.zeros_like(q)
    q2 = jnp.concatenate([jnp.where(lane < B_QK_DIM, q, zero), jnp.where(lane >= B_QK_DIM, q, zero)], axis=0)
    acc_ref[...] = jnp.zeros_like(acc_ref)

    def block(j, carry, diag):
        start = pl.multiple_of(j * tk, tk)
        kb = k_ref[0, pl.ds(start, tk), :]
        vt = jnp.concatenate([vt_ref[0, :, pl.ds(start, tk)], jnp.ones((ONES_ROWS, tk), BF16)], axis=0)
        st = _dot_nt(kb, q2)
        if diag is not None:
            ki = lax.broadcasted_iota(jnp.int32, (tk, tq), 0) + diag
            qi = lax.broadcasted_iota(jnp.int32, (tk, tq), 1)
            ok = ki <= qi
        out, work = [], []
        for c in range(2):
            m = carry[c]
            s = st[:, c * tq:(c + 1) * tq]
            if diag is not None:
                s = jnp.where(ok, s, NEG_INF)
            m_new = jnp.maximum(m, jnp.max(s, axis=0, keepdims=True))
            out.append(m_new)
            work.append((jnp.exp(m - m_new), jnp.exp(s - m_new).astype(BF16)))
        for c, (a, p) in enumerate(work):
            acc_ref[c] = a * acc_ref[c] + _dot(vt, p)
        return tuple(out)

    init = tuple(jnp.full((1, tq), NEG_INF, F32) for _ in range(2))
    per_tile = tq // tk
    carry = lax.fori_loop(0, iq * per_tile, lambda j, cr: block(j, cr, None), init)
    for r in range(per_tile):
        carry = block(iq * per_tile + r, carry, r * tk)

    lam = (jnp.exp(jnp.sum(lq1_ref[...] * lk1_ref[...], axis=1, keepdims=True))
           - jnp.exp(jnp.sum(lq2_ref[...] * lk2_ref[...], axis=1, keepdims=True)) + lambda_init)
    nv = B_V_DIM
    o = (acc_ref[0, :nv, :] / acc_ref[0, nv:nv + 1, :]
         - lam * (acc_ref[1, :nv, :] / acc_ref[1, nv:nv + 1, :]))
    o = o * lax.rsqrt(jnp.mean(o * o, axis=0, keepdims=True) + SUBLN_EPS) * g_ref[...]
    o_ref[0] = (o * (1.0 - lambda_init)).T.astype(BF16)


def _diff_attention(qb, kb, vbt, lq1, lk1, lq2, lk2, subln_g, lambda_init, tq, tk):
    B, S, _ = qb.shape
    vec = lambda n: pl.BlockSpec((1, n), lambda b, h, i: (0, 0))
    return pl.pallas_call(
        functools.partial(_diff_kernel, tq=tq, tk=tk, lambda_init=lambda_init),
        grid=(B, B_HEADS, S // tq),
        in_specs=[vec(B_QK_DIM)] * 4 + [pl.BlockSpec((B_V_DIM, 1), lambda b, h, i: (0, 0)),
                  pl.BlockSpec((1, tq, B_V_DIM), lambda b, h, i: (b, i, h)),
                  pl.BlockSpec((1, S, B_V_DIM), lambda b, h, i: (b, 0, h)),
                  pl.BlockSpec((1, B_V_DIM, S), lambda b, h, i: (b, h, 0))],
        out_specs=pl.BlockSpec((1, tq, B_V_DIM), lambda b, h, i: (b, i, h)),
        out_shape=jax.ShapeDtypeStruct((B, S, B_V_WIDTH), BF16),
        scratch_shapes=[pltpu.VMEM((2, B_V_DIM + ONES_ROWS, tq), F32)],
        compiler_params=_cparams(("parallel", "parallel", "arbitrary")),
        name="diff_attn",
    )(lq1, lk1, lq2, lk2, subln_g, qb, kb, vbt)


def _mix_out_kernel(o1_ref, o2_ref, o3_ref, l1_ref, l2_ref, l3_ref, ob_ref, x_ref, w_ref, e_ref, g_ref, b_ref, h_ref,
                    nat_ref):
    ls = [l_ref[...] if d == 1 else _from_strided(nat_ref, l_ref, d)
          for l_ref, (_, d) in zip((l1_ref, l2_ref, l3_ref), A_PAIRS)]
    mx = jnp.maximum(jnp.maximum(ls[0], ls[1]), ls[2])
    es = [jnp.exp(l - mx) for l in ls]
    den = es[0] + es[1] + es[2]
    e = e_ref[...]
    oa = None
    for ex, o_ref, (_, d) in zip(es, (o1_ref, o2_ref, o3_ref), A_PAIRS):
        wt = ex / den
        hi = wt.astype(BF16)
        lo = (wt - hi.astype(F32)).astype(BF16)
        wexp = _dot(hi, e) + _dot(lo, e)
        o = o_ref[...].astype(F32) if d == 1 else _from_strided(nat_ref, o_ref, d)
        term = wexp * o
        oa = term if oa is None else oa + term
    y = _dot(oa.astype(BF16), w_ref[:A_WIDTH, :]) + _dot(ob_ref[...], w_ref[A_WIDTH:, :])
    h_ref[...] = _layer_norm(DN_ALPHA * x_ref[...] + y, g_ref[...], b_ref[...])


def _matmul_kernel(a_ref, w_ref, o_ref):
    o_ref[...] = _dot(a_ref[...].astype(BF16), w_ref[...]).astype(o_ref.dtype)


def _mem_kv(mem2, wkv, tm):
    R, D = mem2.shape
    N = wkv.shape[1]
    return pl.pallas_call(
        _matmul_kernel,
        grid=(R // tm,),
        in_specs=[pl.BlockSpec((tm, D), lambda i: (i, 0)), pl.BlockSpec((D, N), lambda i: (0, 0))],
        out_specs=pl.BlockSpec((tm, N), lambda i: (i, 0)),
        out_shape=jax.ShapeDtypeStruct((R, N), BF16),
        compiler_params=_cparams(("parallel",)),
        name="mem_kv",
    )(mem2, wkv)


def _mem_attn_kernel(h_ref, kv_ref, wq_ref, wo_ref, g_ref, b_ref, out_ref, slab_ref):
    h = h_ref[...]
    D = h.shape[1]
    hd = D // M_HEADS
    q = (_dot(h.astype(BF16), wq_ref[...]) * (hd ** -0.5)).astype(BF16)
    heads = []
    for i in range(M_HEADS):
        k = kv_ref[:, i * hd:(i + 1) * hd]
        v = kv_ref[:, D + i * hd:D + (i + 1) * hd]
        s = _dot_nt(q[:, i * hd:(i + 1) * hd], k)
        p = jnp.exp(s - jnp.max(s, axis=1, keepdims=True))
        p = p / jnp.sum(p, axis=1, keepdims=True)
        heads.append(_dot(p.astype(BF16), v).astype(BF16))
    o = jnp.concatenate(heads, axis=1)
    h2 = _layer_norm(DN_ALPHA * h + _dot(o, wo_ref[...]), g_ref[...], b_ref[...])
    out_ref[...] = h2
    _store_slabs(slab_ref, h2)


def _argmax_rows(v, row_iota, n_rows):
    m = jnp.max(v, axis=0, keepdims=True)
    idx = jnp.min(jnp.where(v == m, row_iota, n_rows), axis=0, keepdims=True)
    return m, idx


def _router_kernel(h_ref, whi_ref, wlo_ref, bias_ref, idx_ref, w_ref, rank_ref, cnt_ref, carry_ref):
    @pl.when(pl.program_id(0) == 0)
    def _():
        carry_ref[...] = jnp.zeros_like(carry_ref)

    h = h_ref[...]
    tt = h.shape[0]
    xhi = h.astype(BF16)
    xlo = (h - xhi.astype(F32)).astype(BF16)
    whi, wlo = whi_ref[...], wlo_ref[...]
    logits = _dot_nt(whi, xhi) + (_dot_nt(wlo, xhi) + _dot_nt(whi, xlo))
    scores = jax.nn.sigmoid(logits)
    biased = scores + bias_ref[...]

    gsz = N_EXPERTS // N_GROUPS
    giota = lax.broadcasted_iota(jnp.int32, (gsz, tt), 0)
    gscore = []
    for g in range(N_GROUPS):
        blk = biased[g * gsz:(g + 1) * gsz]
        m1, i1 = _argmax_rows(blk, giota, gsz)
        m2 = jnp.max(jnp.where(giota == i1, NEG_INF, blk), axis=0, keepdims=True)
        gscore.append(m1 + m2)
    gscore = jnp.concatenate(gscore, axis=0)
    riota = lax.broadcasted_iota(jnp.int32, (N_GROUPS, tt), 0)
    gsel = jnp.zeros((N_GROUPS, tt), F32)
    for _ in range(TOPK_GROUPS):
        _, gi = _argmax_rows(gscore, riota, N_GROUPS)
        hit = riota == gi
        gsel = jnp.where(hit, 1.0, gsel)
        gscore = jnp.where(hit, NEG_INF, gscore)
    masked = jnp.concatenate(
        [jnp.where(gsel[g:g + 1] > 0.5, biased[g * gsz:(g + 1) * gsz], NEG_INF) for g in range(N_GROUPS)], axis=0)

    eiota = lax.broadcasted_iota(jnp.int32, (N_EXPERTS, tt), 0)
    chosen = jnp.zeros((N_EXPERTS, tt), F32)
    idxs, ws = [], []
    for _ in range(TOP_K):
        _, ei = _argmax_rows(masked, eiota, N_EXPERTS)
        hit = eiota == ei
        idxs.append(ei)
        ws.append(jnp.sum(jnp.where(hit, scores, 0.0), axis=0, keepdims=True))
        chosen = jnp.where(hit, 1.0, chosen)
        masked = jnp.where(hit, NEG_INF, masked)
    idx = jnp.concatenate(idxs, axis=0)
    w = jnp.concatenate(ws, axis=0)
    w = w / jnp.sum(w, axis=0, keepdims=True) * ROUTED_SCALE

    before = (lax.broadcasted_iota(jnp.int32, (tt, tt), 0) < lax.broadcasted_iota(jnp.int32, (tt, tt), 1))
    within = _dot(chosen.astype(BF16), before.astype(BF16))
    erank = within + carry_ref[...]
    ranks = [jnp.sum(jnp.where(eiota == idxs[k], erank, 0.0), axis=0, keepdims=True) for k in range(TOP_K)]
    carry = carry_ref[...] + jnp.sum(chosen, axis=1, keepdims=True)
    carry_ref[...] = carry

    idx_ref[...] = idx
    w_ref[...] = w
    rank_ref[...] = jnp.concatenate(ranks, axis=0).astype(jnp.int32)
    cnt_ref[...] = carry.astype(jnp.int32)


def _token_stage_kernel(o1_ref, o2_ref, o3_ref, l1_ref, l2_ref, l3_ref, ob_ref, x_ref, wout_ref, e_ref, g1_ref, b1_ref,
                        kv_ref, wq_ref, wo_ref, g2_ref, b2_ref, whi_ref, wlo_ref, bias_ref,
                        h2_ref, slab_ref, idx_ref, w_ref, rank_ref, cnt_ref, h1_ref, carry_ref, nat_ref):
    _mix_out_kernel(o1_ref, o2_ref, o3_ref, l1_ref, l2_ref, l3_ref, ob_ref, x_ref, wout_ref, e_ref, g1_ref, b1_ref, h1_ref,
                    nat_ref)
    _mem_attn_kernel(h1_ref, kv_ref, wq_ref, wo_ref, g2_ref, b2_ref, h2_ref, slab_ref)
    _router_kernel(h2_ref, whi_ref, wlo_ref, bias_ref, idx_ref, w_ref, rank_ref, cnt_ref, carry_ref)


def _token_stage(os_, ls_, ob, x2, w_out, expand, g1, b1, kv, wq, wo, g2, b2, whi, wlo, bias, tm, tiles_per_batch):
    T, D = x2.shape
    M = kv.shape[0] // (T // (tm * tiles_per_batch))
    row = lambda i: (i, 0)
    col = lambda i: (0, i)
    const = lambda i: (0, 0)
    full = lambda a: pl.BlockSpec(a.shape, const)

    def o_spec(o):
        if o.ndim == 2:
            return pl.BlockSpec((tm, o.shape[1]), row)
        d = o.shape[1]
        return pl.BlockSpec((1, d, tm // d, o.shape[3]), lambda i: (i // tiles_per_batch, 0, i % tiles_per_batch, 0))

    return pl.pallas_call(
        _token_stage_kernel,
        grid=(T // tm,),
        in_specs=[o_spec(o) for o in os_] + [o_spec(l) for l in ls_] + [
            pl.BlockSpec((tm, B_V_WIDTH), row), pl.BlockSpec((tm, D), row), full(w_out), full(expand), full(g1), full(b1),
            pl.BlockSpec((M, 2 * D), lambda i: (i // tiles_per_batch, 0)), full(wq), full(wo), full(g2), full(b2),
            full(whi), full(wlo), full(bias)],
        out_specs=[pl.BlockSpec((tm, D), row), pl.BlockSpec((tm * _slab_rows(D), LANES), row),
                   pl.BlockSpec((TOP_K, tm), col), pl.BlockSpec((TOP_K, tm), col), pl.BlockSpec((TOP_K, tm), col),
                   pl.BlockSpec((N_EXPERTS, 1), const)],
        out_shape=[jax.ShapeDtypeStruct((T, D), F32), jax.ShapeDtypeStruct((T * _slab_rows(D), LANES), F32),
                   jax.ShapeDtypeStruct((TOP_K, T), jnp.int32), jax.ShapeDtypeStruct((TOP_K, T), F32),
                   jax.ShapeDtypeStruct((TOP_K, T), jnp.int32), jax.ShapeDtypeStruct((N_EXPERTS, 1), jnp.int32)],
        scratch_shapes=[pltpu.VMEM((tm, D), F32), pltpu.VMEM((N_EXPERTS, 1), F32),
                        pltpu.VMEM((A_WIDTH // LANES, tm, LANES), F32)],
        compiler_params=_cparams(("arbitrary",)),
        name="token_stage",
    )(*os_, *ls_, ob, x2, w_out, expand, g1, b1, kv, wq, wo, g2, b2, whi, wlo, bias)


def _dest_kernel(idx_ref, rank_ref, start_ref, dest_ref):
    idx = idx_ref[...]
    tt = idx.shape[1]
    eiota = lax.broadcasted_iota(jnp.int32, (N_EXPERTS, tt), 0)
    start = start_ref[...].astype(F32)
    rows = [jnp.sum(jnp.where(eiota == idx[k:k + 1], start, 0.0), axis=0, keepdims=True) for k in range(TOP_K)]
    dest_ref[...] = jnp.concatenate(rows, axis=0).astype(jnp.int32) + rank_ref[...]


def _dest_rows(idx, rank, start, tt):
    T = idx.shape[1]
    col = lambda i: (0, i)
    return pl.pallas_call(
        _dest_kernel,
        grid=(T // tt,),
        in_specs=[pl.BlockSpec((TOP_K, tt), col), pl.BlockSpec((TOP_K, tt), col),
                  pl.BlockSpec((N_EXPERTS, 1), lambda i: (0, 0))],
        out_specs=pl.BlockSpec((TOP_K, tt), col),
        out_shape=jax.ShapeDtypeStruct((TOP_K, T), jnp.int32),
        compiler_params=_cparams(("parallel",)),
        name="dest_rows",
    )(idx, rank, start)


def _dispatch_kernel(dest_ref, x_ref, xs_ref, sem, *, sr):
    tt = x_ref.shape[0] // sr

    def issue(t, c):
        src = x_ref.at[pl.ds(pl.multiple_of(t * sr, sr), sr)]
        for k in range(TOP_K):
            dst = xs_ref.at[pl.ds(pl.multiple_of(dest_ref[k, t] * sr, sr), sr)]
            pltpu.make_async_copy(src, dst, sem).start(priority=k % 2)
        return c

    lax.fori_loop(0, tt, issue, 0)
    n = TOP_K * tt * sr
    pltpu.make_async_copy(xs_ref.at[pl.ds(0, n)], xs_ref.at[pl.ds(0, n)], sem).wait()


def _dispatch(dest, x_slabs, n_rows, tt, sr):
    T = x_slabs.shape[0] // sr
    return pl.pallas_call(
        functools.partial(_dispatch_kernel, sr=sr),
        grid=(T // tt,),
        in_specs=[pl.BlockSpec((TOP_K, tt), lambda i: (0, i), memory_space=pltpu.SMEM),
                  pl.BlockSpec((tt * sr, LANES), lambda i: (i, 0))],
        out_specs=pl.BlockSpec(memory_space=pl.ANY),
        out_shape=jax.ShapeDtypeStruct((n_rows * sr, LANES), F32),
        scratch_shapes=[pltpu.SemaphoreType.DMA],
        compiler_params=_cparams(("arbitrary",)),
        name="dispatch",
    )(dest, x_slabs)


def _experts_kernel(te_ref, tv_ref, nu_ref, x_ref, wg_ref, wu_ref, wd_ref, y_ref, wg_bf, wu_bf, wd_bf):
    i = pl.program_id(0)

    @pl.when(i < nu_ref[0])
    def _():
        @pl.when((i == 0) | (te_ref[i] != te_ref[jnp.maximum(i - 1, 0)]))
        def _():
            wg_bf[...] = wg_ref[0].astype(BF16)
            wu_bf[...] = wu_ref[0].astype(BF16)
            wd_bf[...] = wd_ref[0].astype(BF16)

        d = wg_bf.shape[0]
        tm = x_ref.shape[0] // _slab_rows(d)
        rows = lax.broadcasted_iota(jnp.int32, (tm, 1), 0)
        x = jnp.where(rows < tv_ref[i], _load_slabs(x_ref, tm, d), 0.0).astype(BF16)
        g = _dot(x, wg_bf[...])
        act = (g * jax.nn.sigmoid(g) * _dot(x, wu_bf[...])).astype(BF16)
        _store_slabs(y_ref, _dot(act, wd_bf[...]))


def _experts(tile_e, tile_valid, n_used, xs, w_gate, w_up, w_down, tm):
    E, D, F = w_gate.shape
    sr = _slab_rows(D)
    n_tiles = xs.shape[0] // (tm * sr)

    def tile(i, te, tv, nu):
        return (jnp.minimum(i, nu[0] - 1), 0)

    def expert(i, te, tv, nu):
        return (te[jnp.minimum(i, nu[0] - 1)], 0, 0)

    return pl.pallas_call(
        _experts_kernel,
        grid_spec=pltpu.PrefetchScalarGridSpec(
            num_scalar_prefetch=3,
            grid=(n_tiles,),
            in_specs=[pl.BlockSpec((tm * sr, LANES), tile), pl.BlockSpec((1, D, F), expert),
                      pl.BlockSpec((1, D, F), expert), pl.BlockSpec((1, F, D), expert)],
            out_specs=pl.BlockSpec((tm * sr, LANES), tile),
            scratch_shapes=[pltpu.VMEM((D, F), BF16), pltpu.VMEM((D, F), BF16), pltpu.VMEM((F, D), BF16)],
        ),
        out_shape=jax.ShapeDtypeStruct(xs.shape, F32),
        compiler_params=_cparams(("arbitrary",)),
        name="experts",
    )(tile_e, tile_valid, n_used, xs, w_gate, w_up, w_down)


def _combine_kernel(dest_ref, nxt_ref, w_ref, h_ref, ys_ref, sg_ref, su_ref, sd_ref, g_ref, b_ref, out_ref,
                    buf, routed_ref, sem):
    i = pl.program_id(0)
    last = pl.num_programs(0) - 1
    tt, d = h_ref.shape
    sr = _slab_rows(d)
    slot = i % 2
    other = 1 - slot

    def start_row(idx_ref, t, to):
        for k in range(TOP_K):
            src = ys_ref.at[pl.ds(pl.multiple_of(idx_ref[k, t] * sr, sr), sr)]
            dst = buf.at[to, k, pl.ds(pl.multiple_of(t * sr, sr), sr)]
            pltpu.make_async_copy(src, dst, sem.at[to]).start(priority=k % 2)

    def wait_slot(s):
        for k in range(TOP_K):
            pltpu.make_async_copy(ys_ref.at[pl.ds(0, tt * sr)], buf.at[s, k], sem.at[s]).wait()

    @pl.when(i == 0)
    def _():
        def prime(t, c):
            start_row(dest_ref, t, 0)
            return c

        lax.fori_loop(0, tt, prime, 0)

    wait_slot(slot)

    def group(gi, carry):
        t0 = pl.multiple_of(gi * COMBINE_GROUP, COMBINE_GROUP)
        for j in range(COMBINE_GROUP):
            start_row(nxt_ref, t0 + j, other)
        w = w_ref[pl.ds(t0, COMBINE_GROUP), :]
        wk = [jnp.broadcast_to(w[:, k:k + 1], (COMBINE_GROUP, LANES)) for k in range(TOP_K)]
        for c in range(sr):
            acc = buf[slot, 0, pl.ds(t0 * sr + c, COMBINE_GROUP, stride=sr), :] * wk[0]
            for k in range(1, TOP_K):
                acc = acc + buf[slot, k, pl.ds(t0 * sr + c, COMBINE_GROUP, stride=sr), :] * wk[k]
            routed_ref[pl.ds(t0, COMBINE_GROUP), c * LANES:(c + 1) * LANES] = acc
        return carry

    lax.fori_loop(0, tt // COMBINE_GROUP, group, 0)

    h = h_ref[...]
    hb = h.astype(BF16)
    gate = _dot(hb, sg_ref[...])
    act = (gate * jax.nn.sigmoid(gate) * _dot(hb, su_ref[...])).astype(BF16)
    shared = _dot(act, sd_ref[...])
    out_ref[...] = _layer_norm(DN_ALPHA * h + (routed_ref[...] + shared), g_ref[...], b_ref[...])

    @pl.when(i == last)
    def _():
        wait_slot(other)


def _combine(dest, w_tok, h2, ys, sg, su, sd, g, b, tt):
    T, D = h2.shape
    sr = _slab_rows(D)
    row = lambda i: (i, 0)
    const = lambda i: (0, 0)
    n = T // tt
    return pl.pallas_call(
        _combine_kernel,
        grid=(n,),
        in_specs=[pl.BlockSpec((TOP_K, tt), lambda i: (0, i), memory_space=pltpu.SMEM),
                  pl.BlockSpec((TOP_K, tt), lambda i: (0, jnp.minimum(i + 1, n - 1)), memory_space=pltpu.SMEM),
                  pl.BlockSpec((tt, TOP_K), row), pl.BlockSpec((tt, D), row),
                  pl.BlockSpec(memory_space=pl.ANY),
                  pl.BlockSpec(sg.shape, const), pl.BlockSpec(su.shape, const), pl.BlockSpec(sd.shape, const),
                  pl.BlockSpec((1, D), const), pl.BlockSpec((1, D), const)],
        out_specs=pl.BlockSpec((tt, D), row),
        out_shape=jax.ShapeDtypeStruct((T, D), F32),
        scratch_shapes=[pltpu.VMEM((2, TOP_K, tt * sr, LANES), F32), pltpu.VMEM((tt, D), F32),
                        pltpu.SemaphoreType.DMA((2,))],
        compiler_params=_cparams(("arbitrary",)),
        name="combine",
    )(dest, dest, w_tok, h2, ys, sg, su, sd, g, b)


def _rope_lane_tables(positions):
    half = ROT_DIM // 2
    inv = ROPE_THETA ** (-jnp.arange(0, ROT_DIM, 2, dtype=F32) / ROT_DIM)
    ang = positions.astype(F32).reshape(-1, 1) * inv
    cos = jnp.tile(jnp.cos(ang), (1, LANES // half))
    sin = jnp.tile(jnp.sin(ang), (1, LANES // half))
    j = jnp.arange(LANES) % A_HEAD_DIM
    cc = jnp.where(j < ROT_DIM, cos, 1.0)
    s1 = jnp.where(j < half, -sin, 0.0)
    s2 = jnp.where((j >= half) & (j < ROT_DIM), sin, 0.0)
    return cc, s1, s2


def _layer(x, mem, cc, s1, s2, l, w_in, w_out, lq1, lk1, lq2, lk2, subln_g, ln1_g, ln1_b,
           wq_mem, wkv_mem, wo_mem, ln2_g, ln2_b, w_router, e_bias, w_gate, w_up, w_down,
           ws_gate, ws_up, ws_down, ln3_g, ln3_b):
    B, S, D = x.shape
    T = B * S
    lambda_init = 0.8 - 0.6 * math.exp(-0.3 * l)
    x2 = x.reshape(T, D)
    tm = min(512, S)

    w_bf = w_in.astype(BF16)
    o_vb = 3 * A_WIDTH + 2 * B_QK_WIDTH
    dils = tuple(d for _, d in A_PAIRS if d > 1)
    qa, ka, va, qb, kb, vbt, *strided = _in_proj(x2, w_bf[:, :o_vb], w_bf[:, o_vb:].T, cc, s1, s2, tm, S, dils)
    strided_qkv = {d: strided[3 * n:3 * n + 3] for n, d in enumerate(dils)}

    outs, lses = [], []
    for window, d in A_PAIRS:
        assert window // d == BAND
        L = S // d
        qkv = (qa, ka, va) if d == 1 else strided_qkv[d]
        o, lse = _dilated(*(t.reshape(B * d, L, A_WIDTH) for t in qkv), min(2 * BAND, L))
        shape = (lambda w: (T, w)) if d == 1 else (lambda w: (B, d, L, w))
        outs.append(o.reshape(shape(A_WIDTH)))
        lses.append(lse.reshape(shape(LANES)))

    row = lambda v: v.reshape(1, -1).astype(F32)
    ob = _diff_attention(qb.reshape(B, S, -1), kb.reshape(B, S, -1), vbt, row(lq1), row(lk1), row(lq2), row(lk2),
                         subln_g.reshape(-1, 1).astype(F32), lambda_init, min(1024, S), min(1024, S))

    expand = (jnp.arange(LANES)[:, None] == (jnp.arange(A_WIDTH)[None, :] // A_HEAD_DIM)).astype(BF16)
    M = mem.shape[1]
    kv = _mem_kv(mem.reshape(B * M, D), wkv_mem.astype(BF16), min(512, B * M))
    wr_t = w_router.T
    wr_hi = wr_t.astype(BF16)
    wr_lo = (wr_t - wr_hi.astype(F32)).astype(BF16)
    h2, h2_slabs, idx, w_route, rank, counts = _token_stage(
        outs, lses, ob.reshape(T, -1), x2, w_out.astype(BF16), expand, row(ln1_g), row(ln1_b),
        kv, wq_mem.astype(BF16), wo_mem.astype(BF16), row(ln2_g), row(ln2_b),
        wr_hi, wr_lo, e_bias.reshape(-1, 1).astype(F32), tm, S // tm)

    tile_rows = 512
    counts = counts.reshape(-1)
    padded = (counts + tile_rows - 1) // tile_rows * tile_rows
    pend = jnp.cumsum(padded)
    pstart = pend - padded
    n_tiles = (T * TOP_K) // tile_rows + N_EXPERTS
    n_rows = n_tiles * tile_rows
    tile_lo = jnp.arange(n_tiles, dtype=jnp.int32) * tile_rows
    tile_e = jnp.minimum(jnp.sum(pend[None, :] <= tile_lo[:, None], axis=1), N_EXPERTS - 1).astype(jnp.int32)
    tile_valid = jnp.clip(counts[tile_e] - (tile_lo - pstart[tile_e]), 0, tile_rows).astype(jnp.int32)
    n_used = (pend[-1] // tile_rows).astype(jnp.int32).reshape(1)

    dest = _dest_rows(idx, rank, pstart.reshape(-1, 1).astype(jnp.int32), min(2048, T))
    xs = _dispatch(dest, h2_slabs, n_rows, min(512, S), _slab_rows(D))
    ys = _experts(tile_e, tile_valid, n_used, xs, w_gate, w_up, w_down, tile_rows)
    out = _combine(dest, w_route.T, h2, ys, ws_gate.astype(BF16), ws_up.astype(BF16), ws_down.astype(BF16),
                   row(ln3_g), row(ln3_b), min(512, S))
    return out.reshape(B, S, D)


def kernel(x, mem, positions, w_in, w_out, lambda_q1, lambda_k1, lambda_q2, lambda_k2, subln_g, ln1_g, ln1_b,
           wq_mem, wkv_mem, wo_mem, ln2_g, ln2_b, w_router, e_bias, w_gate, w_up, w_down, ws_gate, ws_up, ws_down,
           ln3_g, ln3_b):
    cc, s1, s2 = _rope_lane_tables(positions)
    h = x
    for l in range(w_in.shape[0]):
        h = _layer(h, mem, cc, s1, s2, l, w_in[l], w_out[l], lambda_q1[l], lambda_k1[l], lambda_q2[l],
                   lambda_k2[l], subln_g[l], ln1_g[l], ln1_b[l], wq_mem[l], wkv_mem[l], wo_mem[l],
                   ln2_g[l], ln2_b[l], w_router[l], e_bias[l], w_gate[l], w_up[l], w_down[l],
                   ws_gate[l], ws_up[l], ws_down[l], ln3_g[l], ln3_b[l])
    return h
```

```python
import functools
import math

import jax
import jax.numpy as jnp
from jax import lax
from jax.experimental import pallas as pl
from jax.experimental.pallas import tpu as pltpu

A_HEADS = 8
A_HEAD_DIM = 64
A_PAIRS = ((128, 1), (512, 4), (2048, 16))
B_HEADS = 4
B_QK_DIM = 64
B_V_DIM = 2 * B_QK_DIM
A_WIDTH = A_HEADS * A_HEAD_DIM
B_QK_WIDTH = B_HEADS * 2 * B_QK_DIM
B_V_WIDTH = B_HEADS * B_V_DIM
ROPE_THETA = 500000.0
ROT_DIM = A_HEAD_DIM // 4
M_HEADS = 4
N_EXPERTS = 256
TOP_K = 8
N_GROUPS = 8
TOPK_GROUPS = 4
ROUTED_SCALE = 2.5
LN_EPS = 1e-5
SUBLN_EPS = 1e-5
DEPTH = 1
DN_ALPHA = (2 * DEPTH) ** 0.25

LANES = 128
BAND = 128
ONES_ROWS = 16
DIAG_KEYS = 512
COMBINE_GROUP = 32
VMEM_LIMIT = 56 * 1024 * 1024

BF16 = jnp.bfloat16
F32 = jnp.float32
NEG_INF = float("-inf")


def _cparams(sem):
    return pltpu.CompilerParams(dimension_semantics=sem, vmem_limit_bytes=VMEM_LIMIT)


def _dot(a, b):
    return jnp.dot(a, b, preferred_element_type=F32)


def _dot_nt(a, b):
    return lax.dot_general(a, b, (((1,), (1,)), ((), ())), preferred_element_type=F32)


def _dot_tn(a, b):
    return lax.dot_general(a, b, (((0,), (0,)), ((), ())), preferred_element_type=F32)


def _layer_norm(x, g, b):
    mu = jnp.mean(x, axis=-1, keepdims=True)
    xc = x - mu
    var = jnp.mean(xc * xc, axis=-1, keepdims=True)
    return xc * lax.rsqrt(var + LN_EPS) * g + b


def _slab_rows(d):
    return d // LANES


def _store_slabs(ref, x):
    n, d = x.shape
    s = _slab_rows(d)
    for c in range(s):
        ref[pl.ds(c, n, stride=s), :] = x[:, c * LANES:(c + 1) * LANES]


def _load_slabs(ref, n, d):
    s = _slab_rows(d)
    return jnp.concatenate([ref[pl.ds(c, n, stride=s), :] for c in range(s)], axis=1)


def _to_strided(nat_ref, t, sub_refs, dils):
    tm, w = t.shape
    chunks = range(w // LANES)
    for c in chunks:
        nat_ref[c] = t[:, c * LANES:(c + 1) * LANES]
    for d, ref in zip(dils, sub_refs):
        for r in range(d):
            rows = [nat_ref[c, pl.ds(r, tm // d, stride=d), :] for c in chunks]
            ref[0, r] = jnp.concatenate(rows, axis=1).astype(ref.dtype)


def _from_strided(nat_ref, sub_ref, d):
    n, w = sub_ref.shape[2], sub_ref.shape[3]
    chunks = range(w // LANES)
    for r in range(d):
        for c in chunks:
            nat_ref[c, pl.ds(r, n, stride=d), :] = sub_ref[0, r, :, c * LANES:(c + 1) * LANES].astype(F32)
    return jnp.concatenate([nat_ref[c] for c in chunks], axis=1)


def _in_proj_kernel(x_ref, w_ref, wvt_ref, c_ref, s1_ref, s2_ref, qa_ref, ka_ref, va_ref, qb_ref, kb_ref, vbt_ref,
                    *rest, dils):
    sub_q, sub_k, sub_v, nat_ref = rest[0:-1:3], rest[1:-1:3], rest[2:-1:3], rest[-1]
    x = x_ref[...].astype(BF16)
    cc, s1, s2 = c_ref[...], s1_ref[...], s2_ref[...]

    def rope(t):
        outs = []
        for c in range(t.shape[1] // LANES):
            tc = t[:, c * LANES:(c + 1) * LANES]
            nxt = pltpu.roll(tc, LANES - ROT_DIM // 2, 1)
            prv = pltpu.roll(tc, ROT_DIM // 2, 1)
            outs.append(tc * cc + nxt * s1 + prv * s2)
        return jnp.concatenate(outs, axis=1)

    def proj(lo, width):
        return _dot(x, w_ref[:, lo:lo + width])

    o = 0
    qa = rope(proj(o, A_WIDTH)) * (A_HEAD_DIM ** -0.5); o += A_WIDTH
    qa_ref[...] = qa.astype(BF16)
    _to_strided(nat_ref, qa, sub_q, dils)
    ka = rope(proj(o, A_WIDTH)); o += A_WIDTH
    ka_ref[...] = ka.astype(BF16)
    _to_strided(nat_ref, ka, sub_k, dils)
    va = proj(o, A_WIDTH); o += A_WIDTH
    va_ref[...] = va.astype(BF16)
    _to_strided(nat_ref, va, sub_v, dils)
    qb_ref[...] = (rope(proj(o, B_QK_WIDTH)) * (B_QK_DIM ** -0.5)).astype(BF16); o += B_QK_WIDTH
    kb_ref[...] = rope(proj(o, B_QK_WIDTH)).astype(BF16)
    vbt_ref[0] = _dot_nt(wvt_ref[...], x).astype(BF16)


def _in_proj(x2, w_qk, w_vt, cc, s1, s2, tm, seq, dils):
    T, D = x2.shape
    nt = seq // tm
    nb = T // seq
    row = lambda i: (i, 0)
    const = lambda i: (0, 0)
    tcol = lambda i: (i // nt, 0, i % nt)
    outs = [jax.ShapeDtypeStruct((T, A_WIDTH), BF16)] * 3 + [jax.ShapeDtypeStruct((T, B_QK_WIDTH), BF16)] * 2
    outs_t = [jax.ShapeDtypeStruct((nb, B_V_WIDTH, seq), BF16)]
    outs_s = [jax.ShapeDtypeStruct((nb, d, seq // d, A_WIDTH), BF16) for d in dils for _ in range(3)]
    return pl.pallas_call(
        functools.partial(_in_proj_kernel, dils=dils),
        grid=(T // tm,),
        in_specs=[pl.BlockSpec((tm, D), row), pl.BlockSpec(w_qk.shape, const), pl.BlockSpec(w_vt.shape, const),
                  pl.BlockSpec((tm, LANES), row), pl.BlockSpec((tm, LANES), row), pl.BlockSpec((tm, LANES), row)],
        out_specs=[pl.BlockSpec((tm, o.shape[1]), row) for o in outs] + [
            pl.BlockSpec((1, o.shape[1], tm), tcol) for o in outs_t] + [
            pl.BlockSpec((1, o.shape[1], tm // o.shape[1], A_WIDTH), lambda i: (i // nt, 0, i % nt, 0)) for o in outs_s],
        out_shape=outs + outs_t + outs_s,
        scratch_shapes=[pltpu.VMEM((A_WIDTH // LANES, tm, LANES), F32)],
        compiler_params=_cparams(("parallel",)),
        name="in_proj",
    )(x2, w_qk, w_vt, cc, s1, s2)


def _dilated_kernel(q_ref, kp_ref, kc_ref, vp_ref, vc_ref, o_ref, lse_ref):
    n = pl.program_id(1)
    tq = q_ref.shape[1]
    nk = 2 * BAND
    two = LANES // A_HEAD_DIM
    kr = lax.broadcasted_iota(jnp.int32, (nk, two * BAND), 0)
    qi = lax.broadcasted_iota(jnp.int32, (nk, two * BAND), 1) % BAND
    ok_inner = (kr >= qi) & (kr <= qi + BAND)
    ok_first = (kr >= jnp.where(n > 0, qi, BAND)) & (kr <= qi + BAND)
    lane = lax.broadcasted_iota(jnp.int32, (BAND, LANES), 1)
    row = lax.broadcasted_iota(jnp.int32, (LANES, BAND), 0)
    chunks = [slice(c * LANES, (c + 1) * LANES) for c in range(A_WIDTH // LANES)]
    blocks = [(sl, b) for sl in chunks for b in range(tq // BAND)]
    scores = []
    for sl, b in blocks:
        q = q_ref[0, b * BAND:(b + 1) * BAND, sl]
        zero = jnp.zeros_like(q)
        q2 = jnp.concatenate([jnp.where((lane >= h * A_HEAD_DIM) & (lane < (h + 1) * A_HEAD_DIM), q, zero)
                              for h in range(two)], axis=0)
        prev = kp_ref[0, :, sl] if b == 0 else kc_ref[0, (b - 1) * BAND:b * BAND, sl]
        kband = jnp.concatenate([prev, kc_ref[0, b * BAND:(b + 1) * BAND, sl]], axis=0)
        scores.append(_dot_nt(kband, q2))
    probs = []
    lses = [[None] * (tq // BAND) for _ in range(A_HEADS)]
    for (sl, b), st in zip(blocks, scores):
        s = jnp.where(ok_first if b == 0 else ok_inner, st, NEG_INF)
        m = jnp.max(s, axis=0, keepdims=True)
        p = jnp.exp(s - m)
        den = jnp.sum(p, axis=0, keepdims=True)
        probs.append((p.astype(BF16), den))
        lse = m + jnp.log(den)
        for h in range(two):
            lses[sl.start // A_HEAD_DIM + h][b] = lse[:, h * BAND:(h + 1) * BAND]
    for (sl, b), (p, den) in zip(blocks, probs):
        prev = vp_ref[0, :, sl] if b == 0 else vc_ref[0, (b - 1) * BAND:b * BAND, sl]
        vband = jnp.concatenate([prev, vc_ref[0, b * BAND:(b + 1) * BAND, sl]], axis=0)
        ot = _dot_tn(vband, p) / den
        out_t = ot[:, :BAND]
        for h in range(1, two):
            out_t = jnp.where(row >= h * A_HEAD_DIM, ot[:, h * BAND:(h + 1) * BAND], out_t)
        o_ref[0, b * BAND:(b + 1) * BAND, sl] = out_t.T.astype(BF16)
    stats = [jnp.concatenate(per_head, axis=1) for per_head in lses]
    stats.append(jnp.zeros((LANES - A_HEADS, tq), F32))
    lse_ref[0] = jnp.concatenate(stats, axis=0).T


def _dilated(q, k, v, tq):
    G, L, W = q.shape
    per = tq // BAND
    cur = lambda g, n: (g, n, 0)
    prev = lambda g, n: (g, jnp.maximum(n * per - 1, 0), 0)
    return pl.pallas_call(
        _dilated_kernel,
        grid=(G, L // tq),
        in_specs=[pl.BlockSpec((1, tq, W), cur), pl.BlockSpec((1, BAND, W), prev), pl.BlockSpec((1, tq, W), cur),
                  pl.BlockSpec((1, BAND, W), prev), pl.BlockSpec((1, tq, W), cur)],
        out_specs=[pl.BlockSpec((1, tq, W), cur), pl.BlockSpec((1, tq, LANES), cur)],
        out_shape=[jax.ShapeDtypeStruct((G, L, W), BF16), jax.ShapeDtypeStruct((G, L, LANES), F32)],
        compiler_params=_cparams(("parallel", "parallel")),
        name="dilated",
    )(q, k, k, v, v)


def _diff_kernel(lq1_ref, lk1_ref, lq2_ref, lk2_ref, g_ref, q_ref, k_ref, vt_ref, o_ref, acc_ref, *, tq, tk, lambda_init):
    iq = pl.program_id(2)
    q = q_ref[0]
    lane = lax.broadcasted_iota(jnp.int32, q.shape, 1)
    zero = jnp.zeros_like(q)
    qs = (jnp.where(lane < B_QK_DIM, q, zero), jnp.where(lane >= B_QK_DIM, q, zero))
    q2 = jnp.concatenate(qs, axis=0)
    acc_ref[...] = jnp.zeros_like(acc_ref)

    def block(start, nk, carry, lo=None):
        kb = k_ref[0, pl.ds(start, nk), :]
        vt = jnp.concatenate([vt_ref[0, :, pl.ds(start, nk)], jnp.ones((ONES_ROWS, nk), BF16)], axis=0)
        q0 = 0 if lo is None else lo
        nq = tq - q0
        st = _dot_nt(kb, q2 if q0 == 0 else jnp.concatenate([x[q0:] for x in qs], axis=0))
        if lo is not None:
            ok = lax.broadcasted_iota(jnp.int32, (nk, nq), 0) <= lax.broadcasted_iota(jnp.int32, (nk, nq), 1)
        out, work = [], []
        for c in range(2):
            m = carry[c]
            s = st[:, c * nq:(c + 1) * nq]
            if lo is not None:
                s = jnp.where(ok, s, NEG_INF)
            m_new = jnp.maximum(m[:, q0:], jnp.max(s, axis=0, keepdims=True))
            out.append(m_new if q0 == 0 else jnp.concatenate([m[:, :q0], m_new], axis=1))
            work.append((jnp.exp(m[:, q0:] - m_new), jnp.exp(s - m_new).astype(BF16)))
        for c, (a, p) in enumerate(work):
            acc_ref[c, :, q0:] = a * acc_ref[c, :, q0:] + _dot(vt, p)
        return tuple(out)

    init = tuple(jnp.full((1, tq), NEG_INF, F32) for _ in range(2))
    per_tile = tq // tk
    carry = lax.fori_loop(0, iq * per_tile, lambda j, cr: block(pl.multiple_of(j * tk, tk), tk, cr), init)
    dk = min(DIAG_KEYS, tq)
    for lo in range(0, tq, dk):
        carry = block(pl.multiple_of(iq * tq + lo, dk), dk, carry, lo)

    lam = (jnp.exp(jnp.sum(lq1_ref[...] * lk1_ref[...], axis=1, keepdims=True))
           - jnp.exp(jnp.sum(lq2_ref[...] * lk2_ref[...], axis=1, keepdims=True)) + lambda_init)
    nv = B_V_DIM
    o = (acc_ref[0, :nv, :] / acc_ref[0, nv:nv + 1, :]
         - lam * (acc_ref[1, :nv, :] / acc_ref[1, nv:nv + 1, :]))
    o = o * lax.rsqrt(jnp.mean(o * o, axis=0, keepdims=True) + SUBLN_EPS) * g_ref[...]
    o_ref[0] = (o * (1.0 - lambda_init)).T.astype(BF16)


def _diff_attention(qb, kb, vbt, lq1, lk1, lq2, lk2, subln_g, lambda_init, tq, tk):
    B, S, _ = qb.shape
    vec = lambda n: pl.BlockSpec((1, n), lambda b, h, i: (0, 0))
    return pl.pallas_call(
        functools.partial(_diff_kernel, tq=tq, tk=tk, lambda_init=lambda_init),
        grid=(B, B_HEADS, S // tq),
        in_specs=[vec(B_QK_DIM)] * 4 + [pl.BlockSpec((B_V_DIM, 1), lambda b, h, i: (0, 0)),
                  pl.BlockSpec((1, tq, B_V_DIM), lambda b, h, i: (b, i, h)),
                  pl.BlockSpec((1, S, B_V_DIM), lambda b, h, i: (b, 0, h)),
                  pl.BlockSpec((1, B_V_DIM, S), lambda b, h, i: (b, h, 0))],
        out_specs=pl.BlockSpec((1, tq, B_V_DIM), lambda b, h, i: (b, i, h)),
        out_shape=jax.ShapeDtypeStruct((B, S, B_V_WIDTH), BF16),
        scratch_shapes=[pltpu.VMEM((2, B_V_DIM + ONES_ROWS, tq), F32)],
        compiler_params=_cparams(("parallel", "parallel", "arbitrary")),
        name="diff_attn",
    )(lq1, lk1, lq2, lk2, subln_g, qb, kb, vbt)


def _mix_out_kernel(o1_ref, o2_ref, o3_ref, l1_ref, l2_ref, l3_ref, ob_ref, x_ref, w_ref, e_ref, g_ref, b_ref, h_ref,
                    nat_ref):
    ls = [l_ref[...] if d == 1 else _from_strided(nat_ref, l_ref, d)
          for l_ref, (_, d) in zip((l1_ref, l2_ref, l3_ref), A_PAIRS)]
    mx = jnp.maximum(jnp.maximum(ls[0], ls[1]), ls[2])
    es = [jnp.exp(l - mx) for l in ls]
    den = es[0] + es[1] + es[2]
    e = e_ref[...]
    oa = None
    for ex, o_ref, (_, d) in zip(es, (o1_ref, o2_ref, o3_ref), A_PAIRS):
        wt = ex / den
        hi = wt.astype(BF16)
        lo = (wt - hi.astype(F32)).astype(BF16)
        wexp = _dot(hi, e) + _dot(lo, e)
        o = o_ref[...].astype(F32) if d == 1 else _from_strided(nat_ref, o_ref, d)
        term = wexp * o
        oa = term if oa is None else oa + term
    y = _dot(oa.astype(BF16), w_ref[:A_WIDTH, :]) + _dot(ob_ref[...], w_ref[A_WIDTH:, :])
    h_ref[...] = _layer_norm(DN_ALPHA * x_ref[...] + y, g_ref[...], b_ref[...])


def _matmul_kernel(a_ref, w_ref, o_ref):
    o_ref[...] = _dot(a_ref[...].astype(BF16), w_ref[...]).astype(o_ref.dtype)


def _mem_kv(mem2, wkv, tm):
    R, D = mem2.shape
    N = wkv.shape[1]
    return pl.pallas_call(
        _matmul_kernel,
        grid=(R // tm,),
        in_specs=[pl.BlockSpec((tm, D), lambda i: (i, 0)), pl.BlockSpec((D, N), lambda i: (0, 0))],
        out_specs=pl.BlockSpec((tm, N), lambda i: (i, 0)),
        out_shape=jax.ShapeDtypeStruct((R, N), BF16),
        compiler_params=_cparams(("parallel",)),
        name="mem_kv",
    )(mem2, wkv)


def _mem_attn_kernel(h_ref, kv_ref, wq_ref, wo_ref, g_ref, b_ref, out_ref, slab_ref):
    h = h_ref[...]
    D = h.shape[1]
    hd = D // M_HEADS
    q = (_dot(h.astype(BF16), wq_ref[...]) * (hd ** -0.5)).astype(BF16)
    heads = []
    for i in range(M_HEADS):
        k = kv_ref[:, i * hd:(i + 1) * hd]
        v = kv_ref[:, D + i * hd:D + (i + 1) * hd]
        s = _dot_nt(q[:, i * hd:(i + 1) * hd], k)
        p = jnp.exp(s - jnp.max(s, axis=1, keepdims=True))
        p = p / jnp.sum(p, axis=1, keepdims=True)
        heads.append(_dot(p.astype(BF16), v).astype(BF16))
    o = jnp.concatenate(heads, axis=1)
    h2 = _layer_norm(DN_ALPHA * h + _dot(o, wo_ref[...]), g_ref[...], b_ref[...])
    out_ref[...] = h2
    _store_slabs(slab_ref, h2)


def _argmax_rows(v, row_iota, n_rows):
    m = jnp.max(v, axis=0, keepdims=True)
    idx = jnp.min(jnp.where(v == m, row_iota, n_rows), axis=0, keepdims=True)
    return m, idx


def _router_kernel(h_ref, whi_ref, wlo_ref, bias_ref, idx_ref, w_ref, rank_ref, cnt_ref, carry_ref):
    @pl.when(pl.program_id(0) == 0)
    def _():
        carry_ref[...] = jnp.zeros_like(carry_ref)

    h = h_ref[...]
    tt = h.shape[0]
    xhi = h.astype(BF16)
    xlo = (h - xhi.astype(F32)).astype(BF16)
    whi, wlo = whi_ref[...], wlo_ref[...]
    logits = _dot_nt(whi, xhi) + (_dot_nt(wlo, xhi) + _dot_nt(whi, xlo))
    scores = jax.nn.sigmoid(logits)
    biased = scores + bias_ref[...]

    gsz = N_EXPERTS // N_GROUPS
    giota = lax.broadcasted_iota(jnp.int32, (gsz, tt), 0)
    gscore = []
    for g in range(N_GROUPS):
        blk = biased[g * gsz:(g + 1) * gsz]
        m1, i1 = _argmax_rows(blk, giota, gsz)
        m2 = jnp.max(jnp.where(giota == i1, NEG_INF, blk), axis=0, keepdims=True)
        gscore.append(m1 + m2)
    gscore = jnp.concatenate(gscore, axis=0)
    riota = lax.broadcasted_iota(jnp.int32, (N_GROUPS, tt), 0)
    gsel = jnp.zeros((N_GROUPS, tt), F32)
    for _ in range(TOPK_GROUPS):
        _, gi = _argmax_rows(gscore, riota, N_GROUPS)
        hit = riota == gi
        gsel = jnp.where(hit, 1.0, gsel)
        gscore = jnp.where(hit, NEG_INF, gscore)
    masked = jnp.concatenate(
        [jnp.where(gsel[g:g + 1] > 0.5, biased[g * gsz:(g + 1) * gsz], NEG_INF) for g in range(N_GROUPS)], axis=0)

    eiota = lax.broadcasted_iota(jnp.int32, (N_EXPERTS, tt), 0)
    chosen = jnp.zeros((N_EXPERTS, tt), F32)
    idxs, ws = [], []
    for _ in range(TOP_K):
        _, ei = _argmax_rows(masked, eiota, N_EXPERTS)
        hit = eiota == ei
        idxs.append(ei)
        ws.append(jnp.sum(jnp.where(hit, scores, 0.0), axis=0, keepdims=True))
        chosen = jnp.where(hit, 1.0, chosen)
        masked = jnp.where(hit, NEG_INF, masked)
    idx = jnp.concatenate(idxs, axis=0)
    w = jnp.concatenate(ws, axis=0)
    w = w / jnp.sum(w, axis=0, keepdims=True) * ROUTED_SCALE

    before = (lax.broadcasted_iota(jnp.int32, (tt, tt), 0) < lax.broadcasted_iota(jnp.int32, (tt, tt), 1))
    within = _dot(chosen.astype(BF16), before.astype(BF16))
    erank = within + carry_ref[...]
    ranks = [jnp.sum(jnp.where(eiota == idxs[k], erank, 0.0), axis=0, keepdims=True) for k in range(TOP_K)]
    carry = carry_ref[...] + jnp.sum(chosen, axis=1, keepdims=True)
    carry_ref[...] = carry

    idx_ref[...] = idx
    w_ref[...] = w
    rank_ref[...] = jnp.concatenate(ranks, axis=0).astype(jnp.int32)
    cnt_ref[...] = carry.astype(jnp.int32)


def _token_stage_kernel(o1_ref, o2_ref, o3_ref, l1_ref, l2_ref, l3_ref, ob_ref, x_ref, wout_ref, e_ref, g1_ref, b1_ref,
                        kv_ref, wq_ref, wo_ref, g2_ref, b2_ref, whi_ref, wlo_ref, bias_ref,
                        h2_ref, slab_ref, idx_ref, w_ref, rank_ref, cnt_ref, h1_ref, carry_ref, nat_ref):
    _mix_out_kernel(o1_ref, o2_ref, o3_ref, l1_ref, l2_ref, l3_ref, ob_ref, x_ref, wout_ref, e_ref, g1_ref, b1_ref, h1_ref,
                    nat_ref)
    _mem_attn_kernel(h1_ref, kv_ref, wq_ref, wo_ref, g2_ref, b2_ref, h2_ref, slab_ref)
    _router_kernel(h2_ref, whi_ref, wlo_ref, bias_ref, idx_ref, w_ref, rank_ref, cnt_ref, carry_ref)


def _token_stage(os_, ls_, ob, x2, w_out, expand, g1, b1, kv, wq, wo, g2, b2, whi, wlo, bias, tm, tiles_per_batch):
    T, D = x2.shape
    M = kv.shape[0] // (T // (tm * tiles_per_batch))
    row = lambda i: (i, 0)
    col = lambda i: (0, i)
    const = lambda i: (0, 0)
    full = lambda a: pl.BlockSpec(a.shape, const)

    def o_spec(o):
        if o.ndim == 2:
            return pl.BlockSpec((tm, o.shape[1]), row)
        d = o.shape[1]
        return pl.BlockSpec((1, d, tm // d, o.shape[3]), lambda i: (i // tiles_per_batch, 0, i % tiles_per_batch, 0))

    return pl.pallas_call(
        _token_stage_kernel,
        grid=(T // tm,),
        in_specs=[o_spec(o) for o in os_] + [o_spec(l) for l in ls_] + [
            pl.BlockSpec((tm, B_V_WIDTH), row), pl.BlockSpec((tm, D), row), full(w_out), full(expand), full(g1), full(b1),
            pl.BlockSpec((M, 2 * D), lambda i: (i // tiles_per_batch, 0)), full(wq), full(wo), full(g2), full(b2),
            full(whi), full(wlo), full(bias)],
        out_specs=[pl.BlockSpec((tm, D), row), pl.BlockSpec((tm * _slab_rows(D), LANES), row),
                   pl.BlockSpec((TOP_K, tm), col), pl.BlockSpec((TOP_K, tm), col), pl.BlockSpec((TOP_K, tm), col),
                   pl.BlockSpec((N_EXPERTS, 1), const)],
        out_shape=[jax.ShapeDtypeStruct((T, D), F32), jax.ShapeDtypeStruct((T * _slab_rows(D), LANES), F32),
                   jax.ShapeDtypeStruct((TOP_K, T), jnp.int32), jax.ShapeDtypeStruct((TOP_K, T), F32),
                   jax.ShapeDtypeStruct((TOP_K, T), jnp.int32), jax.ShapeDtypeStruct((N_EXPERTS, 1), jnp.int32)],
        scratch_shapes=[pltpu.VMEM((tm, D), F32), pltpu.VMEM((N_EXPERTS, 1), F32),
                        pltpu.VMEM((A_WIDTH // LANES, tm, LANES), F32)],
        compiler_params=_cparams(("arbitrary",)),
        name="token_stage",
    )(*os_, *ls_, ob, x2, w_out, expand, g1, b1, kv, wq, wo, g2, b2, whi, wlo, bias)


def _dest_kernel(idx_ref, rank_ref, start_ref, dest_ref):
    idx = idx_ref[...]
    tt = idx.shape[1]
    eiota = lax.broadcasted_iota(jnp.int32, (N_EXPERTS, tt), 0)
    start = start_ref[...].astype(F32)
    rows = [jnp.sum(jnp.where(eiota == idx[k:k + 1], start, 0.0), axis=0, keepdims=True) for k in range(TOP_K)]
    dest_ref[...] = jnp.concatenate(rows, axis=0).astype(jnp.int32) + rank_ref[...]


def _dest_rows(idx, rank, start, tt):
    T = idx.shape[1]
    col = lambda i: (0, i)
    return pl.pallas_call(
        _dest_kernel,
        grid=(T // tt,),
        in_specs=[pl.BlockSpec((TOP_K, tt), col), pl.BlockSpec((TOP_K, tt), col),
                  pl.BlockSpec((N_EXPERTS, 1), lambda i: (0, 0))],
        out_specs=pl.BlockSpec((TOP_K, tt), col),
        out_shape=jax.ShapeDtypeStruct((TOP_K, T), jnp.int32),
        compiler_params=_cparams(("parallel",)),
        name="dest_rows",
    )(idx, rank, start)


def _dispatch_kernel(dest_ref, x_ref, xs_ref, sem, *, sr):
    tt = x_ref.shape[0] // sr

    def issue(t, c):
        src = x_ref.at[pl.ds(pl.multiple_of(t * sr, sr), sr)]
        for k in range(TOP_K):
            dst = xs_ref.at[pl.ds(pl.multiple_of(dest_ref[k, t] * sr, sr), sr)]
            pltpu.make_async_copy(src, dst, sem).start(priority=k % 2)
        return c

    lax.fori_loop(0, tt, issue, 0)
    n = TOP_K * tt * sr
    pltpu.make_async_copy(xs_ref.at[pl.ds(0, n)], xs_ref.at[pl.ds(0, n)], sem).wait()


def _dispatch(dest, x_slabs, n_rows, tt, sr):
    T = x_slabs.shape[0] // sr
    return pl.pallas_call(
        functools.partial(_dispatch_kernel, sr=sr),
        grid=(T // tt,),
        in_specs=[pl.BlockSpec((TOP_K, tt), lambda i: (0, i), memory_space=pltpu.SMEM),
                  pl.BlockSpec((tt * sr, LANES), lambda i: (i, 0))],
        out_specs=pl.BlockSpec(memory_space=pl.ANY),
        out_shape=jax.ShapeDtypeStruct((n_rows * sr, LANES), F32),
        scratch_shapes=[pltpu.SemaphoreType.DMA],
        compiler_params=_cparams(("arbitrary",)),
        name="dispatch",
    )(dest, x_slabs)


def _experts_kernel(te_ref, tv_ref, nu_ref, x_ref, wg_ref, wu_ref, wd_ref, y_ref, wg_bf, wu_bf, wd_bf):
    i = pl.program_id(0)

    @pl.when(i < nu_ref[0])
    def _():
        @pl.when((i == 0) | (te_ref[i] != te_ref[jnp.maximum(i - 1, 0)]))
        def _():
            wg_bf[...] = wg_ref[0].astype(BF16)
            wu_bf[...] = wu_ref[0].astype(BF16)
            wd_bf[...] = wd_ref[0].astype(BF16)

        d = wg_bf.shape[0]
        tm = x_ref.shape[0] // _slab_rows(d)
        rows = lax.broadcasted_iota(jnp.int32, (tm, 1), 0)
        x = jnp.where(rows < tv_ref[i], _load_slabs(x_ref, tm, d), 0.0).astype(BF16)
        g = _dot(x, wg_bf[...])
        act = (g * jax.nn.sigmoid(g) * _dot(x, wu_bf[...])).astype(BF16)
        _store_slabs(y_ref, _dot(act, wd_bf[...]))


def _experts(tile_e, tile_valid, n_used, xs, w_gate, w_up, w_down, tm):
    E, D, F = w_gate.shape
    sr = _slab_rows(D)
    n_tiles = xs.shape[0] // (tm * sr)

    def tile(i, te, tv, nu):
        return (jnp.minimum(i, nu[0] - 1), 0)

    def expert(i, te, tv, nu):
        return (te[jnp.minimum(i, nu[0] - 1)], 0, 0)

    return pl.pallas_call(
        _experts_kernel,
        grid_spec=pltpu.PrefetchScalarGridSpec(
            num_scalar_prefetch=3,
            grid=(n_tiles,),
            in_specs=[pl.BlockSpec((tm * sr, LANES), tile), pl.BlockSpec((1, D, F), expert),
                      pl.BlockSpec((1, D, F), expert), pl.BlockSpec((1, F, D), expert)],
            out_specs=pl.BlockSpec((tm * sr, LANES), tile),
            scratch_shapes=[pltpu.VMEM((D, F), BF16), pltpu.VMEM((D, F), BF16), pltpu.VMEM((F, D), BF16)],
        ),
        out_shape=jax.ShapeDtypeStruct(xs.shape, F32),
        compiler_params=_cparams(("arbitrary",)),
        name="experts",
    )(tile_e, tile_valid, n_used, xs, w_gate, w_up, w_down)


def _combine_kernel(dest_ref, nxt_ref, w_ref, h_ref, ys_ref, sg_ref, su_ref, sd_ref, g_ref, b_ref, out_ref,
                    buf, routed_ref, sem):
    i = pl.program_id(0)
    last = pl.num_programs(0) - 1
    tt, d = h_ref.shape
    sr = _slab_rows(d)
    slot = i % 2
    other = 1 - slot

    def start_row(idx_ref, t, to):
        for k in range(TOP_K):
            src = ys_ref.at[pl.ds(pl.multiple_of(idx_ref[k, t] * sr, sr), sr)]
            dst = buf.at[to, k, pl.ds(pl.multiple_of(t * sr, sr), sr)]
            pltpu.make_async_copy(src, dst, sem.at[to]).start(priority=k % 2)

    def wait_slot(s):
        for k in range(TOP_K):
            pltpu.make_async_copy(ys_ref.at[pl.ds(0, tt * sr)], buf.at[s, k], sem.at[s]).wait()

    @pl.when(i == 0)
    def _():
        def prime(t, c):
            start_row(dest_ref, t, 0)
            return c

        lax.fori_loop(0, tt, prime, 0)

    wait_slot(slot)

    def group(gi, carry):
        t0 = pl.multiple_of(gi * COMBINE_GROUP, COMBINE_GROUP)
        for j in range(COMBINE_GROUP):
            start_row(nxt_ref, t0 + j, other)
        w = w_ref[pl.ds(t0, COMBINE_GROUP), :]
        wk = [jnp.broadcast_to(w[:, k:k + 1], (COMBINE_GROUP, LANES)) for k in range(TOP_K)]
        for c in range(sr):
            acc = buf[slot, 0, pl.ds(t0 * sr + c, COMBINE_GROUP, stride=sr), :] * wk[0]
            for k in range(1, TOP_K):
                acc = acc + buf[slot, k, pl.ds(t0 * sr + c, COMBINE_GROUP, stride=sr), :] * wk[k]
            routed_ref[pl.ds(t0, COMBINE_GROUP), c * LANES:(c + 1) * LANES] = acc
        return carry

    lax.fori_loop(0, tt // COMBINE_GROUP, group, 0)

    h = h_ref[...]
    hb = h.astype(BF16)
    gate = _dot(hb, sg_ref[...])
    act = (gate * jax.nn.sigmoid(gate) * _dot(hb, su_ref[...])).astype(BF16)
    shared = _dot(act, sd_ref[...])
    out_ref[...] = _layer_norm(DN_ALPHA * h + (routed_ref[...] + shared), g_ref[...], b_ref[...])

    @pl.when(i == last)
    def _():
        wait_slot(other)


def _combine(dest, w_tok, h2, ys, sg, su, sd, g, b, tt):
    T, D = h2.shape
    sr = _slab_rows(D)
    row = lambda i: (i, 0)
    const = lambda i: (0, 0)
    n = T // tt
    return pl.pallas_call(
        _combine_kernel,
        grid=(n,),
        in_specs=[pl.BlockSpec((TOP_K, tt), lambda i: (0, i), memory_space=pltpu.SMEM),
                  pl.BlockSpec((TOP_K, tt), lambda i: (0, jnp.minimum(i + 1, n - 1)), memory_space=pltpu.SMEM),
                  pl.BlockSpec((tt, TOP_K), row), pl.BlockSpec((tt, D), row),
                  pl.BlockSpec(memory_space=pl.ANY),
                  pl.BlockSpec(sg.shape, const), pl.BlockSpec(su.shape, const), pl.BlockSpec(sd.shape, const),
                  pl.BlockSpec((1, D), const), pl.BlockSpec((1, D), const)],
        out_specs=pl.BlockSpec((tt, D), row),
        out_shape=jax.ShapeDtypeStruct((T, D), F32),
        scratch_shapes=[pltpu.VMEM((2, TOP_K, tt * sr, LANES), F32), pltpu.VMEM((tt, D), F32),
                        pltpu.SemaphoreType.DMA((2,))],
        compiler_params=_cparams(("arbitrary",)),
        name="combine",
    )(dest, dest, w_tok, h2, ys, sg, su, sd, g, b)


def _rope_lane_tables(positions):
    half = ROT_DIM // 2
    inv = ROPE_THETA ** (-jnp.arange(0, ROT_DIM, 2, dtype=F32) / ROT_DIM)
    ang = positions.astype(F32).reshape(-1, 1) * inv
    cos = jnp.tile(jnp.cos(ang), (1, LANES // half))
    sin = jnp.tile(jnp.sin(ang), (1, LANES // half))
    j = jnp.arange(LANES) % A_HEAD_DIM
    cc = jnp.where(j < ROT_DIM, cos, 1.0)
    s1 = jnp.where(j < half, -sin, 0.0)
    s2 = jnp.where((j >= half) & (j < ROT_DIM), sin, 0.0)
    return cc, s1, s2


def _layer(x, mem, cc, s1, s2, l, w_in, w_out, lq1, lk1, lq2, lk2, subln_g, ln1_g, ln1_b,
           wq_mem, wkv_mem, wo_mem, ln2_g, ln2_b, w_router, e_bias, w_gate, w_up, w_down,
           ws_gate, ws_up, ws_down, ln3_g, ln3_b):
    B, S, D = x.shape
    T = B * S
    lambda_init = 0.8 - 0.6 * math.exp(-0.3 * l)
    x2 = x.reshape(T, D)
    tm = min(512, S)

    w_bf = w_in.astype(BF16)
    o_vb = 3 * A_WIDTH + 2 * B_QK_WIDTH
    dils = tuple(d for _, d in A_PAIRS if d > 1)
    qa, ka, va, qb, kb, vbt, *strided = _in_proj(x2, w_bf[:, :o_vb], w_bf[:, o_vb:].T, cc, s1, s2, tm, S, dils)
    strided_qkv = {d: strided[3 * n:3 * n + 3] for n, d in enumerate(dils)}

    outs, lses = [], []
    for window, d in A_PAIRS:
        assert window // d == BAND
        L = S // d
        qkv = (qa, ka, va) if d == 1 else strided_qkv[d]
        o, lse = _dilated(*(t.reshape(B * d, L, A_WIDTH) for t in qkv), min(2 * BAND, L))
        shape = (lambda w: (T, w)) if d == 1 else (lambda w: (B, d, L, w))
        outs.append(o.reshape(shape(A_WIDTH)))
        lses.append(lse.reshape(shape(LANES)))

    row = lambda v: v.reshape(1, -1).astype(F32)
    ob = _diff_attention(qb.reshape(B, S, -1), kb.reshape(B, S, -1), vbt, row(lq1), row(lk1), row(lq2), row(lk2),
                         subln_g.reshape(-1, 1).astype(F32), lambda_init, min(1024, S), min(1024, S))

    expand = (jnp.arange(LANES)[:, None] == (jnp.arange(A_WIDTH)[None, :] // A_HEAD_DIM)).astype(BF16)
    M = mem.shape[1]
    kv = _mem_kv(mem.reshape(B * M, D), wkv_mem.astype(BF16), min(512, B * M))
    wr_t = w_router.T
    wr_hi = wr_t.astype(BF16)
    wr_lo = (wr_t - wr_hi.astype(F32)).astype(BF16)
    h2, h2_slabs, idx, w_route, rank, counts = _token_stage(
        outs, lses, ob.reshape(T, -1), x2, w_out.astype(BF16), expand, row(ln1_g), row(ln1_b),
        kv, wq_mem.astype(BF16), wo_mem.astype(BF16), row(ln2_g), row(ln2_b),
        wr_hi, wr_lo, e_bias.reshape(-1, 1).astype(F32), tm, S // tm)

    tile_rows = 512
    counts = counts.reshape(-1)
    padded = (counts + tile_rows - 1) // tile_rows * tile_rows
    pend = jnp.cumsum(padded)
    pstart = pend - padded
    n_tiles = (T * TOP_K) // tile_rows + N_EXPERTS
    n_rows = n_tiles * tile_rows
    tile_lo = jnp.arange(n_tiles, dtype=jnp.int32) * tile_rows
    tile_e = jnp.minimum(jnp.sum(pend[None, :] <= tile_lo[:, None], axis=1), N_EXPERTS - 1).astype(jnp.int32)
    tile_valid = jnp.clip(counts[tile_e] - (tile_lo - pstart[tile_e]), 0, tile_rows).astype(jnp.int32)
    n_used = (pend[-1] // tile_rows).astype(jnp.int32).reshape(1)

    dest = _dest_rows(idx, rank, pstart.reshape(-1, 1).astype(jnp.int32), min(2048, T))
    xs = _dispatch(dest, h2_slabs, n_rows, min(512, S), _slab_rows(D))
    ys = _experts(tile_e, tile_valid, n_used, xs, w_gate, w_up, w_down, tile_rows)
    out = _combine(dest, w_route.T, h2, ys, ws_gate.astype(BF16), ws_up.astype(BF16), ws_down.astype(BF16),
                   row(ln3_g), row(ln3_b), min(512, S))
    return out.reshape(B, S, D)


def kernel(x, mem, positions, w_in, w_out, lambda_q1, lambda_k1, lambda_q2, lambda_k2, subln_g, ln1_g, ln1_b,
           wq_mem, wkv_mem, wo_mem, ln2_g, ln2_b, w_router, e_bias, w_gate, w_up, w_down, ws_gate, ws_up, ws_down,
           ln3_g, ln3_b):
    cc, s1, s2 = _rope_lane_tables(positions)
    h = x
    for l in range(w_in.shape[0]):
        h = _layer(h, mem, cc, s1, s2, l, w_in[l], w_out[l], lambda_q1[l], lambda_k1[l], lambda_q2[l],
                   lambda_k2[l], subln_g[l], ln1_g[l], ln1_b[l], wq_mem[l], wkv_mem[l], wo_mem[l],
                   ln2_g[l], ln2_b[l], w_router[l], e_bias[l], w_gate[l], w_up[l], w_down[l],
                   ws_gate[l], ws_up[l], ws_down[l], ln3_g[l], ln3_b[l])
    return h
```

```python
import functools
import math

import jax
import jax.numpy as jnp
from jax import lax
from jax.experimental import pallas as pl
from jax.experimental.pallas import tpu as pltpu

A_HEADS = 8
A_HEAD_DIM = 64
A_PAIRS = ((128, 1), (512, 4), (2048, 16))
B_HEADS = 4
B_QK_DIM = 64
B_V_DIM = 2 * B_QK_DIM
A_WIDTH = A_HEADS * A_HEAD_DIM
B_QK_WIDTH = B_HEADS * 2 * B_QK_DIM
B_V_WIDTH = B_HEADS * B_V_DIM
ROPE_THETA = 500000.0
ROT_DIM = A_HEAD_DIM // 4
M_HEADS = 4
N_EXPERTS = 256
TOP_K = 8
N_GROUPS = 8
TOPK_GROUPS = 4
ROUTED_SCALE = 2.5
LN_EPS = 1e-5
SUBLN_EPS = 1e-5
DEPTH = 1
DN_ALPHA = (2 * DEPTH) ** 0.25

LANES = 128
BAND = 128
ONES_ROWS = 16
COMBINE_GROUP = 32
VMEM_LIMIT = 56 * 1024 * 1024

BF16 = jnp.bfloat16
F32 = jnp.float32
NEG_INF = float("-inf")


def _cparams(sem):
    return pltpu.CompilerParams(dimension_semantics=sem, vmem_limit_bytes=VMEM_LIMIT)


def _dot(a, b):
    return jnp.dot(a, b, preferred_element_type=F32)


def _dot_nt(a, b):
    return lax.dot_general(a, b, (((1,), (1,)), ((), ())), preferred_element_type=F32)


def _dot_tn(a, b):
    return lax.dot_general(a, b, (((0,), (0,)), ((), ())), preferred_element_type=F32)


def _layer_norm(x, g, b):
    mu = jnp.mean(x, axis=-1, keepdims=True)
    xc = x - mu
    var = jnp.mean(xc * xc, axis=-1, keepdims=True)
    return xc * lax.rsqrt(var + LN_EPS) * g + b


def _slab_rows(d):
    return d // LANES


def _store_slabs(ref, x):
    n, d = x.shape
    s = _slab_rows(d)
    for c in range(s):
        ref[pl.ds(c, n, stride=s), :] = x[:, c * LANES:(c + 1) * LANES]


def _load_slabs(ref, n, d):
    s = _slab_rows(d)
    return jnp.concatenate([ref[pl.ds(c, n, stride=s), :] for c in range(s)], axis=1)


def _to_strided(nat_ref, t, sub_refs, dils):
    tm, w = t.shape
    chunks = range(w // LANES)
    for c in chunks:
        nat_ref[c] = t[:, c * LANES:(c + 1) * LANES]
    for d, ref in zip(dils, sub_refs):
        for r in range(d):
            rows = [nat_ref[c, pl.ds(r, tm // d, stride=d), :] for c in chunks]
            ref[0, r] = jnp.concatenate(rows, axis=1).astype(ref.dtype)


def _from_strided(nat_ref, sub_ref, d):
    n, w = sub_ref.shape[2], sub_ref.shape[3]
    chunks = range(w // LANES)
    for r in range(d):
        for c in chunks:
            nat_ref[c, pl.ds(r, n, stride=d), :] = sub_ref[0, r, :, c * LANES:(c + 1) * LANES].astype(F32)
    return jnp.concatenate([nat_ref[c] for c in chunks], axis=1)


def _in_proj_kernel(x_ref, w_ref, wvt_ref, c_ref, s1_ref, s2_ref, qa_ref, ka_ref, va_ref, qb_ref, kb_ref, vbt_ref,
                    *rest, dils):
    sub_q, sub_k, sub_v, nat_ref = rest[0:-1:3], rest[1:-1:3], rest[2:-1:3], rest[-1]
    x = x_ref[...].astype(BF16)
    cc, s1, s2 = c_ref[...], s1_ref[...], s2_ref[...]

    def rope(t):
        outs = []
        for c in range(t.shape[1] // LANES):
            tc = t[:, c * LANES:(c + 1) * LANES]
            nxt = pltpu.roll(tc, LANES - ROT_DIM // 2, 1)
            prv = pltpu.roll(tc, ROT_DIM // 2, 1)
            outs.append(tc * cc + nxt * s1 + prv * s2)
        return jnp.concatenate(outs, axis=1)

    def proj(lo, width):
        return _dot(x, w_ref[:, lo:lo + width])

    o = 0
    qa = rope(proj(o, A_WIDTH)) * (A_HEAD_DIM ** -0.5); o += A_WIDTH
    qa_ref[...] = qa.astype(BF16)
    _to_strided(nat_ref, qa, sub_q, dils)
    ka = rope(proj(o, A_WIDTH)); o += A_WIDTH
    ka_ref[...] = ka.astype(BF16)
    _to_strided(nat_ref, ka, sub_k, dils)
    va = proj(o, A_WIDTH); o += A_WIDTH
    va_ref[...] = va.astype(BF16)
    _to_strided(nat_ref, va, sub_v, dils)
    qb_ref[...] = (rope(proj(o, B_QK_WIDTH)) * (B_QK_DIM ** -0.5)).astype(BF16); o += B_QK_WIDTH
    kb_ref[...] = rope(proj(o, B_QK_WIDTH)).astype(BF16)
    vbt_ref[0] = _dot_nt(wvt_ref[...], x).astype(BF16)


def _in_proj(x2, w_qk, w_vt, cc, s1, s2, tm, seq, dils):
    T, D = x2.shape
    nt = seq // tm
    nb = T // seq
    row = lambda i: (i, 0)
    const = lambda i: (0, 0)
    tcol = lambda i: (i // nt, 0, i % nt)
    outs = [jax.ShapeDtypeStruct((T, A_WIDTH), BF16)] * 3 + [jax.ShapeDtypeStruct((T, B_QK_WIDTH), BF16)] * 2
    outs_t = [jax.ShapeDtypeStruct((nb, B_V_WIDTH, seq), BF16)]
    outs_s = [jax.ShapeDtypeStruct((nb, d, seq // d, A_WIDTH), BF16) for d in dils for _ in range(3)]
    return pl.pallas_call(
        functools.partial(_in_proj_kernel, dils=dils),
        grid=(T // tm,),
        in_specs=[pl.BlockSpec((tm, D), row), pl.BlockSpec(w_qk.shape, const), pl.BlockSpec(w_vt.shape, const),
                  pl.BlockSpec((tm, LANES), row), pl.BlockSpec((tm, LANES), row), pl.BlockSpec((tm, LANES), row)],
        out_specs=[pl.BlockSpec((tm, o.shape[1]), row) for o in outs] + [
            pl.BlockSpec((1, o.shape[1], tm), tcol) for o in outs_t] + [
            pl.BlockSpec((1, o.shape[1], tm // o.shape[1], A_WIDTH), lambda i: (i // nt, 0, i % nt, 0)) for o in outs_s],
        out_shape=outs + outs_t + outs_s,
        scratch_shapes=[pltpu.VMEM((A_WIDTH // LANES, tm, LANES), F32)],
        compiler_params=_cparams(("parallel",)),
        name="in_proj",
    )(x2, w_qk, w_vt, cc, s1, s2)


def _dilated_kernel(q_ref, kp_ref, kc_ref, vp_ref, vc_ref, o_ref, lse_ref):
    n = pl.program_id(1)
    tq = q_ref.shape[1]
    nk = 2 * BAND
    two = LANES // A_HEAD_DIM
    kr = lax.broadcasted_iota(jnp.int32, (nk, two * BAND), 0)
    qi = lax.broadcasted_iota(jnp.int32, (nk, two * BAND), 1) % BAND
    ok_inner = (kr >= qi) & (kr <= qi + BAND)
    ok_first = (kr >= jnp.where(n > 0, qi, BAND)) & (kr <= qi + BAND)
    lane = lax.broadcasted_iota(jnp.int32, (BAND, LANES), 1)
    row = lax.broadcasted_iota(jnp.int32, (LANES, BAND), 0)
    chunks = [slice(c * LANES, (c + 1) * LANES) for c in range(A_WIDTH // LANES)]
    blocks = [(sl, b) for sl in chunks for b in range(tq // BAND)]
    scores = []
    for sl, b in blocks:
        q = q_ref[0, b * BAND:(b + 1) * BAND, sl]
        zero = jnp.zeros_like(q)
        q2 = jnp.concatenate([jnp.where((lane >= h * A_HEAD_DIM) & (lane < (h + 1) * A_HEAD_DIM), q, zero)
                              for h in range(two)], axis=0)
        prev = kp_ref[0, :, sl] if b == 0 else kc_ref[0, (b - 1) * BAND:b * BAND, sl]
        kband = jnp.concatenate([prev, kc_ref[0, b * BAND:(b + 1) * BAND, sl]], axis=0)
        scores.append(_dot_nt(kband, q2))
    probs = []
    lses = [[None] * (tq // BAND) for _ in range(A_HEADS)]
    for (sl, b), st in zip(blocks, scores):
        s = jnp.where(ok_first if b == 0 else ok_inner, st, NEG_INF)
        m = jnp.max(s, axis=0, keepdims=True)
        p = jnp.exp(s - m)
        den = jnp.sum(p, axis=0, keepdims=True)
        probs.append((p.astype(BF16), den))
        lse = m + jnp.log(den)
        for h in range(two):
            lses[sl.start // A_HEAD_DIM + h][b] = lse[:, h * BAND:(h + 1) * BAND]
    for (sl, b), (p, den) in zip(blocks, probs):
        prev = vp_ref[0, :, sl] if b == 0 else vc_ref[0, (b - 1) * BAND:b * BAND, sl]
        vband = jnp.concatenate([prev, vc_ref[0, b * BAND:(b + 1) * BAND, sl]], axis=0)
        ot = _dot_tn(vband, p) / den
        out_t = ot[:, :BAND]
        for h in range(1, two):
            out_t = jnp.where(row >= h * A_HEAD_DIM, ot[:, h * BAND:(h + 1) * BAND], out_t)
        o_ref[0, b * BAND:(b + 1) * BAND, sl] = out_t.T.astype(BF16)
    stats = [jnp.concatenate(per_head, axis=1) for per_head in lses]
    stats.append(jnp.zeros((LANES - A_HEADS, tq), F32))
    lse_ref[0] = jnp.concatenate(stats, axis=0).T


def _dilated(q, k, v, tq):
    G, L, W = q.shape
    per = tq // BAND
    cur = lambda g, n: (g, n, 0)
    prev = lambda g, n: (g, jnp.maximum(n * per - 1, 0), 0)
    return pl.pallas_call(
        _dilated_kernel,
        grid=(G, L // tq),
        in_specs=[pl.BlockSpec((1, tq, W), cur), pl.BlockSpec((1, BAND, W), prev), pl.BlockSpec((1, tq, W), cur),
                  pl.BlockSpec((1, BAND, W), prev), pl.BlockSpec((1, tq, W), cur)],
        out_specs=[pl.BlockSpec((1, tq, W), cur), pl.BlockSpec((1, tq, LANES), cur)],
        out_shape=[jax.ShapeDtypeStruct((G, L, W), BF16), jax.ShapeDtypeStruct((G, L, LANES), F32)],
        compiler_params=_cparams(("parallel", "parallel")),
        name="dilated",
    )(q, k, k, v, v)


def _diff_kernel(lq1_ref, lk1_ref, lq2_ref, lk2_ref, g_ref, q_ref, k_ref, vt_ref, o_ref, acc_ref, *, tq, tk, lambda_init):
    iq = pl.program_id(2)
    q = q_ref[0]
    lane = lax.broadcasted_iota(jnp.int32, q.shape, 1)
    zero = jnp.zeros_like(q)
    q2 = jnp.concatenate([jnp.where(lane < B_QK_DIM, q, zero), jnp.where(lane >= B_QK_DIM, q, zero)], axis=0)
    acc_ref[...] = jnp.zeros_like(acc_ref)

    def block(j, carry, diag):
        start = pl.multiple_of(j * tk, tk)
        kb = k_ref[0, pl.ds(start, tk), :]
        vt = jnp.concatenate([vt_ref[0, :, pl.ds(start, tk)], jnp.ones((ONES_ROWS, tk), BF16)], axis=0)
        st = _dot_nt(kb, q2)
        if diag is not None:
            ki = lax.broadcasted_iota(jnp.int32, (tk, tq), 0) + diag
            qi = lax.broadcasted_iota(jnp.int32, (tk, tq), 1)
            ok = ki <= qi
        out, work = [], []
        for c in range(2):
            m = carry[c]
            s = st[:, c * tq:(c + 1) * tq]
            if diag is not None:
                s = jnp.where(ok, s, NEG_INF)
            m_new = jnp.maximum(m, jnp.max(s, axis=0, keepdims=True))
            out.append(m_new)
            work.append((jnp.exp(m - m_new), jnp.exp(s - m_new).astype(BF16)))
        for c, (a, p) in enumerate(work):
            acc_ref[c] = a * acc_ref[c] + _dot(vt, p)
        return tuple(out)

    init = tuple(jnp.full((1, tq), NEG_INF, F32) for _ in range(2))
    per_tile = tq // tk
    carry = lax.fori_loop(0, iq * per_tile, lambda j, cr: block(j, cr, None), init)
    for r in range(per_tile):
        carry = block(iq * per_tile + r, carry, r * tk)

    lam = (jnp.exp(jnp.sum(lq1_ref[...] * lk1_ref[...], axis=1, keepdims=True))
           - jnp.exp(jnp.sum(lq2_ref[...] * lk2_ref[...], axis=1, keepdims=True)) + lambda_init)
    nv = B_V_DIM
    o = (acc_ref[0, :nv, :] / acc_ref[0, nv:nv + 1, :]
         - lam * (acc_ref[1, :nv, :] / acc_ref[1, nv:nv + 1, :]))
    o = o * lax.rsqrt(jnp.mean(o * o, axis=0, keepdims=True) + SUBLN_EPS) * g_ref[...]
    o_ref[0] = (o * (1.0 - lambda_init)).T.astype(BF16)


def _diff_attention(qb, kb, vbt, lq1, lk1, lq2, lk2, subln_g, lambda_init, tq, tk):
    B, S, _ = qb.shape
    vec = lambda n: pl.BlockSpec((1, n), lambda b, h, i: (0, 0))
    return pl.pallas_call(
        functools.partial(_diff_kernel, tq=tq, tk=tk, lambda_init=lambda_init),
        grid=(B, B_HEADS, S // tq),
        in_specs=[vec(B_QK_DIM)] * 4 + [pl.BlockSpec((B_V_DIM, 1), lambda b, h, i: (0, 0)),
                  pl.BlockSpec((1, tq, B_V_DIM), lambda b, h, i: (b, i, h)),
                  pl.BlockSpec((1, S, B_V_DIM), lambda b, h, i: (b, 0, h)),
                  pl.BlockSpec((1, B_V_DIM, S), lambda b, h, i: (b, h, 0))],
        out_specs=pl.BlockSpec((1, tq, B_V_DIM), lambda b, h, i: (b, i, h)),
        out_shape=jax.ShapeDtypeStruct((B, S, B_V_WIDTH), BF16),
        scratch_shapes=[pltpu.VMEM((2, B_V_DIM + ONES_ROWS, tq), F32)],
        compiler_params=_cparams(("parallel", "parallel", "arbitrary")),
        name="diff_attn",
    )(lq1, lk1, lq2, lk2, subln_g, qb, kb, vbt)


def _mix_out_kernel(o1_ref, o2_ref, o3_ref, l1_ref, l2_ref, l3_ref, ob_ref, x_ref, w_ref, e_ref, g_ref, b_ref, h_ref,
                    nat_ref):
    ls = [l_ref[...] if d == 1 else _from_strided(nat_ref, l_ref, d)
          for l_ref, (_, d) in zip((l1_ref, l2_ref, l3_ref), A_PAIRS)]
    mx = jnp.maximum(jnp.maximum(ls[0], ls[1]), ls[2])
    es = [jnp.exp(l - mx) for l in ls]
    den = es[0] + es[1] + es[2]
    e = e_ref[...]
    oa = None
    for ex, o_ref, (_, d) in zip(es, (o1_ref, o2_ref, o3_ref), A_PAIRS):
        wt = ex / den
        hi = wt.astype(BF16)
        lo = (wt - hi.astype(F32)).astype(BF16)
        wexp = _dot(hi, e) + _dot(lo, e)
        o = o_ref[...].astype(F32) if d == 1 else _from_strided(nat_ref, o_ref, d)
        term = wexp * o
        oa = term if oa is None else oa + term
    y = _dot(oa.astype(BF16), w_ref[:A_WIDTH, :]) + _dot(ob_ref[...], w_ref[A_WIDTH:, :])
    h_ref[...] = _layer_norm(DN_ALPHA * x_ref[...] + y, g_ref[...], b_ref[...])


def _matmul_kernel(a_ref, w_ref, o_ref):
    o_ref[...] = _dot(a_ref[...].astype(BF16), w_ref[...]).astype(o_ref.dtype)


def _mem_kv(mem2, wkv, tm):
    R, D = mem2.shape
    N = wkv.shape[1]
    return pl.pallas_call(
        _matmul_kernel,
        grid=(R // tm,),
        in_specs=[pl.BlockSpec((tm, D), lambda i: (i, 0)), pl.BlockSpec((D, N), lambda i: (0, 0))],
        out_specs=pl.BlockSpec((tm, N), lambda i: (i, 0)),
        out_shape=jax.ShapeDtypeStruct((R, N), BF16),
        compiler_params=_cparams(("parallel",)),
        name="mem_kv",
    )(mem2, wkv)


def _mem_attn_kernel(h_ref, kv_ref, wq_ref, wo_ref, g_ref, b_ref, out_ref, slab_ref):
    h = h_ref[...]
    D = h.shape[1]
    hd = D // M_HEADS
    q = (_dot(h.astype(BF16), wq_ref[...]) * (hd ** -0.5)).astype(BF16)
    heads = []
    for i in range(M_HEADS):
        k = kv_ref[:, i * hd:(i + 1) * hd]
        v = kv_ref[:, D + i * hd:D + (i + 1) * hd]
        s = _dot_nt(q[:, i * hd:(i + 1) * hd], k)
        p = jnp.exp(s - jnp.max(s, axis=1, keepdims=True))
        p = p / jnp.sum(p, axis=1, keepdims=True)
        heads.append(_dot(p.astype(BF16), v).astype(BF16))
    o = jnp.concatenate(heads, axis=1)
    h2 = _layer_norm(DN_ALPHA * h + _dot(o, wo_ref[...]), g_ref[...], b_ref[...])
    out_ref[...] = h2
    _store_slabs(slab_ref, h2)


def _argmax_rows(v, row_iota, n_rows):
    m = jnp.max(v, axis=0, keepdims=True)
    idx = jnp.min(jnp.where(v == m, row_iota, n_rows), axis=0, keepdims=True)
    return m, idx


def _router_kernel(h_ref, whi_ref, wlo_ref, bias_ref, idx_ref, w_ref, rank_ref, cnt_ref, carry_ref):
    @pl.when(pl.program_id(0) == 0)
    def _():
        carry_ref[...] = jnp.zeros_like(carry_ref)

    h = h_ref[...]
    tt = h.shape[0]
    xhi = h.astype(BF16)
    xlo = (h - xhi.astype(F32)).astype(BF16)
    whi, wlo = whi_ref[...], wlo_ref[...]
    logits = _dot_nt(whi, xhi) + (_dot_nt(wlo, xhi) + _dot_nt(whi, xlo))
    scores = jax.nn.sigmoid(logits)
    biased = scores + bias_ref[...]

    gsz = N_EXPERTS // N_GROUPS
    giota = lax.broadcasted_iota(jnp.int32, (gsz, tt), 0)
    gscore = []
    for g in range(N_GROUPS):
        blk = biased[g * gsz:(g + 1) * gsz]
        m1, i1 = _argmax_rows(blk, giota, gsz)
        m2 = jnp.max(jnp.where(giota == i1, NEG_INF, blk), axis=0, keepdims=True)
        gscore.append(m1 + m2)
    gscore = jnp.concatenate(gscore, axis=0)
    riota = lax.broadcasted_iota(jnp.int32, (N_GROUPS, tt), 0)
    gsel = jnp.zeros((N_GROUPS, tt), F32)
    for _ in range(TOPK_GROUPS):
        _, gi = _argmax_rows(gscore, riota, N_GROUPS)
        hit = riota == gi
        gsel = jnp.where(hit, 1.0, gsel)
        gscore = jnp.where(hit, NEG_INF, gscore)
    masked = jnp.concatenate(
        [jnp.where(gsel[g:g + 1] > 0.5, biased[g * gsz:(g + 1) * gsz], NEG_INF) for g in range(N_GROUPS)], axis=0)

    eiota = lax.broadcasted_iota(jnp.int32, (N_EXPERTS, tt), 0)
    chosen = jnp.zeros((N_EXPERTS, tt), F32)
    idxs, ws = [], []
    for _ in range(TOP_K):
        _, ei = _argmax_rows(masked, eiota, N_EXPERTS)
        hit = eiota == ei
        idxs.append(ei)
        ws.append(jnp.sum(jnp.where(hit, scores, 0.0), axis=0, keepdims=True))
        chosen = jnp.where(hit, 1.0, chosen)
        masked = jnp.where(hit, NEG_INF, masked)
    idx = jnp.concatenate(idxs, axis=0)
    w = jnp.concatenate(ws, axis=0)
    w = w / jnp.sum(w, axis=0, keepdims=True) * ROUTED_SCALE

    before = (lax.broadcasted_iota(jnp.int32, (tt, tt), 0) < lax.broadcasted_iota(jnp.int32, (tt, tt), 1))
    within = _dot(chosen.astype(BF16), before.astype(BF16))
    erank = within + carry_ref[...]
    ranks = [jnp.sum(jnp.where(eiota == idxs[k], erank, 0.0), axis=0, keepdims=True) for k in range(TOP_K)]
    carry = carry_ref[...] + jnp.sum(chosen, axis=1, keepdims=True)
    carry_ref[...] = carry

    idx_ref[...] = idx
    w_ref[...] = w
    rank_ref[...] = jnp.concatenate(ranks, axis=0).astype(jnp.int32)
    cnt_ref[...] = carry.astype(jnp.int32)


def _token_stage_kernel(o1_ref, o2_ref, o3_ref, l1_ref, l2_ref, l3_ref, ob_ref, x_ref, wout_ref, e_ref, g1_ref, b1_ref,
                        kv_ref, wq_ref, wo_ref, g2_ref, b2_ref, whi_ref, wlo_ref, bias_ref,
                        h2_ref, slab_ref, idx_ref, w_ref, rank_ref, cnt_ref, h1_ref, carry_ref, nat_ref):
    _mix_out_kernel(o1_ref, o2_ref, o3_ref, l1_ref, l2_ref, l3_ref, ob_ref, x_ref, wout_ref, e_ref, g1_ref, b1_ref, h1_ref,
                    nat_ref)
    _mem_attn_kernel(h1_ref, kv_ref, wq_ref, wo_ref, g2_ref, b2_ref, h2_ref, slab_ref)
    _router_kernel(h2_ref, whi_ref, wlo_ref, bias_ref, idx_ref, w_ref, rank_ref, cnt_ref, carry_ref)


def _token_stage(os_, ls_, ob, x2, w_out, expand, g1, b1, kv, wq, wo, g2, b2, whi, wlo, bias, tm, tiles_per_batch):
    T, D = x2.shape
    M = kv.shape[0] // (T // (tm * tiles_per_batch))
    row = lambda i: (i, 0)
    col = lambda i: (0, i)
    const = lambda i: (0, 0)
    full = lambda a: pl.BlockSpec(a.shape, const)

    def o_spec(o):
        if o.ndim == 2:
            return pl.BlockSpec((tm, o.shape[1]), row)
        d = o.shape[1]
        return pl.BlockSpec((1, d, tm // d, o.shape[3]), lambda i: (i // tiles_per_batch, 0, i % tiles_per_batch, 0))

    return pl.pallas_call(
        _token_stage_kernel,
        grid=(T // tm,),
        in_specs=[o_spec(o) for o in os_] + [o_spec(l) for l in ls_] + [
            pl.BlockSpec((tm, B_V_WIDTH), row), pl.BlockSpec((tm, D), row), full(w_out), full(expand), full(g1), full(b1),
            pl.BlockSpec((M, 2 * D), lambda i: (i // tiles_per_batch, 0)), full(wq), full(wo), full(g2), full(b2),
            full(whi), full(wlo), full(bias)],
        out_specs=[pl.BlockSpec((tm, D), row), pl.BlockSpec((tm * _slab_rows(D), LANES), row),
                   pl.BlockSpec((TOP_K, tm), col), pl.BlockSpec((TOP_K, tm), col), pl.BlockSpec((TOP_K, tm), col),
                   pl.BlockSpec((N_EXPERTS, 1), const)],
        out_shape=[jax.ShapeDtypeStruct((T, D), F32), jax.ShapeDtypeStruct((T * _slab_rows(D), LANES), F32),
                   jax.ShapeDtypeStruct((TOP_K, T), jnp.int32), jax.ShapeDtypeStruct((TOP_K, T), F32),
                   jax.ShapeDtypeStruct((TOP_K, T), jnp.int32), jax.ShapeDtypeStruct((N_EXPERTS, 1), jnp.int32)],
        scratch_shapes=[pltpu.VMEM((tm, D), F32), pltpu.VMEM((N_EXPERTS, 1), F32),
                        pltpu.VMEM((A_WIDTH // LANES, tm, LANES), F32)],
        compiler_params=_cparams(("arbitrary",)),
        name="token_stage",
    )(*os_, *ls_, ob, x2, w_out, expand, g1, b1, kv, wq, wo, g2, b2, whi, wlo, bias)


def _dest_kernel(idx_ref, rank_ref, cstart_ref, pstart_ref, cdest_ref, pdest_ref):
    idx = idx_ref[...]
    tt = idx.shape[1]
    eiota = lax.broadcasted_iota(jnp.int32, (N_EXPERTS, tt), 0)
    hits = [eiota == idx[k:k + 1] for k in range(TOP_K)]
    for start_ref, dest_ref in ((cstart_ref, cdest_ref), (pstart_ref, pdest_ref)):
        start = start_ref[...].astype(F32)
        rows = [jnp.sum(jnp.where(hit, start, 0.0), axis=0, keepdims=True) for hit in hits]
        dest_ref[...] = jnp.concatenate(rows, axis=0).astype(jnp.int32) + rank_ref[...]


def _dest_rows(idx, rank, cstart, pstart, tt):
    T = idx.shape[1]
    col = lambda i: (0, i)
    const = lambda i: (0, 0)
    return pl.pallas_call(
        _dest_kernel,
        grid=(T // tt,),
        in_specs=[pl.BlockSpec((TOP_K, tt), col), pl.BlockSpec((TOP_K, tt), col),
                  pl.BlockSpec((N_EXPERTS, 1), const), pl.BlockSpec((N_EXPERTS, 1), const)],
        out_specs=[pl.BlockSpec((TOP_K, tt), col)] * 2,
        out_shape=[jax.ShapeDtypeStruct((TOP_K, T), jnp.int32)] * 2,
        compiler_params=_cparams(("parallel",)),
        name="dest_rows",
    )(idx, rank, cstart, pstart)


def _dispatch_kernel(dest_ref, x_ref, xs_ref, zeros_ref, sem, *, sr):
    tt = x_ref.shape[0] // sr

    @pl.when(pl.program_id(0) == 0)
    def _():
        zeros_ref[...] = jnp.zeros_like(zeros_ref)
        tail = xs_ref.at[pl.ds(xs_ref.shape[0] - zeros_ref.shape[0], zeros_ref.shape[0])]
        pltpu.make_async_copy(zeros_ref, tail, sem).start()
        pltpu.make_async_copy(zeros_ref, tail, sem).wait()

    def issue(t, c):
        src = x_ref.at[pl.ds(pl.multiple_of(t * sr, sr), sr)]
        for k in range(TOP_K):
            dst = xs_ref.at[pl.ds(pl.multiple_of(dest_ref[k, t] * sr, sr), sr)]
            pltpu.make_async_copy(src, dst, sem).start(priority=k % 2)
        return c

    lax.fori_loop(0, tt, issue, 0)
    n = TOP_K * tt * sr
    pltpu.make_async_copy(xs_ref.at[pl.ds(0, n)], xs_ref.at[pl.ds(0, n)], sem).wait()


def _dispatch(dest, x_slabs, slack, tt, sr):
    T = x_slabs.shape[0] // sr
    return pl.pallas_call(
        functools.partial(_dispatch_kernel, sr=sr),
        grid=(T // tt,),
        in_specs=[pl.BlockSpec((TOP_K, tt), lambda i: (0, i), memory_space=pltpu.SMEM),
                  pl.BlockSpec((tt * sr, LANES), lambda i: (i, 0))],
        out_specs=pl.BlockSpec(memory_space=pl.ANY),
        out_shape=jax.ShapeDtypeStruct(((T * TOP_K + slack) * sr, LANES), F32),
        scratch_shapes=[pltpu.VMEM((slack * sr, LANES), F32), pltpu.SemaphoreType.DMA],
        compiler_params=_cparams(("arbitrary",)),
        name="dispatch",
    )(dest, x_slabs)


def _experts_kernel(te_ref, tv_ref, ts_ref, nu_ref, xs_ref, wg_ref, wu_ref, wd_ref, y_ref,
                    xbuf, wg_bf, wu_bf, wd_bf, sem):
    i = pl.program_id(0)
    nu = nu_ref[0]
    d = wg_bf.shape[0]
    sr = _slab_rows(d)
    tm = xbuf.shape[1] // sr

    def fetch(j, slot):
        src = xs_ref.at[pl.ds(pl.multiple_of(ts_ref[j] * sr, sr), tm * sr)]
        return pltpu.make_async_copy(src, xbuf.at[slot], sem.at[slot])

    @pl.when(i == 0)
    def _():
        fetch(0, 0).start()

    @pl.when(i + 1 < nu)
    def _():
        fetch(i + 1, (i + 1) % 2).start()

    @pl.when(i < nu)
    def _():
        @pl.when((i == 0) | (te_ref[i] != te_ref[jnp.maximum(i - 1, 0)]))
        def _():
            wg_bf[...] = wg_ref[0].astype(BF16)
            wu_bf[...] = wu_ref[0].astype(BF16)
            wd_bf[...] = wd_ref[0].astype(BF16)

        slot = i % 2
        fetch(i, slot).wait()
        rows = lax.broadcasted_iota(jnp.int32, (tm, 1), 0)
        x = jnp.where(rows < tv_ref[i], _load_slabs(xbuf.at[slot], tm, d), 0.0).astype(BF16)
        g = _dot(x, wg_bf[...])
        act = (g * jax.nn.sigmoid(g) * _dot(x, wu_bf[...])).astype(BF16)
        _store_slabs(y_ref, _dot(act, wd_bf[...]))

    @pl.when(i >= nu)
    def _():
        y_ref[...] = jnp.zeros_like(y_ref)


def _experts(tile_e, tile_valid, tile_src, n_used, xs, w_gate, w_up, w_down, tm, n_tiles):
    E, D, F = w_gate.shape
    sr = _slab_rows(D)

    def expert(i, te, tv, ts, nu):
        return (te[jnp.minimum(i, nu[0] - 1)], 0, 0)

    return pl.pallas_call(
        _experts_kernel,
        grid_spec=pltpu.PrefetchScalarGridSpec(
            num_scalar_prefetch=4,
            grid=(n_tiles,),
            in_specs=[pl.BlockSpec(memory_space=pl.ANY), pl.BlockSpec((1, D, F), expert),
                      pl.BlockSpec((1, D, F), expert), pl.BlockSpec((1, F, D), expert)],
            out_specs=pl.BlockSpec((tm * sr, LANES), lambda i, te, tv, ts, nu: (i, 0)),
            scratch_shapes=[pltpu.VMEM((2, tm * sr, LANES), F32), pltpu.VMEM((D, F), BF16), pltpu.VMEM((D, F), BF16),
                            pltpu.VMEM((F, D), BF16), pltpu.SemaphoreType.DMA((2,))],
        ),
        out_shape=jax.ShapeDtypeStruct((n_tiles * tm * sr, LANES), F32),
        compiler_params=_cparams(("arbitrary",)),
        name="experts",
    )(tile_e, tile_valid, tile_src, n_used, xs, w_gate, w_up, w_down)


def _combine_kernel(dest_ref, nxt_ref, w_ref, h_ref, ys_ref, sg_ref, su_ref, sd_ref, g_ref, b_ref, out_ref,
                    buf, routed_ref, sem):
    i = pl.program_id(0)
    last = pl.num_programs(0) - 1
    tt, d = h_ref.shape
    sr = _slab_rows(d)
    slot = i % 2
    other = 1 - slot

    def start_row(idx_ref, t, to):
        for k in range(TOP_K):
            src = ys_ref.at[pl.ds(pl.multiple_of(idx_ref[k, t] * sr, sr), sr)]
            dst = buf.at[to, k, pl.ds(pl.multiple_of(t * sr, sr), sr)]
            pltpu.make_async_copy(src, dst, sem.at[to]).start(priority=k % 2)

    def wait_slot(s):
        for k in range(TOP_K):
            pltpu.make_async_copy(ys_ref.at[pl.ds(0, tt * sr)], buf.at[s, k], sem.at[s]).wait()

    @pl.when(i == 0)
    def _():
        def prime(t, c):
            start_row(dest_ref, t, 0)
            return c

        lax.fori_loop(0, tt, prime, 0)

    wait_slot(slot)

    def group(gi, carry):
        t0 = pl.multiple_of(gi * COMBINE_GROUP, COMBINE_GROUP)
        for j in range(COMBINE_GROUP):
            start_row(nxt_ref, t0 + j, other)
        w = w_ref[pl.ds(t0, COMBINE_GROUP), :]
        wk = [jnp.broadcast_to(w[:, k:k + 1], (COMBINE_GROUP, LANES)) for k in range(TOP_K)]
        for c in range(sr):
            acc = buf[slot, 0, pl.ds(t0 * sr + c, COMBINE_GROUP, stride=sr), :] * wk[0]
            for k in range(1, TOP_K):
                acc = acc + buf[slot, k, pl.ds(t0 * sr + c, COMBINE_GROUP, stride=sr), :] * wk[k]
            routed_ref[pl.ds(t0, COMBINE_GROUP), c * LANES:(c + 1) * LANES] = acc
        return carry

    lax.fori_loop(0, tt // COMBINE_GROUP, group, 0)

    h = h_ref[...]
    hb = h.astype(BF16)
    gate = _dot(hb, sg_ref[...])
    act = (gate * jax.nn.sigmoid(gate) * _dot(hb, su_ref[...])).astype(BF16)
    shared = _dot(act, sd_ref[...])
    out_ref[...] = _layer_norm(DN_ALPHA * h + (routed_ref[...] + shared), g_ref[...], b_ref[...])

    @pl.when(i == last)
    def _():
        wait_slot(other)


def _combine(dest, w_tok, h2, ys, sg, su, sd, g, b, tt):
    T, D = h2.shape
    sr = _slab_rows(D)
    row = lambda i: (i, 0)
    const = lambda i: (0, 0)
    n = T // tt
    return pl.pallas_call(
        _combine_kernel,
        grid=(n,),
        in_specs=[pl.BlockSpec((TOP_K, tt), lambda i: (0, i), memory_space=pltpu.SMEM),
                  pl.BlockSpec((TOP_K, tt), lambda i: (0, jnp.minimum(i + 1, n - 1)), memory_space=pltpu.SMEM),
                  pl.BlockSpec((tt, TOP_K), row), pl.BlockSpec((tt, D), row),
                  pl.BlockSpec(memory_space=pl.ANY),
                  pl.BlockSpec(sg.shape, const), pl.BlockSpec(su.shape, const), pl.BlockSpec(sd.shape, const),
                  pl.BlockSpec((1, D), const), pl.BlockSpec((1, D), const)],
        out_specs=pl.BlockSpec((tt, D), row),
        out_shape=jax.ShapeDtypeStruct((T, D), F32),
        scratch_shapes=[pltpu.VMEM((2, TOP_K, tt * sr, LANES), F32), pltpu.VMEM((tt, D), F32),
                        pltpu.SemaphoreType.DMA((2,))],
        compiler_params=_cparams(("arbitrary",)),
        name="combine",
    )(dest, dest, w_tok, h2, ys, sg, su, sd, g, b)


def _rope_lane_tables(positions):
    half = ROT_DIM // 2
    inv = ROPE_THETA ** (-jnp.arange(0, ROT_DIM, 2, dtype=F32) / ROT_DIM)
    ang = positions.astype(F32).reshape(-1, 1) * inv
    cos = jnp.tile(jnp.cos(ang), (1, LANES // half))
    sin = jnp.tile(jnp.sin(ang), (1, LANES // half))
    j = jnp.arange(LANES) % A_HEAD_DIM
    cc = jnp.where(j < ROT_DIM, cos, 1.0)
    s1 = jnp.where(j < half, -sin, 0.0)
    s2 = jnp.where((j >= half) & (j < ROT_DIM), sin, 0.0)
    return cc, s1, s2


def _layer(x, mem, cc, s1, s2, l, w_in, w_out, lq1, lk1, lq2, lk2, subln_g, ln1_g, ln1_b,
           wq_mem, wkv_mem, wo_mem, ln2_g, ln2_b, w_router, e_bias, w_gate, w_up, w_down,
           ws_gate, ws_up, ws_down, ln3_g, ln3_b):
    B, S, D = x.shape
    T = B * S
    lambda_init = 0.8 - 0.6 * math.exp(-0.3 * l)
    x2 = x.reshape(T, D)
    tm = min(512, S)

    w_bf = w_in.astype(BF16)
    o_vb = 3 * A_WIDTH + 2 * B_QK_WIDTH
    dils = tuple(d for _, d in A_PAIRS if d > 1)
    qa, ka, va, qb, kb, vbt, *strided = _in_proj(x2, w_bf[:, :o_vb], w_bf[:, o_vb:].T, cc, s1, s2, tm, S, dils)
    strided_qkv = {d: strided[3 * n:3 * n + 3] for n, d in enumerate(dils)}

    outs, lses = [], []
    for window, d in A_PAIRS:
        assert window // d == BAND
        L = S // d
        qkv = (qa, ka, va) if d == 1 else strided_qkv[d]
        o, lse = _dilated(*(t.reshape(B * d, L, A_WIDTH) for t in qkv), min(2 * BAND, L))
        shape = (lambda w: (T, w)) if d == 1 else (lambda w: (B, d, L, w))
        outs.append(o.reshape(shape(A_WIDTH)))
        lses.append(lse.reshape(shape(LANES)))

    row = lambda v: v.reshape(1, -1).astype(F32)
    ob = _diff_attention(qb.reshape(B, S, -1), kb.reshape(B, S, -1), vbt, row(lq1), row(lk1), row(lq2), row(lk2),
                         subln_g.reshape(-1, 1).astype(F32), lambda_init, min(1024, S), min(1024, S))

    expand = (jnp.arange(LANES)[:, None] == (jnp.arange(A_WIDTH)[None, :] // A_HEAD_DIM)).astype(BF16)
    M = mem.shape[1]
    kv = _mem_kv(mem.reshape(B * M, D), wkv_mem.astype(BF16), min(512, B * M))
    wr_t = w_router.T
    wr_hi = wr_t.astype(BF16)
    wr_lo = (wr_t - wr_hi.astype(F32)).astype(BF16)
    h2, h2_slabs, idx, w_route, rank, counts = _token_stage(
        outs, lses, ob.reshape(T, -1), x2, w_out.astype(BF16), expand, row(ln1_g), row(ln1_b),
        kv, wq_mem.astype(BF16), wo_mem.astype(BF16), row(ln2_g), row(ln2_b),
        wr_hi, wr_lo, e_bias.reshape(-1, 1).astype(F32), tm, S // tm)

    tile_rows = 512
    counts = counts.reshape(-1)
    padded = (counts + tile_rows - 1) // tile_rows * tile_rows
    pend = jnp.cumsum(padded)
    pstart = pend - padded
    n_tiles = (T * TOP_K) // tile_rows + N_EXPERTS
    n_rows = n_tiles * tile_rows
    tile_lo = jnp.arange(n_tiles, dtype=jnp.int32) * tile_rows
    tile_e = jnp.minimum(jnp.sum(pend[None, :] <= tile_lo[:, None], axis=1), N_EXPERTS - 1).astype(jnp.int32)
    tile_off = tile_lo - pstart[tile_e]
    tile_valid = jnp.clip(counts[tile_e] - tile_off, 0, tile_rows).astype(jnp.int32)
    n_used = (pend[-1] // tile_rows).astype(jnp.int32).reshape(1)
    cstart = jnp.cumsum(counts) - counts
    tile_src = jnp.clip(cstart[tile_e] + tile_off, 0, T * TOP_K).astype(jnp.int32)

    col = lambda v: v.reshape(-1, 1).astype(jnp.int32)
    cdest, pdest = _dest_rows(idx, rank, col(cstart), col(pstart), min(2048, T))
    xs = _dispatch(cdest, h2_slabs, tile_rows, min(512, S), _slab_rows(D))
    ys = _experts(tile_e, tile_valid, tile_src, n_used, xs, w_gate, w_up, w_down, tile_rows, n_tiles)
    out = _combine(pdest, w_route.T, h2, ys, ws_gate.astype(BF16), ws_up.astype(BF16), ws_down.astype(BF16),
                   row(ln3_g), row(ln3_b), min(512, S))
    return out.reshape(B, S, D)


def kernel(x, mem, positions, w_in, w_out, lambda_q1, lambda_k1, lambda_q2, lambda_k2, subln_g, ln1_g, ln1_b,
           wq_mem, wkv_mem, wo_mem, ln2_g, ln2_b, w_router, e_bias, w_gate, w_up, w_down, ws_gate, ws_up, ws_down,
           ln3_g, ln3_b):
    cc, s1, s2 = _rope_lane_tables(positions)
    h = x
    for l in range(w_in.shape[0]):
        h = _layer(h, mem, cc, s1, s2, l, w_in[l], w_out[l], lambda_q1[l], lambda_k1[l], lambda_q2[l],
                   lambda_k2[l], subln_g[l], ln1_g[l], ln1_b[l], wq_mem[l], wkv_mem[l], wo_mem[l],
                   ln2_g[l], ln2_b[l], w_router[l], e_bias[l], w_gate[l], w_up[l], w_down[l],
                   ws_gate[l], ws_up[l], ws_down[l], ln3_g[l], ln3_b[l])
    return h
```

```python
import functools
import math

import jax
import jax.numpy as jnp
from jax import lax
from jax.experimental import pallas as pl
from jax.experimental.pallas import tpu as pltpu

A_HEADS = 8
A_HEAD_DIM = 64
A_PAIRS = ((128, 1), (512, 4), (2048, 16))
B_HEADS = 4
B_QK_DIM = 64
B_V_DIM = 2 * B_QK_DIM
A_WIDTH = A_HEADS * A_HEAD_DIM
B_QK_WIDTH = B_HEADS * 2 * B_QK_DIM
B_V_WIDTH = B_HEADS * B_V_DIM
ROPE_THETA = 500000.0
ROT_DIM = A_HEAD_DIM // 4
M_HEADS = 4
N_EXPERTS = 256
TOP_K = 8
N_GROUPS = 8
TOPK_GROUPS = 4
ROUTED_SCALE = 2.5
LN_EPS = 1e-5
SUBLN_EPS = 1e-5
DEPTH = 1
DN_ALPHA = (2 * DEPTH) ** 0.25

LANES = 128
BAND = 128
ONES_ROWS = 16
COMBINE_GROUP = 32
VMEM_LIMIT = 56 * 1024 * 1024

BF16 = jnp.bfloat16
F32 = jnp.float32
NEG_INF = float("-inf")


def _cparams(sem):
    return pltpu.CompilerParams(dimension_semantics=sem, vmem_limit_bytes=VMEM_LIMIT)


def _dot(a, b):
    return jnp.dot(a, b, preferred_element_type=F32)


def _dot_nt(a, b):
    return lax.dot_general(a, b, (((1,), (1,)), ((), ())), preferred_element_type=F32)


def _dot_tn(a, b):
    return lax.dot_general(a, b, (((0,), (0,)), ((), ())), preferred_element_type=F32)


def _layer_norm(x, g, b):
    mu = jnp.mean(x, axis=-1, keepdims=True)
    xc = x - mu
    var = jnp.mean(xc * xc, axis=-1, keepdims=True)
    return xc * lax.rsqrt(var + LN_EPS) * g + b


def _slab_rows(d):
    return d // LANES


def _store_slabs(ref, x):
    n, d = x.shape
    s = _slab_rows(d)
    for c in range(s):
        ref[pl.ds(c, n, stride=s), :] = x[:, c * LANES:(c + 1) * LANES]


def _load_slabs(ref, n, d):
    s = _slab_rows(d)
    return jnp.concatenate([ref[pl.ds(c, n, stride=s), :] for c in range(s)], axis=1)


def _to_strided(nat_ref, t, sub_refs, dils):
    tm, w = t.shape
    chunks = range(w // LANES)
    for c in chunks:
        nat_ref[c] = t[:, c * LANES:(c + 1) * LANES]
    for d, ref in zip(dils, sub_refs):
        for r in range(d):
            rows = [nat_ref[c, pl.ds(r, tm // d, stride=d), :] for c in chunks]
            ref[0, r] = jnp.concatenate(rows, axis=1).astype(ref.dtype)


def _from_strided(nat_ref, sub_ref, d):
    n, w = sub_ref.shape[2], sub_ref.shape[3]
    chunks = range(w // LANES)
    for r in range(d):
        for c in chunks:
            nat_ref[c, pl.ds(r, n, stride=d), :] = sub_ref[0, r, :, c * LANES:(c + 1) * LANES].astype(F32)
    return jnp.concatenate([nat_ref[c] for c in chunks], axis=1)


def _in_proj_kernel(x_ref, w_ref, wvt_ref, c_ref, s1_ref, s2_ref, qa_ref, ka_ref, va_ref, qb_ref, kb_ref, vbt_ref,
                    *rest, dils):
    sub_q, sub_k, sub_v, nat_ref = rest[0:-1:3], rest[1:-1:3], rest[2:-1:3], rest[-1]
    x = x_ref[...].astype(BF16)
    cc, s1, s2 = c_ref[...], s1_ref[...], s2_ref[...]

    def rope(t):
        outs = []
        for c in range(t.shape[1] // LANES):
            tc = t[:, c * LANES:(c + 1) * LANES]
            nxt = pltpu.roll(tc, LANES - ROT_DIM // 2, 1)
            prv = pltpu.roll(tc, ROT_DIM // 2, 1)
            outs.append(tc * cc + nxt * s1 + prv * s2)
        return jnp.concatenate(outs, axis=1)

    def proj(lo, width):
        return _dot(x, w_ref[:, lo:lo + width])

    o = 0
    qa = rope(proj(o, A_WIDTH)) * (A_HEAD_DIM ** -0.5); o += A_WIDTH
    qa_ref[...] = qa.astype(BF16)
    _to_strided(nat_ref, qa, sub_q, dils)
    ka = rope(proj(o, A_WIDTH)); o += A_WIDTH
    ka_ref[...] = ka.astype(BF16)
    _to_strided(nat_ref, ka, sub_k, dils)
    va = proj(o, A_WIDTH); o += A_WIDTH
    va_ref[...] = va.astype(BF16)
    _to_strided(nat_ref, va, sub_v, dils)
    qb_ref[...] = (rope(proj(o, B_QK_WIDTH)) * (B_QK_DIM ** -0.5)).astype(BF16); o += B_QK_WIDTH
    kb_ref[...] = rope(proj(o, B_QK_WIDTH)).astype(BF16)
    vbt_ref[0] = _dot_nt(wvt_ref[...], x).astype(BF16)


def _in_proj(x2, w_qk, w_vt, cc, s1, s2, tm, seq, dils):
    T, D = x2.shape
    nt = seq // tm
    nb = T // seq
    row = lambda i: (i, 0)
    const = lambda i: (0, 0)
    tcol = lambda i: (i // nt, 0, i % nt)
    outs = [jax.ShapeDtypeStruct((T, A_WIDTH), BF16)] * 3 + [jax.ShapeDtypeStruct((T, B_QK_WIDTH), BF16)] * 2
    outs_t = [jax.ShapeDtypeStruct((nb, B_V_WIDTH, seq), BF16)]
    outs_s = [jax.ShapeDtypeStruct((nb, d, seq // d, A_WIDTH), BF16) for d in dils for _ in range(3)]
    return pl.pallas_call(
        functools.partial(_in_proj_kernel, dils=dils),
        grid=(T // tm,),
        in_specs=[pl.BlockSpec((tm, D), row), pl.BlockSpec(w_qk.shape, const), pl.BlockSpec(w_vt.shape, const),
                  pl.BlockSpec((tm, LANES), row), pl.BlockSpec((tm, LANES), row), pl.BlockSpec((tm, LANES), row)],
        out_specs=[pl.BlockSpec((tm, o.shape[1]), row) for o in outs] + [
            pl.BlockSpec((1, o.shape[1], tm), tcol) for o in outs_t] + [
            pl.BlockSpec((1, o.shape[1], tm // o.shape[1], A_WIDTH), lambda i: (i // nt, 0, i % nt, 0)) for o in outs_s],
        out_shape=outs + outs_t + outs_s,
        scratch_shapes=[pltpu.VMEM((A_WIDTH // LANES, tm, LANES), F32)],
        compiler_params=_cparams(("parallel",)),
        name="in_proj",
    )(x2, w_qk, w_vt, cc, s1, s2)


def _dilated_kernel(q_ref, kp_ref, kc_ref, vp_ref, vc_ref, o_ref, lse_ref):
    n = pl.program_id(1)
    tq = q_ref.shape[1]
    nk = 2 * BAND
    two = LANES // A_HEAD_DIM
    kr = lax.broadcasted_iota(jnp.int32, (nk, two * BAND), 0)
    qi = lax.broadcasted_iota(jnp.int32, (nk, two * BAND), 1) % BAND
    ok_inner = (kr >= qi) & (kr <= qi + BAND)
    ok_first = (kr >= jnp.where(n > 0, qi, BAND)) & (kr <= qi + BAND)
    lane = lax.broadcasted_iota(jnp.int32, (BAND, LANES), 1)
    row = lax.broadcasted_iota(jnp.int32, (LANES, BAND), 0)
    chunks = [slice(c * LANES, (c + 1) * LANES) for c in range(A_WIDTH // LANES)]
    blocks = [(sl, b) for sl in chunks for b in range(tq // BAND)]
    scores = []
    for sl, b in blocks:
        q = q_ref[0, b * BAND:(b + 1) * BAND, sl]
        zero = jnp.zeros_like(q)
        q2 = jnp.concatenate([jnp.where((lane >= h * A_HEAD_DIM) & (lane < (h + 1) * A_HEAD_DIM), q, zero)
                              for h in range(two)], axis=0)
        prev = kp_ref[0, :, sl] if b == 0 else kc_ref[0, (b - 1) * BAND:b * BAND, sl]
        kband = jnp.concatenate([prev, kc_ref[0, b * BAND:(b + 1) * BAND, sl]], axis=0)
        scores.append(_dot_nt(kband, q2))
    probs = []
    lses = [[None] * (tq // BAND) for _ in range(A_HEADS)]
    for (sl, b), st in zip(blocks, scores):
        s = jnp.where(ok_first if b == 0 else ok_inner, st, NEG_INF)
        m = jnp.max(s, axis=0, keepdims=True)
        p = jnp.exp(s - m)
        den = jnp.sum(p, axis=0, keepdims=True)
        probs.append((p.astype(BF16), den))
        lse = m + jnp.log(den)
        for h in range(two):
            lses[sl.start // A_HEAD_DIM + h][b] = lse[:, h * BAND:(h + 1) * BAND]
    for (sl, b), (p, den) in zip(blocks, probs):
        prev = vp_ref[0, :, sl] if b == 0 else vc_ref[0, (b - 1) * BAND:b * BAND, sl]
        vband = jnp.concatenate([prev, vc_ref[0, b * BAND:(b + 1) * BAND, sl]], axis=0)
        ot = _dot_tn(vband, p) / den
        out_t = ot[:, :BAND]
        for h in range(1, two):
            out_t = jnp.where(row >= h * A_HEAD_DIM, ot[:, h * BAND:(h + 1) * BAND], out_t)
        o_ref[0, b * BAND:(b + 1) * BAND, sl] = out_t.T.astype(BF16)
    stats = [jnp.concatenate(per_head, axis=1) for per_head in lses]
    stats.append(jnp.zeros((LANES - A_HEADS, tq), F32))
    lse_ref[0] = jnp.concatenate(stats, axis=0).T


def _dilated(q, k, v, tq):
    G, L, W = q.shape
    per = tq // BAND
    cur = lambda g, n: (g, n, 0)
    prev = lambda g, n: (g, jnp.maximum(n * per - 1, 0), 0)
    return pl.pallas_call(
        _dilated_kernel,
        grid=(G, L // tq),
        in_specs=[pl.BlockSpec((1, tq, W), cur), pl.BlockSpec((1, BAND, W), prev), pl.BlockSpec((1, tq, W), cur),
                  pl.BlockSpec((1, BAND, W), prev), pl.BlockSpec((1, tq, W), cur)],
        out_specs=[pl.BlockSpec((1, tq, W), cur), pl.BlockSpec((1, tq, LANES), cur)],
        out_shape=[jax.ShapeDtypeStruct((G, L, W), BF16), jax.ShapeDtypeStruct((G, L, LANES), F32)],
        compiler_params=_cparams(("parallel", "parallel")),
        name="dilated",
    )(q, k, k, v, v)


def _diff_kernel(lq1_ref, lk1_ref, lq2_ref, lk2_ref, g_ref, q_ref, k_ref, vt_ref, o_ref, acc_ref, *, tq, tk, lambda_init):
    iq = pl.program_id(2)
    q = q_ref[0]
    lane = lax.broadcasted_iota(jnp.int32, q.shape, 1)
    zero = jnp.zeros_like(q)
    q2 = jnp.concatenate([jnp.where(lane < B_QK_DIM, q, zero), jnp.where(lane >= B_QK_DIM, q, zero)], axis=0)
    acc_ref[...] = jnp.zeros_like(acc_ref)

    def block(j, carry, diag):
        start = pl.multiple_of(j * tk, tk)
        kb = k_ref[0, pl.ds(start, tk), :]
        vt = jnp.concatenate([vt_ref[0, :, pl.ds(start, tk)], jnp.ones((ONES_ROWS, tk), BF16)], axis=0)
        st = _dot_nt(kb, q2)
        if diag is not None:
            ki = lax.broadcasted_iota(jnp.int32, (tk, tq), 0) + diag
            qi = lax.broadcasted_iota(jnp.int32, (tk, tq), 1)
            ok = ki <= qi
        out, work = [], []
        for c in range(2):
            m = carry[c]
            s = st[:, c * tq:(c + 1) * tq]
            if diag is not None:
                s = jnp.where(ok, s, NEG_INF)
            m_new = jnp.maximum(m, jnp.max(s, axis=0, keepdims=True))
            out.append(m_new)
            work.append((jnp.exp(m - m_new), jnp.exp(s - m_new).astype(BF16)))
        for c, (a, p) in enumerate(work):
            acc_ref[c] = a * acc_ref[c] + _dot(vt, p)
        return tuple(out)

    init = tuple(jnp.full((1, tq), NEG_INF, F32) for _ in range(2))
    per_tile = tq // tk
    carry = lax.fori_loop(0, iq * per_tile, lambda j, cr: block(j, cr, None), init)
    for r in range(per_tile):
        carry = block(iq * per_tile + r, carry, r * tk)

    lam = (jnp.exp(jnp.sum(lq1_ref[...] * lk1_ref[...], axis=1, keepdims=True))
           - jnp.exp(jnp.sum(lq2_ref[...] * lk2_ref[...], axis=1, keepdims=True)) + lambda_init)
    nv = B_V_DIM
    o = (acc_ref[0, :nv, :] / acc_ref[0, nv:nv + 1, :]
         - lam * (acc_ref[1, :nv, :] / acc_ref[1, nv:nv + 1, :]))
    o = o * lax.rsqrt(jnp.mean(o * o, axis=0, keepdims=True) + SUBLN_EPS) * g_ref[...]
    o_ref[0] = (o * (1.0 - lambda_init)).T.astype(BF16)


def _diff_attention(qb, kb, vbt, lq1, lk1, lq2, lk2, subln_g, lambda_init, tq, tk):
    B, S, _ = qb.shape
    vec = lambda n: pl.BlockSpec((1, n), lambda b, h, i: (0, 0))
    return pl.pallas_call(
        functools.partial(_diff_kernel, tq=tq, tk=tk, lambda_init=lambda_init),
        grid=(B, B_HEADS, S // tq),
        in_specs=[vec(B_QK_DIM)] * 4 + [pl.BlockSpec((B_V_DIM, 1), lambda b, h, i: (0, 0)),
                  pl.BlockSpec((1, tq, B_V_DIM), lambda b, h, i: (b, i, h)),
                  pl.BlockSpec((1, S, B_V_DIM), lambda b, h, i: (b, 0, h)),
                  pl.BlockSpec((1, B_V_DIM, S), lambda b, h, i: (b, h, 0))],
        out_specs=pl.BlockSpec((1, tq, B_V_DIM), lambda b, h, i: (b, i, h)),
        out_shape=jax.ShapeDtypeStruct((B, S, B_V_WIDTH), BF16),
        scratch_shapes=[pltpu.VMEM((2, B_V_DIM + ONES_ROWS, tq), F32)],
        compiler_params=_cparams(("parallel", "parallel", "arbitrary")),
        name="diff_attn",
    )(lq1, lk1, lq2, lk2, subln_g, qb, kb, vbt)


def _mix_out_kernel(o1_ref, o2_ref, o3_ref, l1_ref, l2_ref, l3_ref, ob_ref, x_ref, w_ref, e_ref, g_ref, b_ref, h_ref,
                    nat_ref):
    ls = [l_ref[...] if d == 1 else _from_strided(nat_ref, l_ref, d)
          for l_ref, (_, d) in zip((l1_ref, l2_ref, l3_ref), A_PAIRS)]
    mx = jnp.maximum(jnp.maximum(ls[0], ls[1]), ls[2])
    es = [jnp.exp(l - mx) for l in ls]
    den = es[0] + es[1] + es[2]
    e = e_ref[...]
    oa = None
    for ex, o_ref, (_, d) in zip(es, (o1_ref, o2_ref, o3_ref), A_PAIRS):
        wt = ex / den
        hi = wt.astype(BF16)
        lo = (wt - hi.astype(F32)).astype(BF16)
        wexp = _dot(hi, e) + _dot(lo, e)
        o = o_ref[...].astype(F32) if d == 1 else _from_strided(nat_ref, o_ref, d)
        term = wexp * o
        oa = term if oa is None else oa + term
    y = _dot(oa.astype(BF16), w_ref[:A_WIDTH, :]) + _dot(ob_ref[...], w_ref[A_WIDTH:, :])
    h_ref[...] = _layer_norm(DN_ALPHA * x_ref[...] + y, g_ref[...], b_ref[...])


def _matmul_kernel(a_ref, w_ref, o_ref):
    o_ref[...] = _dot(a_ref[...].astype(BF16), w_ref[...]).astype(o_ref.dtype)


def _mem_kv(mem2, wkv, tm):
    R, D = mem2.shape
    N = wkv.shape[1]
    return pl.pallas_call(
        _matmul_kernel,
        grid=(R // tm,),
        in_specs=[pl.BlockSpec((tm, D), lambda i: (i, 0)), pl.BlockSpec((D, N), lambda i: (0, 0))],
        out_specs=pl.BlockSpec((tm, N), lambda i: (i, 0)),
        out_shape=jax.ShapeDtypeStruct((R, N), BF16),
        compiler_params=_cparams(("parallel",)),
        name="mem_kv",
    )(mem2, wkv)


def _mem_attn_kernel(h_ref, kv_ref, wq_ref, wo_ref, g_ref, b_ref, out_ref, slab_ref):
    h = h_ref[...]
    D = h.shape[1]
    hd = D // M_HEADS
    q = (_dot(h.astype(BF16), wq_ref[...]) * (hd ** -0.5)).astype(BF16)
    heads = []
    for i in range(M_HEADS):
        k = kv_ref[:, i * hd:(i + 1) * hd]
        v = kv_ref[:, D + i * hd:D + (i + 1) * hd]
        s = _dot_nt(q[:, i * hd:(i + 1) * hd], k)
        p = jnp.exp(s - jnp.max(s, axis=1, keepdims=True))
        p = p / jnp.sum(p, axis=1, keepdims=True)
        heads.append(_dot(p.astype(BF16), v).astype(BF16))
    o = jnp.concatenate(heads, axis=1)
    h2 = _layer_norm(DN_ALPHA * h + _dot(o, wo_ref[...]), g_ref[...], b_ref[...])
    out_ref[...] = h2
    _store_slabs(slab_ref, h2)


def _argmax_rows(v, row_iota, n_rows):
    m = jnp.max(v, axis=0, keepdims=True)
    idx = jnp.min(jnp.where(v == m, row_iota, n_rows), axis=0, keepdims=True)
    return m, idx


def _router_kernel(h_ref, whi_ref, wlo_ref, bias_ref, idx_ref, w_ref, rank_ref, cnt_ref, carry_ref):
    @pl.when(pl.program_id(0) == 0)
    def _():
        carry_ref[...] = jnp.zeros_like(carry_ref)

    h = h_ref[...]
    tt = h.shape[0]
    xhi = h.astype(BF16)
    xlo = (h - xhi.astype(F32)).astype(BF16)
    whi, wlo = whi_ref[...], wlo_ref[...]
    logits = _dot_nt(whi, xhi) + (_dot_nt(wlo, xhi) + _dot_nt(whi, xlo))
    scores = jax.nn.sigmoid(logits)
    biased = scores + bias_ref[...]

    gsz = N_EXPERTS // N_GROUPS
    giota = lax.broadcasted_iota(jnp.int32, (gsz, tt), 0)
    gscore = []
    for g in range(N_GROUPS):
        blk = biased[g * gsz:(g + 1) * gsz]
        m1, i1 = _argmax_rows(blk, giota, gsz)
        m2 = jnp.max(jnp.where(giota == i1, NEG_INF, blk), axis=0, keepdims=True)
        gscore.append(m1 + m2)
    gscore = jnp.concatenate(gscore, axis=0)
    riota = lax.broadcasted_iota(jnp.int32, (N_GROUPS, tt), 0)
    gsel = jnp.zeros((N_GROUPS, tt), F32)
    for _ in range(TOPK_GROUPS):
        _, gi = _argmax_rows(gscore, riota, N_GROUPS)
        hit = riota == gi
        gsel = jnp.where(hit, 1.0, gsel)
        gscore = jnp.where(hit, NEG_INF, gscore)
    masked = jnp.concatenate(
        [jnp.where(gsel[g:g + 1] > 0.5, biased[g * gsz:(g + 1) * gsz], NEG_INF) for g in range(N_GROUPS)], axis=0)

    eiota = lax.broadcasted_iota(jnp.int32, (N_EXPERTS, tt), 0)
    chosen = jnp.zeros((N_EXPERTS, tt), F32)
    idxs, ws = [], []
    for _ in range(TOP_K):
        _, ei = _argmax_rows(masked, eiota, N_EXPERTS)
        hit = eiota == ei
        idxs.append(ei)
        ws.append(jnp.sum(jnp.where(hit, scores, 0.0), axis=0, keepdims=True))
        chosen = jnp.where(hit, 1.0, chosen)
        masked = jnp.where(hit, NEG_INF, masked)
    idx = jnp.concatenate(idxs, axis=0)
    w = jnp.concatenate(ws, axis=0)
    w = w / jnp.sum(w, axis=0, keepdims=True) * ROUTED_SCALE

    before = (lax.broadcasted_iota(jnp.int32, (tt, tt), 0) < lax.broadcasted_iota(jnp.int32, (tt, tt), 1))
    within = _dot(chosen.astype(BF16), before.astype(BF16))
    erank = within + carry_ref[...]
    ranks = [jnp.sum(jnp.where(eiota == idxs[k], erank, 0.0), axis=0, keepdims=True) for k in range(TOP_K)]
    carry = carry_ref[...] + jnp.sum(chosen, axis=1, keepdims=True)
    carry_ref[...] = carry

    idx_ref[...] = idx
    w_ref[...] = w
    rank_ref[...] = jnp.concatenate(ranks, axis=0).astype(jnp.int32)
    cnt_ref[...] = carry.astype(jnp.int32)


def _token_stage_kernel(o1_ref, o2_ref, o3_ref, l1_ref, l2_ref, l3_ref, ob_ref, x_ref, wout_ref, e_ref, g1_ref, b1_ref,
                        kv_ref, wq_ref, wo_ref, g2_ref, b2_ref, whi_ref, wlo_ref, bias_ref,
                        h2_ref, slab_ref, idx_ref, w_ref, rank_ref, cnt_ref, h1_ref, carry_ref, nat_ref):
    _mix_out_kernel(o1_ref, o2_ref, o3_ref, l1_ref, l2_ref, l3_ref, ob_ref, x_ref, wout_ref, e_ref, g1_ref, b1_ref, h1_ref,
                    nat_ref)
    _mem_attn_kernel(h1_ref, kv_ref, wq_ref, wo_ref, g2_ref, b2_ref, h2_ref, slab_ref)
    _router_kernel(h2_ref, whi_ref, wlo_ref, bias_ref, idx_ref, w_ref, rank_ref, cnt_ref, carry_ref)


def _token_stage(os_, ls_, ob, x2, w_out, expand, g1, b1, kv, wq, wo, g2, b2, whi, wlo, bias, tm, tiles_per_batch):
    T, D = x2.shape
    M = kv.shape[0] // (T // (tm * tiles_per_batch))
    row = lambda i: (i, 0)
    col = lambda i: (0, i)
    const = lambda i: (0, 0)
    full = lambda a: pl.BlockSpec(a.shape, const)

    def o_spec(o):
        if o.ndim == 2:
            return pl.BlockSpec((tm, o.shape[1]), row)
        d = o.shape[1]
        return pl.BlockSpec((1, d, tm // d, o.shape[3]), lambda i: (i // tiles_per_batch, 0, i % tiles_per_batch, 0))

    return pl.pallas_call(
        _token_stage_kernel,
        grid=(T // tm,),
        in_specs=[o_spec(o) for o in os_] + [o_spec(l) for l in ls_] + [
            pl.BlockSpec((tm, B_V_WIDTH), row), pl.BlockSpec((tm, D), row), full(w_out), full(expand), full(g1), full(b1),
            pl.BlockSpec((M, 2 * D), lambda i: (i // tiles_per_batch, 0)), full(wq), full(wo), full(g2), full(b2),
            full(whi), full(wlo), full(bias)],
        out_specs=[pl.BlockSpec((tm, D), row), pl.BlockSpec((tm * _slab_rows(D), LANES), row),
                   pl.BlockSpec((TOP_K, tm), col), pl.BlockSpec((TOP_K, tm), col), pl.BlockSpec((TOP_K, tm), col),
                   pl.BlockSpec((N_EXPERTS, 1), const)],
        out_shape=[jax.ShapeDtypeStruct((T, D), F32), jax.ShapeDtypeStruct((T * _slab_rows(D), LANES), F32),
                   jax.ShapeDtypeStruct((TOP_K, T), jnp.int32), jax.ShapeDtypeStruct((TOP_K, T), F32),
                   jax.ShapeDtypeStruct((TOP_K, T), jnp.int32), jax.ShapeDtypeStruct((N_EXPERTS, 1), jnp.int32)],
        scratch_shapes=[pltpu.VMEM((tm, D), F32), pltpu.VMEM((N_EXPERTS, 1), F32),
                        pltpu.VMEM((A_WIDTH // LANES, tm, LANES), F32)],
        compiler_params=_cparams(("arbitrary",)),
        name="token_stage",
    )(*os_, *ls_, ob, x2, w_out, expand, g1, b1, kv, wq, wo, g2, b2, whi, wlo, bias)


def _dest_kernel(idx_ref, rank_ref, start_ref, dest_ref):
    idx = idx_ref[...]
    tt = idx.shape[1]
    eiota = lax.broadcasted_iota(jnp.int32, (N_EXPERTS, tt), 0)
    start = start_ref[...].astype(F32)
    rows = [jnp.sum(jnp.where(eiota == idx[k:k + 1], start, 0.0), axis=0, keepdims=True) for k in range(TOP_K)]
    dest_ref[...] = jnp.concatenate(rows, axis=0).astype(jnp.int32) + rank_ref[...]


def _dest_rows(idx, rank, start, tt):
    T = idx.shape[1]
    col = lambda i: (0, i)
    return pl.pallas_call(
        _dest_kernel,
        grid=(T // tt,),
        in_specs=[pl.BlockSpec((TOP_K, tt), col), pl.BlockSpec((TOP_K, tt), col),
                  pl.BlockSpec((N_EXPERTS, 1), lambda i: (0, 0))],
        out_specs=pl.BlockSpec((TOP_K, tt), col),
        out_shape=jax.ShapeDtypeStruct((TOP_K, T), jnp.int32),
        compiler_params=_cparams(("parallel",)),
        name="dest_rows",
    )(idx, rank, start)


def _dispatch_kernel(dest_ref, x_ref, xs_ref, zeros_ref, sem, *, sr):
    tt = x_ref.shape[0] // sr

    @pl.when(pl.program_id(0) == 0)
    def _():
        zeros_ref[...] = jnp.zeros_like(zeros_ref)
        tail = xs_ref.at[pl.ds(xs_ref.shape[0] - zeros_ref.shape[0], zeros_ref.shape[0])]
        pltpu.make_async_copy(zeros_ref, tail, sem).start()
        pltpu.make_async_copy(zeros_ref, tail, sem).wait()

    def issue(t, c):
        src = x_ref.at[pl.ds(pl.multiple_of(t * sr, sr), sr)]
        for k in range(TOP_K):
            dst = xs_ref.at[pl.ds(pl.multiple_of(dest_ref[k, t] * sr, sr), sr)]
            pltpu.make_async_copy(src, dst, sem).start(priority=k % 2)
        return c

    lax.fori_loop(0, tt, issue, 0)
    n = TOP_K * tt * sr
    pltpu.make_async_copy(xs_ref.at[pl.ds(0, n)], xs_ref.at[pl.ds(0, n)], sem).wait()


def _dispatch(dest, x_slabs, slack, tt, sr):
    T = x_slabs.shape[0] // sr
    return pl.pallas_call(
        functools.partial(_dispatch_kernel, sr=sr),
        grid=(T // tt,),
        in_specs=[pl.BlockSpec((TOP_K, tt), lambda i: (0, i), memory_space=pltpu.SMEM),
                  pl.BlockSpec((tt * sr, LANES), lambda i: (i, 0))],
        out_specs=pl.BlockSpec(memory_space=pl.ANY),
        out_shape=jax.ShapeDtypeStruct(((T * TOP_K + slack) * sr, LANES), F32),
        scratch_shapes=[pltpu.VMEM((slack * sr, LANES), F32), pltpu.SemaphoreType.DMA],
        compiler_params=_cparams(("arbitrary",)),
        name="dispatch",
    )(dest, x_slabs)


def _experts_kernel(te_ref, tv_ref, ts_ref, nu_ref, xs_ref, wg_ref, wu_ref, wd_ref, ys_ref,
                    xbuf, ybuf, wg_bf, wu_bf, wd_bf, sem, ysem):
    i = pl.program_id(0)
    nu = nu_ref[0]
    d = wg_bf.shape[0]
    sr = _slab_rows(d)
    tm = xbuf.shape[1] // sr

    def fetch(j, slot):
        src = xs_ref.at[pl.ds(pl.multiple_of(ts_ref[j] * sr, sr), tm * sr)]
        return pltpu.make_async_copy(src, xbuf.at[slot], sem.at[slot])

    def put(j, slot):
        dst = ys_ref.at[pl.ds(pl.multiple_of(ts_ref[j] * sr, sr), tm * sr)]
        return pltpu.make_async_copy(ybuf.at[slot], dst, ysem.at[slot])

    @pl.when(i == 0)
    def _():
        fetch(0, 0).start()
        ybuf[1] = jnp.zeros_like(ybuf[1])
        tail = ys_ref.at[pl.ds(ys_ref.shape[0] - tm * sr, tm * sr)]
        pltpu.make_async_copy(ybuf.at[1], tail, ysem.at[1]).start()
        pltpu.make_async_copy(ybuf.at[1], tail, ysem.at[1]).wait()

    @pl.when(i + 1 < nu)
    def _():
        fetch(i + 1, (i + 1) % 2).start()

    @pl.when(i < nu)
    def _():
        @pl.when((i == 0) | (te_ref[i] != te_ref[jnp.maximum(i - 1, 0)]))
        def _():
            wg_bf[...] = wg_ref[0].astype(BF16)
            wu_bf[...] = wu_ref[0].astype(BF16)
            wd_bf[...] = wd_ref[0].astype(BF16)

        slot = i % 2
        fetch(i, slot).wait()
        rows = lax.broadcasted_iota(jnp.int32, (tm, 1), 0)
        x = jnp.where(rows < tv_ref[i], _load_slabs(xbuf.at[slot], tm, d), 0.0).astype(BF16)
        g = _dot(x, wg_bf[...])
        act = (g * jax.nn.sigmoid(g) * _dot(x, wu_bf[...])).astype(BF16)
        _store_slabs(ybuf.at[slot], _dot(act, wd_bf[...]))

        @pl.when(i > 0)
        def _():
            put(i - 1, 1 - slot).wait()

        put(i, slot).start()

        @pl.when(i == nu - 1)
        def _():
            put(i, slot).wait()


def _experts(tile_e, tile_valid, tile_src, n_used, xs, w_gate, w_up, w_down, tm, n_tiles):
    E, D, F = w_gate.shape
    sr = _slab_rows(D)

    def expert(i, te, tv, ts, nu):
        return (te[jnp.minimum(i, nu[0] - 1)], 0, 0)

    return pl.pallas_call(
        _experts_kernel,
        grid_spec=pltpu.PrefetchScalarGridSpec(
            num_scalar_prefetch=4,
            grid=(n_tiles,),
            in_specs=[pl.BlockSpec(memory_space=pl.ANY), pl.BlockSpec((1, D, F), expert),
                      pl.BlockSpec((1, D, F), expert), pl.BlockSpec((1, F, D), expert)],
            out_specs=pl.BlockSpec(memory_space=pl.ANY),
            scratch_shapes=[pltpu.VMEM((2, tm * sr, LANES), F32), pltpu.VMEM((2, tm * sr, LANES), F32),
                            pltpu.VMEM((D, F), BF16), pltpu.VMEM((D, F), BF16), pltpu.VMEM((F, D), BF16),
                            pltpu.SemaphoreType.DMA((2,)), pltpu.SemaphoreType.DMA((2,))],
        ),
        out_shape=jax.ShapeDtypeStruct(xs.shape, F32),
        compiler_params=_cparams(("arbitrary",)),
        name="experts",
    )(tile_e, tile_valid, tile_src, n_used, xs, w_gate, w_up, w_down)


def _combine_kernel(dest_ref, nxt_ref, w_ref, h_ref, ys_ref, sg_ref, su_ref, sd_ref, g_ref, b_ref, out_ref,
                    buf, routed_ref, sem):
    i = pl.program_id(0)
    last = pl.num_programs(0) - 1
    tt, d = h_ref.shape
    sr = _slab_rows(d)
    slot = i % 2
    other = 1 - slot

    def start_row(idx_ref, t, to):
        for k in range(TOP_K):
            src = ys_ref.at[pl.ds(pl.multiple_of(idx_ref[k, t] * sr, sr), sr)]
            dst = buf.at[to, k, pl.ds(pl.multiple_of(t * sr, sr), sr)]
            pltpu.make_async_copy(src, dst, sem.at[to]).start(priority=k % 2)

    def wait_slot(s):
        for k in range(TOP_K):
            pltpu.make_async_copy(ys_ref.at[pl.ds(0, tt * sr)], buf.at[s, k], sem.at[s]).wait()

    @pl.when(i == 0)
    def _():
        def prime(t, c):
            start_row(dest_ref, t, 0)
            return c

        lax.fori_loop(0, tt, prime, 0)

    wait_slot(slot)

    def group(gi, carry):
        t0 = pl.multiple_of(gi * COMBINE_GROUP, COMBINE_GROUP)
        for j in range(COMBINE_GROUP):
            start_row(nxt_ref, t0 + j, other)
        w = w_ref[pl.ds(t0, COMBINE_GROUP), :]
        wk = [jnp.broadcast_to(w[:, k:k + 1], (COMBINE_GROUP, LANES)) for k in range(TOP_K)]
        for c in range(sr):
            acc = buf[slot, 0, pl.ds(t0 * sr + c, COMBINE_GROUP, stride=sr), :] * wk[0]
            for k in range(1, TOP_K):
                acc = acc + buf[slot, k, pl.ds(t0 * sr + c, COMBINE_GROUP, stride=sr), :] * wk[k]
            routed_ref[pl.ds(t0, COMBINE_GROUP), c * LANES:(c + 1) * LANES] = acc
        return carry

    lax.fori_loop(0, tt // COMBINE_GROUP, group, 0)

    h = h_ref[...]
    hb = h.astype(BF16)
    gate = _dot(hb, sg_ref[...])
    act = (gate * jax.nn.sigmoid(gate) * _dot(hb, su_ref[...])).astype(BF16)
    shared = _dot(act, sd_ref[...])
    out_ref[...] = _layer_norm(DN_ALPHA * h + (routed_ref[...] + shared), g_ref[...], b_ref[...])

    @pl.when(i == last)
    def _():
        wait_slot(other)


def _combine(dest, w_tok, h2, ys, sg, su, sd, g, b, tt):
    T, D = h2.shape
    sr = _slab_rows(D)
    row = lambda i: (i, 0)
    const = lambda i: (0, 0)
    n = T // tt
    return pl.pallas_call(
        _combine_kernel,
        grid=(n,),
        in_specs=[pl.BlockSpec((TOP_K, tt), lambda i: (0, i), memory_space=pltpu.SMEM),
                  pl.BlockSpec((TOP_K, tt), lambda i: (0, jnp.minimum(i + 1, n - 1)), memory_space=pltpu.SMEM),
                  pl.BlockSpec((tt, TOP_K), row), pl.BlockSpec((tt, D), row),
                  pl.BlockSpec(memory_space=pl.ANY),
                  pl.BlockSpec(sg.shape, const), pl.BlockSpec(su.shape, const), pl.BlockSpec(sd.shape, const),
                  pl.BlockSpec((1, D), const), pl.BlockSpec((1, D), const)],
        out_specs=pl.BlockSpec((tt, D), row),
        out_shape=jax.ShapeDtypeStruct((T, D), F32),
        scratch_shapes=[pltpu.VMEM((2, TOP_K, tt * sr, LANES), F32), pltpu.VMEM((tt, D), F32),
                        pltpu.SemaphoreType.DMA((2,))],
        compiler_params=_cparams(("arbitrary",)),
        name="combine",
    )(dest, dest, w_tok, h2, ys, sg, su, sd, g, b)


def _rope_lane_tables(positions):
    half = ROT_DIM // 2
    inv = ROPE_THETA ** (-jnp.arange(0, ROT_DIM, 2, dtype=F32) / ROT_DIM)
    ang = positions.astype(F32).reshape(-1, 1) * inv
    cos = jnp.tile(jnp.cos(ang), (1, LANES // half))
    sin = jnp.tile(jnp.sin(ang), (1, LANES // half))
    j = jnp.arange(LANES) % A_HEAD_DIM
    cc = jnp.where(j < ROT_DIM, cos, 1.0)
    s1 = jnp.where(j < half, -sin, 0.0)
    s2 = jnp.where((j >= half) & (j < ROT_DIM), sin, 0.0)
    return cc, s1, s2


def _layer(x, mem, cc, s1, s2, l, w_in, w_out, lq1, lk1, lq2, lk2, subln_g, ln1_g, ln1_b,
           wq_mem, wkv_mem, wo_mem, ln2_g, ln2_b, w_router, e_bias, w_gate, w_up, w_down,
           ws_gate, ws_up, ws_down, ln3_g, ln3_b):
    B, S, D = x.shape
    T = B * S
    lambda_init = 0.8 - 0.6 * math.exp(-0.3 * l)
    x2 = x.reshape(T, D)
    tm = min(512, S)

    w_bf = w_in.astype(BF16)
    o_vb = 3 * A_WIDTH + 2 * B_QK_WIDTH
    dils = tuple(d for _, d in A_PAIRS if d > 1)
    qa, ka, va, qb, kb, vbt, *strided = _in_proj(x2, w_bf[:, :o_vb], w_bf[:, o_vb:].T, cc, s1, s2, tm, S, dils)
    strided_qkv = {d: strided[3 * n:3 * n + 3] for n, d in enumerate(dils)}

    outs, lses = [], []
    for window, d in A_PAIRS:
        assert window // d == BAND
        L = S // d
        qkv = (qa, ka, va) if d == 1 else strided_qkv[d]
        o, lse = _dilated(*(t.reshape(B * d, L, A_WIDTH) for t in qkv), min(2 * BAND, L))
        shape = (lambda w: (T, w)) if d == 1 else (lambda w: (B, d, L, w))
        outs.append(o.reshape(shape(A_WIDTH)))
        lses.append(lse.reshape(shape(LANES)))

    row = lambda v: v.reshape(1, -1).astype(F32)
    ob = _diff_attention(qb.reshape(B, S, -1), kb.reshape(B, S, -1), vbt, row(lq1), row(lk1), row(lq2), row(lk2),
                         subln_g.reshape(-1, 1).astype(F32), lambda_init, min(1024, S), min(1024, S))

    expand = (jnp.arange(LANES)[:, None] == (jnp.arange(A_WIDTH)[None, :] // A_HEAD_DIM)).astype(BF16)
    M = mem.shape[1]
    kv = _mem_kv(mem.reshape(B * M, D), wkv_mem.astype(BF16), min(512, B * M))
    wr_t = w_router.T
    wr_hi = wr_t.astype(BF16)
    wr_lo = (wr_t - wr_hi.astype(F32)).astype(BF16)
    h2, h2_slabs, idx, w_route, rank, counts = _token_stage(
        outs, lses, ob.reshape(T, -1), x2, w_out.astype(BF16), expand, row(ln1_g), row(ln1_b),
        kv, wq_mem.astype(BF16), wo_mem.astype(BF16), row(ln2_g), row(ln2_b),
        wr_hi, wr_lo, e_bias.reshape(-1, 1).astype(F32), tm, S // tm)

    tile_rows = 512
    counts = counts.reshape(-1)
    padded = (counts + tile_rows - 1) // tile_rows * tile_rows
    pend = jnp.cumsum(padded)
    pstart = pend - padded
    n_tiles = (T * TOP_K) // tile_rows + N_EXPERTS
    tile_lo = jnp.arange(n_tiles, dtype=jnp.int32) * tile_rows
    tile_e = jnp.minimum(jnp.sum(pend[None, :] <= tile_lo[:, None], axis=1), N_EXPERTS - 1).astype(jnp.int32)
    tile_off = tile_lo - pstart[tile_e]
    tile_valid = jnp.clip(counts[tile_e] - tile_off, 0, tile_rows).astype(jnp.int32)
    n_used = (pend[-1] // tile_rows).astype(jnp.int32).reshape(1)
    cstart = jnp.cumsum(counts) - counts
    tile_src = jnp.clip(cstart[tile_e] + tile_off, 0, T * TOP_K).astype(jnp.int32)

    dest = _dest_rows(idx, rank, cstart.reshape(-1, 1).astype(jnp.int32), min(2048, T))
    xs = _dispatch(dest, h2_slabs, tile_rows, min(512, S), _slab_rows(D))
    ys = _experts(tile_e, tile_valid, tile_src, n_used, xs, w_gate, w_up, w_down, tile_rows, n_tiles)
    out = _combine(dest, w_route.T, h2, ys, ws_gate.astype(BF16), ws_up.astype(BF16), ws_down.astype(BF16),
                   row(ln3_g), row(ln3_b), min(512, S))
    return out.reshape(B, S, D)


def kernel(x, mem, positions, w_in, w_out, lambda_q1, lambda_k1, lambda_q2, lambda_k2, subln_g, ln1_g, ln1_b,
           wq_mem, wkv_mem, wo_mem, ln2_g, ln2_b, w_router, e_bias, w_gate, w_up, w_down, ws_gate, ws_up, ws_down,
           ln3_g, ln3_b):
    cc, s1, s2 = _rope_lane_tables(positions)
    h = x
    for l in range(w_in.shape[0]):
        h = _layer(h, mem, cc, s1, s2, l, w_in[l], w_out[l], lambda_q1[l], lambda_k1[l], lambda_q2[l],
                   lambda_k2[l], subln_g[l], ln1_g[l], ln1_b[l], wq_mem[l], wkv_mem[l], wo_mem[l],
                   ln2_g[l], ln2_b[l], w_router[l], e_bias[l], w_gate[l], w_up[l], w_down[l],
                   ws_gate[l], ws_up[l], ws_down[l], ln3_g[l], ln3_b[l])
    return h
```
